```python
import math
import jax, jax.numpy as jnp
from jax import lax
import numpy as np

D_MODEL = 4096
BATCH = 16
SEQ = 256
DEPTH = 2
DEC_BATCH = 8
DEC_SEQ = 4096
PAST_LEN = 512

GRID_W = 64
HEAD_DIM = 128
H_A = D_MODEL // 4 // HEAD_DIM
MIX_A = H_A * HEAD_DIM
NA_ROWS = 8
NA_COLS = 16
NA_QCOLS = 16
NA_KCOLS = 2 * NA_COLS
H_B = D_MODEL // 4 // HEAD_DIM
DB_HALF = HEAD_DIM // 2
MIX_B = H_B * HEAD_DIM
H_C = D_MODEL // 2 // HEAD_DIM
KV_C = H_C // 4
G_C = H_C // KV_C
MIX_C = H_C * HEAD_DIM
KVW_C = KV_C * HEAD_DIM
WINDOW_C = 128
N_BRANCH = 3
PROJ_WIDTHS = (MIX_A, MIX_A, MIX_A, MIX_B, MIX_B, MIX_B, MIX_C, KVW_C, KVW_C, N_BRANCH * D_MODEL)
IN_W = 3 * MIX_A + 3 * MIX_B + MIX_C + 2 * KVW_C + N_BRANCH * D_MODEL
MIX_W = MIX_A + MIX_B + MIX_C
D_FF = ((8 * D_MODEL // 3 + 255) // 256) * 256
Q_BLOCK = 128
ROPE_BASE = 10000.0
LN_EPS = 1e-5
RMS_EPS = 1e-5
ALPHA = (2 * DEPTH) ** 0.25
BETA = (8 * DEPTH) ** -0.25
NEG_INF = -1e30

kernel_name = 'hybrid_diffusion_trunk_ctx_and_denoise_step'


def layer_norm(x, g, b):
    xf = x.astype(jnp.float32)
    mu = jnp.mean(xf, axis=-1, keepdims=True)
    var = jnp.mean(jnp.square(xf - mu), axis=-1, keepdims=True)
    return ((xf - mu) * lax.rsqrt(var + LN_EPS) * g.astype(jnp.float32) + b.astype(jnp.float32)).astype(x.dtype)


def ada_modulation(cvec, w_ada_l, b_ada_l):
    m = jax.nn.silu(cvec) @ w_ada_l + b_ada_l
    return m.reshape(cvec.shape[0], 1, 6, D_MODEL)


def modulate(x, mod, i):
    return x * (1 + mod[..., 3 * i + 1, :]) + mod[..., 3 * i, :]


def post_norm_residual(x, y, mod, i, g, b):
    return layer_norm(ALPHA * x + mod[..., 3 * i + 2, :] * y, g, b)


def rope_2d(x):
    t, d = x.shape[1], x.shape[-1]
    half, quarter = d // 2, d // 4
    pos = jnp.arange(t)
    row = (pos // GRID_W).astype(jnp.float32)
    col = (pos % GRID_W).astype(jnp.float32)
    inv = jnp.exp(-math.log(ROPE_BASE) * jnp.arange(quarter, dtype=jnp.float32) / quarter)
    bshape = (1, t) + (1,) * (x.ndim - 3) + (quarter,)

    def rot(xp, p):
        ang = (p[:, None] * inv[None, :]).reshape(bshape)
        cos, sin = jnp.cos(ang), jnp.sin(ang)
        x1, x2 = xp[..., :quarter], xp[..., quarter:]
        return jnp.concatenate([x1 * cos - x2 * sin, x2 * cos + x1 * sin], axis=-1)

    xf = x.astype(jnp.float32)
    return jnp.concatenate([rot(xf[..., :half], row), rot(xf[..., half:], col)], axis=-1).astype(x.dtype)


def map_query_blocks(fn, q):
    b, t = q.shape[:2]
    qb = jnp.moveaxis(q.reshape((b, t // Q_BLOCK, Q_BLOCK) + q.shape[2:]), 1, 0)
    out = jnp.moveaxis(lax.map(fn, qb), 0, 1)
    return out.reshape((b, t) + out.shape[3:])


def project(h, w_in_l):
    b, t = h.shape[:2]
    z = h @ w_in_l
    bounds = np.cumsum(PROJ_WIDTHS)[:-1].tolist()
    qa, ka, va, qb, kb, vb, qc, kc, vc, g = jnp.split(z, bounds, axis=-1)
    return (qa.reshape(b, t, H_A, HEAD_DIM), ka.reshape(b, t, H_A, HEAD_DIM), va.reshape(b, t, H_A, HEAD_DIM),
            qb.reshape(b, t, H_B, 2, DB_HALF), kb.reshape(b, t, H_B, 2, DB_HALF), vb.reshape(b, t, H_B, HEAD_DIM),
            qc.reshape(b, t, KV_C, G_C, HEAD_DIM), kc.reshape(b, t, KV_C, HEAD_DIM), vc.reshape(b, t, KV_C, HEAD_DIM),
            g.reshape(b, t, N_BRANCH, D_MODEL))


def dense_attn(q, k, v):
    s = jnp.einsum('bqhd,bkhd->bhqk', q, k, preferred_element_type=jnp.float32) * (q.shape[-1] ** -0.5)
    p = jax.nn.softmax(s, axis=-1).astype(v.dtype)
    return jnp.einsum('bhqk,bkhd->bqhd', p, v)


def diff_lambda(lam_l, layer):
    lam_init = 0.8 - 0.6 * math.exp(-0.3 * layer)
    lf = lam_l.astype(jnp.float32)
    lam = jnp.exp(jnp.sum(lf[0] * lf[1])) - jnp.exp(jnp.sum(lf[2] * lf[3])) + lam_init
    return lam, lam_init


def diff_core(q, k, v, lam):
    s = jnp.einsum('bqhid,bkhid->ibhqk', q, k, preferred_element_type=jnp.float32) * (DB_HALF ** -0.5)
    p = jax.nn.softmax(s, axis=-1)
    w = (p[0] - lam * p[1]).astype(v.dtype)
    return jnp.einsum('bhqk,bkhd->bqhd', w, v)


def diff_finish(o, gain, lam_init):
    of = o.astype(jnp.float32)
    of = of * lax.rsqrt(jnp.mean(jnp.square(of), axis=-1, keepdims=True) + RMS_EPS)
    return (of * gain.astype(jnp.float32) * (1.0 - lam_init)).astype(o.dtype)


def sink_column(sink, s):
    return jnp.broadcast_to(sink.astype(jnp.float32)[None, :, :, None, None], s.shape[:-1] + (1,))


def gqa_sink_dense(q, k, v, sink):
    s = jnp.einsum('bqngd,bknd->bngqk', q, k, preferred_element_type=jnp.float32) * (HEAD_DIM ** -0.5)
    p = jax.nn.softmax(jnp.concatenate([s, sink_column(sink, s)], axis=-1), axis=-1)
    return jnp.einsum('bngqk,bknd->bqngd', p[..., :-1].astype(v.dtype), v)


def window_gqa_latent(q, k, v, k_ctx, v_ctx, sink):
    b, t = q.shape[:2]
    nb = t // Q_BLOCK
    band = 3 * Q_BLOCK
    n_ctx = k_ctx.shape[1]
    pad = ((0, 0), (Q_BLOCK, Q_BLOCK), (0, 0), (0, 0))
    k_pad, v_pad = jnp.pad(k, pad), jnp.pad(v, pad)
    qoff = jnp.arange(Q_BLOCK)
    koff = jnp.arange(band)
    rel_ok = jnp.abs(qoff[:, None] + Q_BLOCK - koff[None, :]) <= WINDOW_C
    scale = HEAD_DIM ** -0.5

    def blk(args):
        i, qi = args
        kb = lax.dynamic_slice_in_dim(k_pad, i * Q_BLOCK, band, axis=1)
        vb = lax.dynamic_slice_in_dim(v_pad, i * Q_BLOCK, band, axis=1)
        kpos = (i - 1) * Q_BLOCK + koff
        valid = rel_ok & ((kpos >= 0) & (kpos < t))[None, :]
        s_loc = jnp.einsum('bqngd,bknd->bngqk', qi, kb, preferred_element_type=jnp.float32) * scale
        s_loc = jnp.where(valid, s_loc, NEG_INF)
        s_ctx = jnp.einsum('bqngd,bknd->bngqk', qi, k_ctx, preferred_element_type=jnp.float32) * scale
        p = jax.nn.softmax(jnp.concatenate([s_loc, s_ctx, sink_column(sink, s_loc)], axis=-1), axis=-1).astype(v.dtype)
        return (jnp.einsum('bngqk,bknd->bqngd', p[..., :band], vb)
                + jnp.einsum('bngqk,bknd->bqngd', p[..., band:band + n_ctx], v_ctx))

    qb = jnp.moveaxis(q.reshape(b, nb, Q_BLOCK, KV_C, G_C, HEAD_DIM), 1, 0)
    out = lax.map(blk, (jnp.arange(nb), qb))
    return jnp.moveaxis(out, 0, 1).reshape(b, t, KV_C, G_C, HEAD_DIM)


def neighbourhood_attn_latent(q, k, v, k_ctx, v_ctx, rpb_l):
    b, t, h, d = q.shape
    rows = t // GRID_W
    kr = min(NA_ROWS, rows)
    ncb = GRID_W // NA_QCOLS
    n_loc = kr * NA_KCOLS
    col_start = np.clip(np.arange(ncb) * NA_QCOLS - NA_COLS // 2, 0, GRID_W - NA_KCOLS)
    col_idx = col_start[:, None] + np.arange(NA_KCOLS)
    qcol = np.arange(ncb)[:, None] * NA_QCOLS + np.arange(NA_QCOLS)
    win_start = np.clip(qcol - NA_COLS // 2, 0, GRID_W - NA_COLS)
    kc = col_idx[:, None, :]
    col_valid = (kc >= win_start[..., None]) & (kc < win_start[..., None] + NA_COLS)
    dc_idx = np.clip(kc - qcol[..., None] + NA_COLS - 1, 0, 2 * NA_COLS - 2)
    scale = d ** -0.5
    kg = k.reshape(b, rows, GRID_W, h, d)
    vg = v.reshape(b, rows, GRID_W, h, d)

    def row_block(args):
        r, qr = args
        rs = jnp.clip(r - kr // 2, 0, rows - kr)
        k_blk = lax.dynamic_slice_in_dim(kg, rs, kr, axis=1)[:, :, col_idx]
        v_blk = lax.dynamic_slice_in_dim(vg, rs, kr, axis=1)[:, :, col_idx]
        s_loc = jnp.einsum('bjqhd,brjchd->bhjqrc', qr, k_blk, preferred_element_type=jnp.float32) * scale
        dr_idx = rs + jnp.arange(kr) - r + NA_ROWS - 1
        bias = rpb_l[:, dr_idx][:, :, dc_idx].transpose(0, 2, 3, 1, 4).astype(jnp.float32)
        s_loc = jnp.where(col_valid[:, :, None, :], s_loc + bias, NEG_INF).reshape(b, h, ncb, NA_QCOLS, n_loc)
        s_ctx = jnp.einsum('bjqhd,bchd->bhjqc', qr, k_ctx, preferred_element_type=jnp.float32) * scale
        p = jax.nn.softmax(jnp.concatenate([s_loc, s_ctx], axis=-1), axis=-1).astype(v.dtype)
        p_loc = p[..., :n_loc].reshape(b, h, ncb, NA_QCOLS, kr, NA_KCOLS)
        out = (jnp.einsum('bhjqrc,brjchd->bjqhd', p_loc, v_blk)
               + jnp.einsum('bhjqc,bchd->bjqhd', p[..., n_loc:], v_ctx))
        return out.reshape(b, GRID_W, h, d)

    qg = jnp.moveaxis(q.reshape(b, rows, ncb, NA_QCOLS, h, d), 1, 0)
    out = lax.map(row_block, (jnp.arange(rows), qg))
    return jnp.moveaxis(out, 0, 1).reshape(b, t, h, d)


def merge_branches(o_a, o_b, o_c, g, w_branch_l, w_out_l):
    b, t = o_a.shape[:2]
    gate = jax.nn.sigmoid(g.astype(jnp.float32)).astype(o_a.dtype)
    y_a = o_a.reshape(b, t, MIX_A) @ w_branch_l[:MIX_A]
    y_b = o_b.reshape(b, t, MIX_B) @ w_branch_l[MIX_A:MIX_A + MIX_B]
    y_c = o_c.reshape(b, t, MIX_C) @ w_branch_l[MIX_A + MIX_B:]
    return (gate[..., 0, :] * y_a + gate[..., 1, :] * y_b + gate[..., 2, :] * y_c) @ w_out_l


def context_mixer(h, w_in_l, lam, lam_init, subln_l, sink, w_branch_l, w_out_l):
    qa, ka, va, qb, kb, vb, qc, kc, vc, g = project(h, w_in_l)
    o_a = map_query_blocks(lambda qq: dense_attn(qq, ka, va), qa)
    o_b = diff_finish(map_query_blocks(lambda qq: diff_core(qq, kb, vb, lam), qb), subln_l, lam_init)
    o_c = map_query_blocks(lambda qq: gqa_sink_dense(qq, kc, vc, sink), qc)
    return merge_branches(o_a, o_b, o_c, g, w_branch_l, w_out_l), (ka, va, kb, vb, kc, vc)


def latent_mixer(h, cka, cva, ckb, cvb, ckc, cvc, w_in_l, rpb_l, lam, lam_init, subln_l, sink, w_branch_l, w_out_l):
    qa, ka, va, qb, kb, vb, qc, kc, vc, g = project(h, w_in_l)
    o_a = neighbourhood_attn_latent(qa, ka, va, cka, cva, rpb_l)
    kb_all = jnp.concatenate([rope_2d(kb), ckb], axis=1)
    vb_all = jnp.concatenate([vb, cvb], axis=1)
    o_b = diff_finish(map_query_blocks(lambda qq: diff_core(qq, kb_all, vb_all, lam), rope_2d(qb)), subln_l, lam_init)
    o_c = window_gqa_latent(rope_2d(qc), rope_2d(kc), vc, ckc, cvc, sink)
    return merge_branches(o_a, o_b, o_c, g, w_branch_l, w_out_l)


def swiglu(h, w_gate_up_l, w_down_l):
    gu = h @ w_gate_up_l
    return (jax.nn.silu(gu[..., :D_FF]) * gu[..., D_FF:]) @ w_down_l


def setup_inputs(seed: int = 0) -> dict:
    key = jax.random.key(seed)
    ks = jax.random.split(key, 26)

    def nrm(k, shape, s=1.0):
        return s * jax.random.normal(k, shape, jnp.float32)

    w_branch = jnp.concatenate([
        nrm(ks[15], (DEPTH, MIX_A, D_MODEL), MIX_A ** -0.5),
        nrm(ks[16], (DEPTH, MIX_B, D_MODEL), MIX_B ** -0.5),
        nrm(ks[17], (DEPTH, MIX_C, D_MODEL), MIX_C ** -0.5)], axis=1)
    return {
        'x_prompt': nrm(ks[0], (BATCH, SEQ, D_MODEL)),
        'x_sample': nrm(ks[1], (DEC_BATCH, DEC_SEQ, D_MODEL)),
        'cache_a_k': nrm(ks[2], (DEC_BATCH, DEPTH, PAST_LEN, H_A, HEAD_DIM)),
        'cache_a_v': nrm(ks[3], (DEC_BATCH, DEPTH, PAST_LEN, H_A, HEAD_DIM)),
        'cache_b_k': nrm(ks[4], (DEC_BATCH, DEPTH, PAST_LEN, H_B, 2, DB_HALF)),
        'cache_b_v': nrm(ks[5], (DEC_BATCH, DEPTH, PAST_LEN, H_B, HEAD_DIM)),
        'cache_c_k': nrm(ks[6], (DEC_BATCH, DEPTH, PAST_LEN, KV_C, HEAD_DIM)),
        'cache_c_v': nrm(ks[7], (DEC_BATCH, DEPTH, PAST_LEN, KV_C, HEAD_DIM)),
        'c': nrm(ks[8], (DEC_BATCH, D_MODEL)),
        'c_ctx': nrm(ks[9], (D_MODEL,)),
        'w_ada': nrm(ks[10], (DEPTH, D_MODEL, 6 * D_MODEL), 0.5 * D_MODEL ** -0.5),
        'b_ada': nrm(ks[11], (DEPTH, 6 * D_MODEL), 0.02),
        'w_in': nrm(ks[12], (DEPTH, D_MODEL, IN_W), D_MODEL ** -0.5),
        'rpb_a': nrm(ks[13], (DEPTH, H_A, 2 * NA_ROWS - 1, 2 * NA_COLS - 1), 0.5),
        'lam_b': nrm(ks[14], (DEPTH, 4, DB_HALF), 0.1),
        'subln_b': 1.0 + nrm(ks[18], (DEPTH, HEAD_DIM), 0.02),
        'sink_c': nrm(ks[19], (DEPTH, H_C)),
        'w_branch': w_branch,
        'w_out': nrm(ks[20], (DEPTH, D_MODEL, D_MODEL), BETA * D_MODEL ** -0.5),
        'ln_g': 1.0 + nrm(ks[21], (DEPTH, 2, D_MODEL), 0.02),
        'ln_b': nrm(ks[22], (DEPTH, 2, D_MODEL), 0.02),
        'w_gate_up': nrm(ks[23], (DEPTH, D_MODEL, 2 * D_FF), D_MODEL ** -0.5),
        'w_down': nrm(ks[24], (DEPTH, D_FF, D_MODEL), BETA * D_FF ** -0.5),
    }


def reference(x_prompt, x_sample, cache_a_k, cache_a_v, cache_b_k, cache_b_v, cache_c_k, cache_c_v, c, c_ctx,
              w_ada, b_ada, w_in, rpb_a, lam_b, subln_b, sink_c, w_branch, w_out, ln_g, ln_b, w_gate_up, w_down):
    xp, xs = x_prompt, x_sample
    new_lists = ([], [], [], [], [], [])
    for l in range(DEPTH):
        lam, lam_init = diff_lambda(lam_b[l], l)
        sink = sink_c[l].reshape(KV_C, G_C)
        mod_p = ada_modulation(c_ctx[None, :], w_ada[l], b_ada[l])
        y, ctx_kv = context_mixer(modulate(xp, mod_p, 0), w_in[l], lam, lam_init, subln_b[l], sink, w_branch[l], w_out[l])
        xp = post_norm_residual(xp, y, mod_p, 0, ln_g[l, 0], ln_b[l, 0])
        xp = post_norm_residual(xp, swiglu(modulate(xp, mod_p, 1), w_gate_up[l], w_down[l]), mod_p, 1, ln_g[l, 1], ln_b[l, 1])
        for lst, arr in zip(new_lists, ctx_kv):
            lst.append(arr)
        mod_s = ada_modulation(c, w_ada[l], b_ada[l])
        y = latent_mixer(modulate(xs, mod_s, 0), cache_a_k[:, l], cache_a_v[:, l], cache_b_k[:, l], cache_b_v[:, l],
                         cache_c_k[:, l], cache_c_v[:, l], w_in[l], rpb_a[l], lam, lam_init, subln_b[l], sink,
                         w_branch[l], w_out[l])
        xs = post_norm_residual(xs, y, mod_s, 0, ln_g[l, 0], ln_b[l, 0])
        xs = post_norm_residual(xs, swiglu(modulate(xs, mod_s, 1), w_gate_up[l], w_down[l]), mod_s, 1, ln_g[l, 1], ln_b[l, 1])
    new_a_k = jnp.stack(new_lists[0], axis=1)
    new_a_v = jnp.stack(new_lists[1], axis=1)
    new_b_k = jnp.stack(new_lists[2], axis=1)
    new_b_v = jnp.stack(new_lists[3], axis=1)
    new_c_k = jnp.stack(new_lists[4], axis=1)
    new_c_v = jnp.stack(new_lists[5], axis=1)
    return (xp, xs, new_a_k, new_a_v, new_b_k, new_b_v, new_c_k, new_c_v)
```

```python
import functools
import math

import numpy as np
import jax
import jax.numpy as jnp
from jax import lax
from jax.experimental import pallas as pl
from jax.experimental.pallas import tpu as pltpu

D_MODEL = 4096
DEPTH = 2
GRID_W = 64
HEAD_DIM = 128
H_A = 8
MIX_A = H_A * HEAD_DIM
NA_ROWS = 8
NA_COLS = 16
H_B = 8
DB_HALF = HEAD_DIM // 2
MIX_B = H_B * HEAD_DIM
H_C = 16
KV_C = 4
G_C = H_C // KV_C
MIX_C = H_C * HEAD_DIM
KVW_C = KV_C * HEAD_DIM
WINDOW_C = 128
N_BRANCH = 3
QKV_W = 3 * MIX_A + 3 * MIX_B + MIX_C + 2 * KVW_C
D_FF = 11008
ROPE_BASE = 10000.0
LN_EPS = 1e-5
RMS_EPS = 1e-5
ALPHA = (2 * DEPTH) ** 0.25
NEG_INF = -1e30
BF16 = jnp.bfloat16
F32 = jnp.float32

OFF_QA, OFF_KA, OFF_VA = 0, MIX_A, 2 * MIX_A
OFF_QB, OFF_KB, OFF_VB = 3 * MIX_A, 3 * MIX_A + MIX_B, 3 * MIX_A + 2 * MIX_B
OFF_QC = 3 * MIX_A + 3 * MIX_B
OFF_KC = OFF_QC + MIX_C
OFF_VC = OFF_KC + KVW_C

V7X_VMEM_LIMIT = 56 * 1024 * 1024

NA_QROWS = 8
NA_KROWS = 16
SW_TQ = 512
SW_SPAN = SW_TQ + 2 * WINDOW_C


def _params(*sem):
    return pltpu.CompilerParams(dimension_semantics=sem, vmem_limit_bytes=V7X_VMEM_LIMIT)


def _dot(a, b):
    return jnp.dot(a, b, preferred_element_type=F32)


def _dot_nt(a, b):
    return lax.dot_general(a, b, (((1,), (1,)), ((), ())), preferred_element_type=F32)


def _ada_kernel(c_ref, w_ref, b_ref, o_ref):
    cv = c_ref[...]
    s = (cv * jax.nn.sigmoid(cv)).astype(BF16)
    o_ref[...] = _dot(s, w_ref[...].astype(BF16)) + b_ref[...]


def ada_modulation_all(cc, w_ada, b_ada):
    r = cc.shape[0]
    n = 6 * D_MODEL
    tn = 512
    out = pl.pallas_call(
        _ada_kernel,
        grid=(DEPTH, n // tn),
        in_specs=[
            pl.BlockSpec((r, D_MODEL), lambda l, j: (0, 0)),
            pl.BlockSpec((None, D_MODEL, tn), lambda l, j: (l, 0, j)),
            pl.BlockSpec((None, 1, tn), lambda l, j: (l, 0, j)),
        ],
        out_specs=pl.BlockSpec((None, r, tn), lambda l, j: (l, 0, j)),
        out_shape=jax.ShapeDtypeStruct((DEPTH, r, n), F32),
        compiler_params=_params("arbitrary", "arbitrary"),
        name="ada_modulation",
    )(cc, w_ada, b_ada.reshape(DEPTH, 1, n))
    return out.reshape(DEPTH, r, 6, 1, D_MODEL)


def _mod_spec(layer, chunk, rowfn):
    return pl.BlockSpec((None, None, None, 1, D_MODEL), lambda i: (layer, rowfn(i), chunk, 0, 0))


def _modulate_kernel(x_ref, sc_ref, sh_ref, o_ref):
    o_ref[...] = (x_ref[...] * (1.0 + sc_ref[...]) + sh_ref[...]).astype(BF16)


def modulate_cast(x, mod, layer, idx, rowfn_of_tile, tm=512):
    m = x.shape[0]
    rowfn = functools.partial(rowfn_of_tile, tm)
    return pl.pallas_call(
        _modulate_kernel,
        grid=(m // tm,),
        in_specs=[
            pl.BlockSpec((tm, D_MODEL), lambda i: (i, 0)),
            _mod_spec(layer, 3 * idx + 1, rowfn),
            _mod_spec(layer, 3 * idx, rowfn),
        ],
        out_specs=pl.BlockSpec((tm, D_MODEL), lambda i: (i, 0)),
        out_shape=jax.ShapeDtypeStruct((m, D_MODEL), BF16),
        compiler_params=_params("arbitrary"),
        name="modulate",
    )(x, mod, mod)


def _ln_core(x_ref, y_ref, gate_ref, g_ref, b_ref):
    v = ALPHA * x_ref[...] + gate_ref[...] * y_ref[...]
    mu = jnp.mean(v, axis=-1, keepdims=True)
    d = v - mu
    var = jnp.mean(d * d, axis=-1, keepdims=True)
    return d * lax.rsqrt(var + LN_EPS) * g_ref[...] + b_ref[...]


def _ln_mod_kernel(x_ref, y_ref, gate_ref, g_ref, b_ref, sc_ref, sh_ref, xo_ref, ho_ref):
    xn = _ln_core(x_ref, y_ref, gate_ref, g_ref, b_ref)
    xo_ref[...] = xn
    ho_ref[...] = (xn * (1.0 + sc_ref[...]) + sh_ref[...]).astype(BF16)


def _ln_kernel(x_ref, y_ref, gate_ref, g_ref, b_ref, xo_ref):
    xo_ref[...] = _ln_core(x_ref, y_ref, gate_ref, g_ref, b_ref)


def post_norm_residual(x, y, mod, layer, idx, ln_g, ln_b, rowfn_of_tile, next_mod=None, tm=256):
    m = x.shape[0]
    rowfn = functools.partial(rowfn_of_tile, tm)
    row_spec = pl.BlockSpec((tm, D_MODEL), lambda i: (i, 0))
    vec_spec = pl.BlockSpec((1, D_MODEL), lambda i: (0, 0))
    in_specs = [row_spec, row_spec, _mod_spec(layer, 3 * idx + 2, rowfn), vec_spec, vec_spec]
    args = [x, y, mod, ln_g.reshape(1, D_MODEL), ln_b.reshape(1, D_MODEL)]
    if next_mod is None:
        return pl.pallas_call(
            _ln_kernel, grid=(m // tm,), in_specs=in_specs, out_specs=row_spec,
            out_shape=jax.ShapeDtypeStruct((m, D_MODEL), F32),
            compiler_params=_params("arbitrary"), name="post_norm",
        )(*args), None
    nl, nidx = next_mod
    in_specs += [_mod_spec(nl, 3 * nidx + 1, rowfn), _mod_spec(nl, 3 * nidx, rowfn)]
    args += [mod, mod]
    return pl.pallas_call(
        _ln_mod_kernel, grid=(m // tm,), in_specs=in_specs, out_specs=[row_spec, row_spec],
        out_shape=[jax.ShapeDtypeStruct((m, D_MODEL), F32), jax.ShapeDtypeStruct((m, D_MODEL), BF16)],
        compiler_params=_params("arbitrary"), name="post_norm_modulate",
    )(*args)


def _mm_kernel(x_ref, w_ref, o_ref, *, sigmoid):
    acc = _dot(x_ref[...], w_ref[...])
    if sigmoid:
        acc = jax.nn.sigmoid(acc)
    o_ref[...] = acc.astype(o_ref.dtype)


def matmul(x, w, out_dtype, tm, tn, sigmoid=False, name="matmul"):
    m, k = x.shape
    n = w.shape[1]
    return pl.pallas_call(
        functools.partial(_mm_kernel, sigmoid=sigmoid),
        grid=(m // tm, n // tn),
        in_specs=[pl.BlockSpec((tm, k), lambda i, j: (i, 0)), pl.BlockSpec((k, tn), lambda i, j: (0, j))],
        out_specs=pl.BlockSpec((tm, tn), lambda i, j: (i, j)),
        out_shape=jax.ShapeDtypeStruct((m, n), out_dtype),
        compiler_params=_params("arbitrary", "arbitrary"),
        name=name,
    )(x, w)


def _rope_tables(t, width):
    half, quarter = width // 2, width // 4
    lane = np.arange(HEAD_DIM)
    e = lane % width
    pos = np.arange(t)
    p = np.where((e < half)[None, :], (pos // GRID_W)[:, None], (pos % GRID_W)[:, None]).astype(np.float64)
    inv = np.exp(-math.log(ROPE_BASE) * (e % quarter).astype(np.float64) / quarter)
    ang = (p.astype(np.float32) * inv.astype(np.float32)[None, :]).astype(np.float64)
    first = ((e % half) < quarter)[None, :]
    cos, sin = np.cos(ang), np.sin(ang)
    return (jnp.asarray(cos, F32), jnp.asarray(np.where(first, -sin, 0.0), F32),
            jnp.asarray(np.where(first, 0.0, sin), F32))


def _rope(x, cos, s_lo, s_hi, quarter):
    return x * cos + pltpu.roll(x, HEAD_DIM - quarter, 1) * s_lo + pltpu.roll(x, quarter, 1) * s_hi


def _qkv_latent_kernel(x_ref, w_ref, cb_ref, lb_ref, hb_ref, cc_ref, lc_ref, hc_ref, o_ref, *, tn):
    j = pl.program_id(1)
    acc = _dot(x_ref[...], w_ref[...])
    nchunk = tn // HEAD_DIM
    tiles_qb = (OFF_QB // tn,)
    tiles_kb = (OFF_KB // tn,)
    tiles_qc = tuple(range(OFF_QC // tn, OFF_KC // tn))
    tile_kcvc = OFF_KC // tn

    def is_in(tiles):
        c = j == tiles[0]
        for tt in tiles[1:]:
            c = jnp.logical_or(c, j == tt)
        return c

    def store(fn_of_chunk):
        for ch in range(nchunk):
            sl = slice(ch * HEAD_DIM, (ch + 1) * HEAD_DIM)
            o_ref[:, sl] = fn_of_chunk(ch, acc[:, sl]).astype(o_ref.dtype)

    rope_b = lambda x: _rope(x, cb_ref[...], lb_ref[...], hb_ref[...], DB_HALF // 4)
    rope_c = lambda x: _rope(x, cc_ref[...], lc_ref[...], hc_ref[...], HEAD_DIM // 4)
    special = is_in(tiles_qb + tiles_kb + tiles_qc + (tile_kcvc,))

    @pl.when(jnp.logical_not(special))
    def _():
        o_ref[...] = acc.astype(o_ref.dtype)

    @pl.when(is_in(tiles_qb))
    def _():
        store(lambda ch, x: rope_b(x) * (DB_HALF ** -0.5))

    @pl.when(is_in(tiles_kb))
    def _():
        store(lambda ch, x: rope_b(x))

    @pl.when(is_in(tiles_qc))
    def _():
        store(lambda ch, x: rope_c(x))

    @pl.when(j == tile_kcvc)
    def _():
        store(lambda ch, x: rope_c(x) if ch < KVW_C // HEAD_DIM else x)


def qkv_latent(h, w_qkv, seq, tm=1024, tn=1024):
    m, k = h.shape
    assert tn == MIX_A and OFF_KC % tn == 0 and (OFF_VC - OFF_KC) == KVW_C and seq % tm == 0
    tabs = _rope_tables(seq, DB_HALF) + _rope_tables(seq, HEAD_DIM)
    per_seq = seq // tm
    tab_spec = pl.BlockSpec((tm, HEAD_DIM), lambda i, j: (i % per_seq, 0))
    return pl.pallas_call(
        functools.partial(_qkv_latent_kernel, tn=tn),
        grid=(m // tm, QKV_W // tn),
        in_specs=[pl.BlockSpec((tm, k), lambda i, j: (i, 0)), pl.BlockSpec((k, tn), lambda i, j: (0, j))]
        + [tab_spec] * 6,
        out_specs=pl.BlockSpec((tm, tn), lambda i, j: (i, j)),
        out_shape=jax.ShapeDtypeStruct((m, QKV_W), BF16),
        compiler_params=_params("arbitrary", "arbitrary"),
        name="qkv_latent",
    )(h, w_qkv, *tabs)


def _merge_kernel(oa_ref, ob_ref, oc_ref, wa_ref, wb_ref, wc_ref, ga_ref, gb_ref, gc_ref, o_ref):
    y = ga_ref[...].astype(F32) * _dot(oa_ref[...], wa_ref[...])
    y += gb_ref[...].astype(F32) * _dot(ob_ref[...], wb_ref[...])
    y += gc_ref[...].astype(F32) * _dot(oc_ref[...], wc_ref[...])
    o_ref[...] = y.astype(o_ref.dtype)


def merge_branches(o_parts, gates, w_branch, tm=1024, tn=512):
    (oa, ja), (ob, jb), (oc, jc) = o_parts
    m = gates.shape[0]
    nj = D_MODEL // tn
    return pl.pallas_call(
        _merge_kernel,
        grid=(m // tm, nj),
        in_specs=[
            pl.BlockSpec((tm, MIX_A), lambda i, j: (i, ja)),
            pl.BlockSpec((tm, MIX_B), lambda i, j: (i, jb)),
            pl.BlockSpec((tm, MIX_C), lambda i, j: (i, jc)),
            pl.BlockSpec((MIX_A, tn), lambda i, j: (0, j)),
            pl.BlockSpec((MIX_B, tn), lambda i, j: (MIX_A // MIX_B, j)),
            pl.BlockSpec((MIX_C, tn), lambda i, j: ((MIX_A + MIX_B) // MIX_C, j)),
            pl.BlockSpec((tm, tn), lambda i, j: (i, j)),
            pl.BlockSpec((tm, tn), lambda i, j: (i, nj + j)),
            pl.BlockSpec((tm, tn), lambda i, j: (i, 2 * nj + j)),
        ],
        out_specs=pl.BlockSpec((tm, tn), lambda i, j: (i, j)),
        out_shape=jax.ShapeDtypeStruct((m, D_MODEL), BF16),
        compiler_params=_params("arbitrary", "arbitrary"),
        name="merge_branches",
    )(oa, ob, oc, w_branch, w_branch, w_branch, gates, gates, gates)


def _swiglu_kernel(x_ref, wg_ref, wu_ref, o_ref):
    x = x_ref[...]
    g = _dot(x, wg_ref[...])
    u = _dot(x, wu_ref[...])
    o_ref[...] = (g * jax.nn.sigmoid(g) * u).astype(o_ref.dtype)


def swiglu_up(h, w_gate_up, tm=2048, tn=256):
    m, k = h.shape
    nj = D_FF // tn
    return pl.pallas_call(
        _swiglu_kernel,
        grid=(m // tm, nj),
        in_specs=[
            pl.BlockSpec((tm, k), lambda i, j: (i, 0)),
            pl.BlockSpec((k, tn), lambda i, j: (0, j)),
            pl.BlockSpec((k, tn), lambda i, j: (0, nj + j)),
        ],
        out_specs=pl.BlockSpec((tm, tn), lambda i, j: (i, j)),
        out_shape=jax.ShapeDtypeStruct((m, D_FF), BF16),
        compiler_params=_params("arbitrary", "arbitrary"),
        name="swiglu_up",
    )(h, w_gate_up, w_gate_up)


def _diff_lambda(lam_ref, layer):
    lam_init = 0.8 - 0.6 * math.exp(-0.3 * layer)
    lf = lam_ref[...]
    a = jnp.sum(lf[0:1, :] * lf[1:2, :], axis=-1, keepdims=True)
    b = jnp.sum(lf[2:3, :] * lf[3:4, :], axis=-1, keepdims=True)
    return jnp.exp(a) - jnp.exp(b) + lam_init, lam_init


def _diff_finish(o, gain_ref, lam_init):
    o = o * lax.rsqrt(jnp.mean(o * o, axis=-1, keepdims=True) + RMS_EPS)
    return o * gain_ref[...] * (1.0 - lam_init)


def _split_halves(q):
    lane = lax.broadcasted_iota(jnp.int32, (1, HEAD_DIM), 1)
    zero = jnp.zeros_like(q)
    return jnp.where(lane < DB_HALF, q, zero), jnp.where(lane >= DB_HALF, q, zero)


def _sink_column(sink_ref, n, t):
    row = lax.broadcasted_iota(jnp.int32, (G_C * t, 1), 0)
    col = jnp.full((G_C * t, 1), sink_ref[n * G_C + G_C - 1], F32)
    for g in range(G_C - 2, -1, -1):
        col = jnp.where(row < (g + 1) * t, sink_ref[n * G_C + g], col)
    return col


def _softmax_parts(parts, extra=None):
    m = functools.reduce(jnp.maximum, [jnp.max(s, axis=-1, keepdims=True) for s in parts])
    if extra is not None:
        m = jnp.maximum(m, extra)
    ps = [jnp.exp(s - m) for s in parts]
    l = functools.reduce(jnp.add, [jnp.sum(p, axis=-1, keepdims=True) for p in ps])
    if extra is not None:
        l = l + jnp.exp(extra - m)
    return ps, l


def _context_attn_kernel(sink_ref, z_ref, lam_ref, gain_ref, o_ref, *, layer):
    scale = HEAD_DIM ** -0.5

    def col(off, h, width=HEAD_DIM):
        return z_ref[:, off + h * width: off + (h + 1) * width]

    for h in range(H_A):
        q, k, v = col(OFF_QA, h).astype(BF16), col(OFF_KA, h).astype(BF16), col(OFF_VA, h).astype(BF16)
        (p,), l = _softmax_parts([_dot_nt(q, k) * scale])
        o = _dot((p / l).astype(BF16), v)
        o_ref[:, h * HEAD_DIM:(h + 1) * HEAD_DIM] = o.astype(o_ref.dtype)

    lam, lam_init = _diff_lambda(lam_ref, layer)
    for h in range(H_B):
        q, k, v = col(OFF_QB, h).astype(BF16), col(OFF_KB, h).astype(BF16), col(OFF_VB, h).astype(BF16)
        q1, q2 = _split_halves(q)
        (p0,), l0 = _softmax_parts([_dot_nt(q1, k) * (DB_HALF ** -0.5)])
        (p1,), l1 = _softmax_parts([_dot_nt(q2, k) * (DB_HALF ** -0.5)])
        w = p0 / l0 - lam * (p1 / l1)
        o = _diff_finish(_dot(w.astype(BF16), v), gain_ref, lam_init)
        o_ref[:, MIX_A + h * HEAD_DIM: MIX_A + (h + 1) * HEAD_DIM] = o.astype(o_ref.dtype)

    t = z_ref.shape[0]
    for n in range(KV_C):
        q = jnp.concatenate([col(OFF_QC, n * G_C + g) for g in range(G_C)], axis=0).astype(BF16)
        k, v = col(OFF_KC, n).astype(BF16), col(OFF_VC, n).astype(BF16)
        sink = _sink_column(sink_ref, n, t)
        (p,), l = _softmax_parts([_dot_nt(q, k) * scale], extra=sink)
        o = _dot((p / l).astype(BF16), v)
        for g in range(G_C):
            c0 = MIX_A + MIX_B + (n * G_C + g) * HEAD_DIM
            o_ref[:, c0:c0 + HEAD_DIM] = o[g * t:(g + 1) * t].astype(o_ref.dtype)


def context_attention(z, seq, lam, gain, sink, layer):
    m = z.shape[0]
    return pl.pallas_call(
        functools.partial(_context_attn_kernel, layer=layer),
        grid=(m // seq,),
        in_specs=[
            pl.BlockSpec(memory_space=pltpu.SMEM),
            pl.BlockSpec((seq, QKV_W), lambda b: (b, 0)),
            pl.BlockSpec((4, DB_HALF), lambda b: (0, 0)),
            pl.BlockSpec((1, HEAD_DIM), lambda b: (0, 0)),
        ],
        out_specs=pl.BlockSpec((seq, D_MODEL), lambda b: (b, 0)),
        out_shape=jax.ShapeDtypeStruct((m, D_MODEL), BF16),
        compiler_params=_params("arbitrary"),
        name="context_attention",
    )(sink, z, lam, gain.reshape(1, HEAD_DIM))


def _na_bias_table(rpb_l, rows):
    nq, nk = NA_QROWS * GRID_W, NA_KROWS * GRID_W
    qi, ki = np.arange(nq)[:, None], np.arange(nk)[None, :]
    tabs = []
    for r0 in (0, NA_QROWS, rows - NA_QROWS):
        ks = int(np.clip(r0 - NA_ROWS // 2, 0, rows - NA_KROWS))
        r, qc = r0 + qi // GRID_W, qi % GRID_W
        kr, kc = ks + ki // GRID_W, ki % GRID_W
        rs = np.clip(r - NA_ROWS // 2, 0, rows - NA_ROWS)
        ws = np.clip(qc - NA_COLS // 2, 0, GRID_W - NA_COLS)
        valid = (kr >= rs) & (kr < rs + NA_ROWS) & (kc >= ws) & (kc < ws + NA_COLS)
        dr = np.clip(kr - r + NA_ROWS - 1, 0, 2 * NA_ROWS - 2)
        dc = np.clip(kc - qc + NA_COLS - 1, 0, 2 * NA_COLS - 2)
        tabs.append(jnp.where(jnp.asarray(valid)[None], rpb_l[:, dr, dc].astype(F32), NEG_INF))
    return jnp.stack(tabs)


def _na_latent_kernel(q_ref, k_ref, v_ref, ck_ref, cv_ref, bias_ref, o_ref, *, rows):
    rb = pl.program_id(2)
    scale = HEAD_DIM ** -0.5
    ks = jnp.clip(rb * NA_QROWS - NA_ROWS // 2, 0, rows - NA_KROWS) * GRID_W
    ks = pl.multiple_of(ks, GRID_W)
    q = q_ref[...]
    kl = k_ref[pl.ds(ks, NA_KROWS * GRID_W), :]
    vl = v_ref[pl.ds(ks, NA_KROWS * GRID_W), :]
    s_loc = _dot_nt(q, kl) * scale + bias_ref[...]
    s_ctx = _dot_nt(q, ck_ref[...].astype(BF16)) * scale
    (p_loc, p_ctx), l = _softmax_parts([s_loc, s_ctx])
    o = _dot(p_loc.astype(BF16), vl) + _dot(p_ctx.astype(BF16), cv_ref[...].astype(BF16))
    o_ref[...] = (o / l).astype(o_ref.dtype)


def na_latent(z, cache_k, cache_v, layer, bias, batch, seq):
    rows = seq // GRID_W
    assert rows % NA_QROWS == 0 and rows >= NA_KROWS
    nrb = rows // NA_QROWS
    tq = NA_QROWS * GRID_W
    n_ctx = cache_k.shape[2]
    ck = cache_k.reshape(batch, DEPTH, n_ctx, MIX_A)
    cv = cache_v.reshape(batch, DEPTH, n_ctx, MIX_A)
    case = lambda rb: jnp.where(rb == 0, 0, jnp.where(rb == nrb - 1, 2, 1))
    ctx_spec = pl.BlockSpec((None, None, n_ctx, HEAD_DIM), lambda b, h, rb: (b, layer, 0, h))
    return pl.pallas_call(
        functools.partial(_na_latent_kernel, rows=rows),
        grid=(batch, H_A, nrb),
        in_specs=[
            pl.BlockSpec((tq, HEAD_DIM), lambda b, h, rb: (b * nrb + rb, OFF_QA // HEAD_DIM + h)),
            pl.BlockSpec((seq, HEAD_DIM), lambda b, h, rb: (b, OFF_KA // HEAD_DIM + h)),
            pl.BlockSpec((seq, HEAD_DIM), lambda b, h, rb: (b, OFF_VA // HEAD_DIM + h)),
            ctx_spec, ctx_spec,
            pl.BlockSpec((None, None, tq, NA_KROWS * GRID_W), lambda b, h, rb: (case(rb), h, 0, 0)),
        ],
        out_specs=pl.BlockSpec((tq, HEAD_DIM), lambda b, h, rb: (b * nrb + rb, h)),
        out_shape=jax.ShapeDtypeStruct((batch * seq, MIX_A), BF16),
        compiler_params=_params("arbitrary", "arbitrary", "arbitrary"),
        name="na_latent",
    )(z, z, z, ck, cv, bias)


def _diff_latent_kernel(lam_ref, gain_ref, q_ref, k_ref, v_ref, ck_ref, cv_ref, o_ref, *, layer):
    lam, lam_init = _diff_lambda(lam_ref, layer)
    q1, q2 = _split_halves(q_ref[...])
    k = k_ref[...]
    ck = ck_ref[...].astype(BF16)
    (p0l, p0c), l0 = _softmax_parts([_dot_nt(q1, k), _dot_nt(q1, ck)])
    (p1l, p1c), l1 = _softmax_parts([_dot_nt(q2, k), _dot_nt(q2, ck)])
    r0 = 1.0 / l0
    r1 = lam / l1
    wl = (p0l * r0 - p1l * r1).astype(BF16)
    wc = (p0c * r0 - p1c * r1).astype(BF16)
    o = _dot(wl, v_ref[...]) + _dot(wc, cv_ref[...].astype(BF16))
    o_ref[...] = _diff_finish(o, gain_ref, lam_init).astype(o_ref.dtype)


def diff_latent(z, cache_k, cache_v, layer, lam, gain, batch, seq, tq=256):
    n_ctx = cache_k.shape[2]
    ck = cache_k.reshape(batch, DEPTH, n_ctx, MIX_B)
    cv = cache_v.reshape(batch, DEPTH, n_ctx, MIX_B)
    nqt = seq // tq
    ctx_spec = pl.BlockSpec((None, None, n_ctx, HEAD_DIM), lambda b, h, t: (b, layer, 0, h))
    return pl.pallas_call(
        functools.partial(_diff_latent_kernel, layer=layer),
        grid=(batch, H_B, nqt),
        in_specs=[
            pl.BlockSpec((4, DB_HALF), lambda b, h, t: (0, 0)),
            pl.BlockSpec((1, HEAD_DIM), lambda b, h, t: (0, 0)),
            pl.BlockSpec((tq, HEAD_DIM), lambda b, h, t: (b * nqt + t, OFF_QB // HEAD_DIM + h)),
            pl.BlockSpec((seq, HEAD_DIM), lambda b, h, t: (b, OFF_KB // HEAD_DIM + h)),
            pl.BlockSpec((seq, HEAD_DIM), lambda b, h, t: (b, OFF_VB // HEAD_DIM + h)),
            ctx_spec, ctx_spec,
        ],
        out_specs=pl.BlockSpec((tq, HEAD_DIM), lambda b, h, t: (b * nqt + t, h)),
        out_shape=jax.ShapeDtypeStruct((batch * seq, MIX_B), BF16),
        compiler_params=_params("arbitrary", "arbitrary", "arbitrary"),
        name="diff_latent",
    )(lam, gain.reshape(1, HEAD_DIM), z, z, z, ck, cv)


def _window_latent_kernel(sink_ref, q_ref, k_ref, v_ref, ck_ref, cv_ref, o_ref, *, seq):
    n = pl.program_id(1)
    t0 = pl.program_id(2) * SW_TQ
    scale = HEAD_DIM ** -0.5
    ks = pl.multiple_of(jnp.clip(t0 - WINDOW_C, 0, seq - SW_SPAN), WINDOW_C)
    kl = k_ref[pl.ds(ks, SW_SPAN), :]
    vl = v_ref[pl.ds(ks, SW_SPAN), :]
    q = jnp.concatenate([q_ref[:, g * HEAD_DIM:(g + 1) * HEAD_DIM] for g in range(G_C)], axis=0)
    tq_idx = t0 + (lax.broadcasted_iota(jnp.int32, (G_C * SW_TQ, SW_SPAN), 0) & (SW_TQ - 1))
    tk_idx = ks + lax.broadcasted_iota(jnp.int32, (G_C * SW_TQ, SW_SPAN), 1)
    valid = jnp.abs(tq_idx - tk_idx) <= WINDOW_C
    s_loc = jnp.where(valid, _dot_nt(q, kl) * scale, NEG_INF)
    s_ctx = _dot_nt(q, ck_ref[...].astype(BF16)) * scale
    sink = _sink_column(sink_ref, n, SW_TQ)
    (p_loc, p_ctx), l = _softmax_parts([s_loc, s_ctx], extra=sink)
    o = (_dot(p_loc.astype(BF16), vl) + _dot(p_ctx.astype(BF16), cv_ref[...].astype(BF16))) / l
    for g in range(G_C):
        o_ref[:, g * HEAD_DIM:(g + 1) * HEAD_DIM] = o[g * SW_TQ:(g + 1) * SW_TQ].astype(o_ref.dtype)


def window_latent(z, cache_k, cache_v, layer, sink, batch, seq):
    assert seq % SW_TQ == 0 and seq >= SW_SPAN
    n_ctx = cache_k.shape[2]
    ck = cache_k.reshape(batch, DEPTH, n_ctx, KVW_C)
    cv = cache_v.reshape(batch, DEPTH, n_ctx, KVW_C)
    nqt = seq // SW_TQ
    gw = G_C * HEAD_DIM
    ctx_spec = pl.BlockSpec((None, None, n_ctx, HEAD_DIM), lambda b, n, t: (b, layer, 0, n))
    return pl.pallas_call(
        functools.partial(_window_latent_kernel, seq=seq),
        grid=(batch, KV_C, nqt),
        in_specs=[
            pl.BlockSpec(memory_space=pltpu.SMEM),
            pl.BlockSpec((SW_TQ, gw), lambda b, n, t: (b * nqt + t, OFF_QC // gw + n)),
            pl.BlockSpec((seq, HEAD_DIM), lambda b, n, t: (b, OFF_KC // HEAD_DIM + n)),
            pl.BlockSpec((seq, HEAD_DIM), lambda b, n, t: (b, OFF_VC // HEAD_DIM + n)),
            ctx_spec, ctx_spec,
        ],
        out_specs=pl.BlockSpec((SW_TQ, gw), lambda b, n, t: (b * nqt + t, n)),
        out_shape=jax.ShapeDtypeStruct((batch * seq, MIX_C), BF16),
        compiler_params=_params("arbitrary", "arbitrary", "arbitrary"),
        name="window_latent",
    )(sink, z, z, z, ck, cv)


def _prompt_row(tm, i):
    return 0


def kernel(x_prompt, x_sample, cache_a_k, cache_a_v, cache_b_k, cache_b_v, cache_c_k, cache_c_v, c, c_ctx,
           w_ada, b_ada, w_in, rpb_a, lam_b, subln_b, sink_c, w_branch, w_out, ln_g, ln_b, w_gate_up, w_down):
    bp, tp, _ = x_prompt.shape
    bs, ts, _ = x_sample.shape

    def sample_row(tm, i):
        return 1 + (i * tm) // ts

    streams = {
        "p": dict(x=x_prompt.reshape(bp * tp, D_MODEL), rowfn=_prompt_row),
        "s": dict(x=x_sample.reshape(bs * ts, D_MODEL), rowfn=sample_row),
    }
    cc = jnp.concatenate([c_ctx[None, :], c], axis=0)
    cc = jnp.pad(cc, ((0, (-cc.shape[0]) % 8), (0, 0)))
    mod = ada_modulation_all(cc, w_ada, b_ada)

    for st in streams.values():
        st["h"] = modulate_cast(st["x"], mod, 0, 0, st["rowfn"])

    new_kv = [[] for _ in range(6)]
    for l in range(DEPTH):
        w_qkv = w_in[l, :, :QKV_W].astype(BF16)
        w_g = w_in[l, :, QKV_W:].astype(BF16)
        w_br = w_branch[l].astype(BF16)
        w_o = w_out[l].astype(BF16)
        w_gu = w_gate_up[l].astype(BF16)
        w_dn = w_down[l].astype(BF16)
        na_bias = _na_bias_table(rpb_a[l], ts // GRID_W)

        sp = streams["p"]
        zp = matmul(sp["h"], w_qkv, F32, 1024, 1024, name="qkv_context")
        for lst, off, wdt, shp in (
                (new_kv[0], OFF_KA, MIX_A, (H_A, HEAD_DIM)), (new_kv[1], OFF_VA, MIX_A, (H_A, HEAD_DIM)),
                (new_kv[2], OFF_KB, MIX_B, (H_B, 2, DB_HALF)), (new_kv[3], OFF_VB, MIX_B, (H_B, HEAD_DIM)),
                (new_kv[4], OFF_KC, KVW_C, (KV_C, HEAD_DIM)), (new_kv[5], OFF_VC, KVW_C, (KV_C, HEAD_DIM))):
            lst.append(zp[:, off:off + wdt].reshape((bp, tp) + shp))
        op = context_attention(zp, tp, lam_b[l], subln_b[l], sink_c[l], l)
        sp["o"] = ((op, 0), (op, MIX_A // MIX_B), (op, (MIX_A + MIX_B) // MIX_C))

        ss = streams["s"]
        zs = qkv_latent(ss["h"], w_qkv, ts)
        oa = na_latent(zs, cache_a_k, cache_a_v, l, na_bias, bs, ts)
        ob = diff_latent(zs, cache_b_k, cache_b_v, l, lam_b[l], subln_b[l], bs, ts)
        oc = window_latent(zs, cache_c_k, cache_c_v, l, sink_c[l], bs, ts)
        ss["o"] = ((oa, 0), (ob, 0), (oc, 0))

        for st in streams.values():
            gates = matmul(st["h"], w_g, BF16, 1024, 1024, sigmoid=True, name="branch_gates")
            u = merge_branches(st["o"], gates, w_br)
            y = matmul(u, w_o, F32, 1024, 1024, name="mixer_out")
            st["x"], h2 = post_norm_residual(st["x"], y, mod, l, 0, ln_g[l, 0], ln_b[l, 0], st["rowfn"],
                                             next_mod=(l, 1))
            a = swiglu_up(h2, w_gu)
            y = matmul(a, w_dn, F32, 512, 512, name="ffn_down")
            nxt = (l + 1, 0) if l + 1 < DEPTH else None
            st["x"], st["h"] = post_norm_residual(st["x"], y, mod, l, 1, ln_g[l, 1], ln_b[l, 1], st["rowfn"],
                                                  next_mod=nxt)

    outs = [jnp.stack(lst, axis=1) for lst in new_kv]
    return (streams["p"]["x"].reshape(bp, tp, D_MODEL), streams["s"]["x"].reshape(bs, ts, D_MODEL), *outs)
```

```python
import functools
import math

import numpy as np
import jax
import jax.numpy as jnp
from jax import lax
from jax.experimental import pallas as pl
from jax.experimental.pallas import tpu as pltpu

D_MODEL = 4096
DEPTH = 2
GRID_W = 64
HEAD_DIM = 128
H_A = 8
MIX_A = H_A * HEAD_DIM
NA_ROWS = 8
NA_COLS = 16
H_B = 8
DB_HALF = HEAD_DIM // 2
MIX_B = H_B * HEAD_DIM
H_C = 16
KV_C = 4
G_C = H_C // KV_C
MIX_C = H_C * HEAD_DIM
KVW_C = KV_C * HEAD_DIM
WINDOW_C = 128
N_BRANCH = 3
QKV_W = 3 * MIX_A + 3 * MIX_B + MIX_C + 2 * KVW_C
D_FF = 11008
ROPE_BASE = 10000.0
LN_EPS = 1e-5
RMS_EPS = 1e-5
ALPHA = (2 * DEPTH) ** 0.25
NEG_INF = -1e30
LOG2E = math.log2(math.e)
BF16 = jnp.bfloat16
F32 = jnp.float32

OFF_QA, OFF_KA, OFF_VA = 0, MIX_A, 2 * MIX_A
OFF_QB, OFF_KB, OFF_VB = 3 * MIX_A, 3 * MIX_A + MIX_B, 3 * MIX_A + 2 * MIX_B
OFF_QC = 3 * MIX_A + 3 * MIX_B
OFF_KC = OFF_QC + MIX_C
OFF_VC = OFF_KC + KVW_C

V7X_VMEM_LIMIT = 56 * 1024 * 1024

NA_QROWS = 8
SW_TQ = 512
SW_SUB = WINDOW_C
SW_SPAN = 3 * SW_SUB
DIFF_TQ = 128
DIFF_NSUB = 4


def _params(*sem):
    return pltpu.CompilerParams(dimension_semantics=sem, vmem_limit_bytes=V7X_VMEM_LIMIT)


def _dot(a, b):
    return jnp.dot(a, b, preferred_element_type=F32)


def _dot_nt(a, b):
    return lax.dot_general(a, b, (((1,), (1,)), ((), ())), preferred_element_type=F32)


def _ada_kernel(c_ref, w_ref, b_ref, o_ref):
    cv = c_ref[...]
    s = (cv * jax.nn.sigmoid(cv)).astype(BF16)
    o_ref[...] = _dot(s, w_ref[...].astype(BF16)) + b_ref[...]


def ada_modulation_all(cc, w_ada, b_ada):
    r = cc.shape[0]
    n = 6 * D_MODEL
    tn = 512
    out = pl.pallas_call(
        _ada_kernel,
        grid=(DEPTH, n // tn),
        in_specs=[
            pl.BlockSpec((r, D_MODEL), lambda l, j: (0, 0)),
            pl.BlockSpec((None, D_MODEL, tn), lambda l, j: (l, 0, j)),
            pl.BlockSpec((None, 1, tn), lambda l, j: (l, 0, j)),
        ],
        out_specs=pl.BlockSpec((None, r, tn), lambda l, j: (l, 0, j)),
        out_shape=jax.ShapeDtypeStruct((DEPTH, r, n), F32),
        compiler_params=_params("arbitrary", "arbitrary"),
        name="ada_modulation",
    )(cc, w_ada, b_ada.reshape(DEPTH, 1, n))
    return out.reshape(DEPTH, r, 6, 1, D_MODEL)


def _mod_spec(layer, chunk, rowfn):
    return pl.BlockSpec((None, None, None, 1, D_MODEL), lambda i: (layer, rowfn(i), chunk, 0, 0))


def _modulate_kernel(x_ref, sc_ref, sh_ref, o_ref):
    o_ref[...] = (x_ref[...] * (1.0 + sc_ref[...]) + sh_ref[...]).astype(BF16)


def modulate_cast(x, mod, layer, idx, rowfn_of_tile, tm=512):
    m = x.shape[0]
    rowfn = functools.partial(rowfn_of_tile, tm)
    return pl.pallas_call(
        _modulate_kernel,
        grid=(m // tm,),
        in_specs=[
            pl.BlockSpec((tm, D_MODEL), lambda i: (i, 0)),
            _mod_spec(layer, 3 * idx + 1, rowfn),
            _mod_spec(layer, 3 * idx, rowfn),
        ],
        out_specs=pl.BlockSpec((tm, D_MODEL), lambda i: (i, 0)),
        out_shape=jax.ShapeDtypeStruct((m, D_MODEL), BF16),
        compiler_params=_params("arbitrary"),
        name="modulate",
    )(x, mod, mod)


def _ln_core(x_ref, y_ref, gate_ref, g_ref, b_ref):
    v = ALPHA * x_ref[...] + gate_ref[...] * y_ref[...]
    mu = jnp.mean(v, axis=-1, keepdims=True)
    d = v - mu
    var = jnp.mean(d * d, axis=-1, keepdims=True)
    return d * lax.rsqrt(var + LN_EPS) * g_ref[...] + b_ref[...]


def _ln_mod_kernel(x_ref, y_ref, gate_ref, g_ref, b_ref, sc_ref, sh_ref, xo_ref, ho_ref):
    xn = _ln_core(x_ref, y_ref, gate_ref, g_ref, b_ref)
    xo_ref[...] = xn
    ho_ref[...] = (xn * (1.0 + sc_ref[...]) + sh_ref[...]).astype(BF16)


def _ln_kernel(x_ref, y_ref, gate_ref, g_ref, b_ref, xo_ref):
    xo_ref[...] = _ln_core(x_ref, y_ref, gate_ref, g_ref, b_ref)


def post_norm_residual(x, y, mod, layer, idx, ln_g, ln_b, rowfn_of_tile, next_mod=None, tm=256):
    m = x.shape[0]
    rowfn = functools.partial(rowfn_of_tile, tm)
    row_spec = pl.BlockSpec((tm, D_MODEL), lambda i: (i, 0))
    vec_spec = pl.BlockSpec((1, D_MODEL), lambda i: (0, 0))
    in_specs = [row_spec, row_spec, _mod_spec(layer, 3 * idx + 2, rowfn), vec_spec, vec_spec]
    args = [x, y, mod, ln_g.reshape(1, D_MODEL), ln_b.reshape(1, D_MODEL)]
    if next_mod is None:
        return pl.pallas_call(
            _ln_kernel, grid=(m // tm,), in_specs=in_specs, out_specs=row_spec,
            out_shape=jax.ShapeDtypeStruct((m, D_MODEL), F32),
            compiler_params=_params("arbitrary"), name="post_norm",
        )(*args), None
    nl, nidx = next_mod
    in_specs += [_mod_spec(nl, 3 * nidx + 1, rowfn), _mod_spec(nl, 3 * nidx, rowfn)]
    args += [mod, mod]
    return pl.pallas_call(
        _ln_mod_kernel, grid=(m // tm,), in_specs=in_specs, out_specs=[row_spec, row_spec],
        out_shape=[jax.ShapeDtypeStruct((m, D_MODEL), F32), jax.ShapeDtypeStruct((m, D_MODEL), BF16)],
        compiler_params=_params("arbitrary"), name="post_norm_modulate",
    )(*args)


def _mm_kernel(x_ref, w_ref, o_ref, *, sigmoid):
    acc = _dot(x_ref[...], w_ref[...])
    if sigmoid:
        acc = jax.nn.sigmoid(acc)
    o_ref[...] = acc.astype(o_ref.dtype)


def matmul(x, w, out_dtype, tm, tn, sigmoid=False, name="matmul"):
    m, k = x.shape
    n = w.shape[1]
    return pl.pallas_call(
        functools.partial(_mm_kernel, sigmoid=sigmoid),
        grid=(m // tm, n // tn),
        in_specs=[pl.BlockSpec((tm, k), lambda i, j: (i, 0)), pl.BlockSpec((k, tn), lambda i, j: (0, j))],
        out_specs=pl.BlockSpec((tm, tn), lambda i, j: (i, j)),
        out_shape=jax.ShapeDtypeStruct((m, n), out_dtype),
        compiler_params=_params("arbitrary", "arbitrary"),
        name=name,
    )(x, w)


def _rope_tables(t, width):
    half, quarter = width // 2, width // 4
    lane = np.arange(HEAD_DIM)
    e = lane % width
    pos = np.arange(t)
    p = np.where((e < half)[None, :], (pos // GRID_W)[:, None], (pos % GRID_W)[:, None]).astype(np.float64)
    inv = np.exp(-math.log(ROPE_BASE) * (e % quarter).astype(np.float64) / quarter)
    ang = (p.astype(np.float32) * inv.astype(np.float32)[None, :]).astype(np.float64)
    first = ((e % half) < quarter)[None, :]
    cos, sin = np.cos(ang), np.sin(ang)
    return (jnp.asarray(cos, F32), jnp.asarray(np.where(first, -sin, 0.0), F32),
            jnp.asarray(np.where(first, 0.0, sin), F32))


def _rope(x, cos, s_lo, s_hi, quarter):
    return x * cos + pltpu.roll(x, HEAD_DIM - quarter, 1) * s_lo + pltpu.roll(x, quarter, 1) * s_hi


def _qkv_latent_kernel(x_ref, w_ref, cb_ref, lb_ref, hb_ref, cc_ref, lc_ref, hc_ref, o_ref, *, tn):
    j = pl.program_id(1)
    acc = _dot(x_ref[...], w_ref[...])
    nchunk = tn // HEAD_DIM
    tiles_qb = (OFF_QB // tn,)
    tiles_kb = (OFF_KB // tn,)
    tiles_qc = tuple(range(OFF_QC // tn, OFF_KC // tn))
    tile_kcvc = OFF_KC // tn

    def is_in(tiles):
        c = j == tiles[0]
        for tt in tiles[1:]:
            c = jnp.logical_or(c, j == tt)
        return c

    def store(fn_of_chunk):
        for ch in range(nchunk):
            sl = slice(ch * HEAD_DIM, (ch + 1) * HEAD_DIM)
            o_ref[:, sl] = fn_of_chunk(ch, acc[:, sl]).astype(o_ref.dtype)

    rope_b = lambda x: _rope(x, cb_ref[...], lb_ref[...], hb_ref[...], DB_HALF // 4)
    rope_c = lambda x: _rope(x, cc_ref[...], lc_ref[...], hc_ref[...], HEAD_DIM // 4)
    tiles_qa = tuple(range(OFF_QA // tn, OFF_KA // tn))
    special = is_in(tiles_qa + tiles_qb + tiles_kb + tiles_qc + (tile_kcvc,))

    @pl.when(jnp.logical_not(special))
    def _():
        o_ref[...] = acc.astype(o_ref.dtype)

    @pl.when(is_in(tiles_qa))
    def _():
        o_ref[...] = (acc * (HEAD_DIM ** -0.5 * LOG2E)).astype(o_ref.dtype)

    @pl.when(is_in(tiles_qb))
    def _():
        store(lambda ch, x: rope_b(x) * (DB_HALF ** -0.5 * LOG2E))

    @pl.when(is_in(tiles_kb))
    def _():
        store(lambda ch, x: rope_b(x))

    @pl.when(is_in(tiles_qc))
    def _():
        store(lambda ch, x: rope_c(x) * (HEAD_DIM ** -0.5 * LOG2E))

    @pl.when(j == tile_kcvc)
    def _():
        store(lambda ch, x: rope_c(x) if ch < KVW_C // HEAD_DIM else x)


def qkv_latent(h, w_qkv, seq, tm=1024, tn=1024):
    m, k = h.shape
    assert tn == MIX_A and OFF_KC % tn == 0 and (OFF_VC - OFF_KC) == KVW_C and seq % tm == 0
    tabs = _rope_tables(seq, DB_HALF) + _rope_tables(seq, HEAD_DIM)
    per_seq = seq // tm
    tab_spec = pl.BlockSpec((tm, HEAD_DIM), lambda i, j: (i % per_seq, 0))
    return pl.pallas_call(
        functools.partial(_qkv_latent_kernel, tn=tn),
        grid=(m // tm, QKV_W // tn),
        in_specs=[pl.BlockSpec((tm, k), lambda i, j: (i, 0)), pl.BlockSpec((k, tn), lambda i, j: (0, j))]
        + [tab_spec] * 6,
        out_specs=pl.BlockSpec((tm, tn), lambda i, j: (i, j)),
        out_shape=jax.ShapeDtypeStruct((m, QKV_W), BF16),
        compiler_params=_params("arbitrary", "arbitrary"),
        name="qkv_latent",
    )(h, w_qkv, *tabs)


def _merge_kernel(oa_ref, ob_ref, oc_ref, wa_ref, wb_ref, wc_ref, ga_ref, gb_ref, gc_ref, o_ref):
    y = ga_ref[...].astype(F32) * _dot(oa_ref[...], wa_ref[...])
    y += gb_ref[...].astype(F32) * _dot(ob_ref[...], wb_ref[...])
    y += gc_ref[...].astype(F32) * _dot(oc_ref[...], wc_ref[...])
    o_ref[...] = y.astype(o_ref.dtype)


def merge_branches(o_parts, gates, w_branch, tm=1024, tn=512):
    (oa, ja), (ob, jb), (oc, jc) = o_parts
    m = gates.shape[0]
    nj = D_MODEL // tn
    return pl.pallas_call(
        _merge_kernel,
        grid=(m // tm, nj),
        in_specs=[
            pl.BlockSpec((tm, MIX_A), lambda i, j: (i, ja)),
            pl.BlockSpec((tm, MIX_B), lambda i, j: (i, jb)),
            pl.BlockSpec((tm, MIX_C), lambda i, j: (i, jc)),
            pl.BlockSpec((MIX_A, tn), lambda i, j: (0, j)),
            pl.BlockSpec((MIX_B, tn), lambda i, j: (MIX_A // MIX_B, j)),
            pl.BlockSpec((MIX_C, tn), lambda i, j: ((MIX_A + MIX_B) // MIX_C, j)),
            pl.BlockSpec((tm, tn), lambda i, j: (i, j)),
            pl.BlockSpec((tm, tn), lambda i, j: (i, nj + j)),
            pl.BlockSpec((tm, tn), lambda i, j: (i, 2 * nj + j)),
        ],
        out_specs=pl.BlockSpec((tm, tn), lambda i, j: (i, j)),
        out_shape=jax.ShapeDtypeStruct((m, D_MODEL), BF16),
        compiler_params=_params("arbitrary", "arbitrary"),
        name="merge_branches",
    )(oa, ob, oc, w_branch, w_branch, w_branch, gates, gates, gates)


def _swiglu_kernel(x_ref, wg_ref, wu_ref, o_ref):
    x = x_ref[...]
    g = _dot(x, wg_ref[...])
    u = _dot(x, wu_ref[...])
    o_ref[...] = (g * jax.nn.sigmoid(g) * u).astype(o_ref.dtype)


def swiglu_up(h, w_gate_up, tm=2048, tn=256):
    m, k = h.shape
    nj = D_FF // tn
    return pl.pallas_call(
        _swiglu_kernel,
        grid=(m // tm, nj),
        in_specs=[
            pl.BlockSpec((tm, k), lambda i, j: (i, 0)),
            pl.BlockSpec((k, tn), lambda i, j: (0, j)),
            pl.BlockSpec((k, tn), lambda i, j: (0, nj + j)),
        ],
        out_specs=pl.BlockSpec((tm, tn), lambda i, j: (i, j)),
        out_shape=jax.ShapeDtypeStruct((m, D_FF), BF16),
        compiler_params=_params("arbitrary", "arbitrary"),
        name="swiglu_up",
    )(h, w_gate_up, w_gate_up)


def _diff_lambda(lam_ref, layer):
    lam_init = 0.8 - 0.6 * math.exp(-0.3 * layer)
    lf = lam_ref[...]
    a = jnp.sum(lf[0:1, :] * lf[1:2, :], axis=-1, keepdims=True)
    b = jnp.sum(lf[2:3, :] * lf[3:4, :], axis=-1, keepdims=True)
    return jnp.exp(a) - jnp.exp(b) + lam_init, lam_init


def _diff_finish(o, gain_ref, lam_init):
    o = o * lax.rsqrt(jnp.mean(o * o, axis=-1, keepdims=True) + RMS_EPS)
    return o * gain_ref[...] * (1.0 - lam_init)


def _split_halves(q):
    lane = lax.broadcasted_iota(jnp.int32, (1, HEAD_DIM), 1)
    zero = jnp.zeros_like(q)
    return jnp.where(lane < DB_HALF, q, zero), jnp.where(lane >= DB_HALF, q, zero)


def _sink_column(sink_ref, n, t):
    row = lax.broadcasted_iota(jnp.int32, (G_C * t, 1), 0)
    col = jnp.full((G_C * t, 1), sink_ref[n * G_C + G_C - 1], F32)
    for g in range(G_C - 2, -1, -1):
        col = jnp.where(row < (g + 1) * t, sink_ref[n * G_C + g], col)
    return col


def _softmax_parts(parts, extra=None, exp=jnp.exp):
    m = functools.reduce(jnp.maximum, [jnp.max(s, axis=-1, keepdims=True) for s in parts])
    if extra is not None:
        m = jnp.maximum(m, extra)
    ps = [exp(s - m) for s in parts]
    l = functools.reduce(jnp.add, [jnp.sum(p, axis=-1, keepdims=True) for p in ps])
    if extra is not None:
        l = l + exp(extra - m)
    return ps, l


def _context_attn_kernel(sink_ref, z_ref, lam_ref, gain_ref, o_ref, *, layer):
    scale = HEAD_DIM ** -0.5

    def col(off, h, width=HEAD_DIM):
        return z_ref[:, off + h * width: off + (h + 1) * width]

    for h in range(H_A):
        q, k, v = col(OFF_QA, h).astype(BF16), col(OFF_KA, h).astype(BF16), col(OFF_VA, h).astype(BF16)
        (p,), l = _softmax_parts([_dot_nt(q, k) * scale])
        o = _dot((p / l).astype(BF16), v)
        o_ref[:, h * HEAD_DIM:(h + 1) * HEAD_DIM] = o.astype(o_ref.dtype)

    lam, lam_init = _diff_lambda(lam_ref, layer)
    for h in range(H_B):
        q, k, v = col(OFF_QB, h).astype(BF16), col(OFF_KB, h).astype(BF16), col(OFF_VB, h).astype(BF16)
        q1, q2 = _split_halves(q)
        (p0,), l0 = _softmax_parts([_dot_nt(q1, k) * (DB_HALF ** -0.5)])
        (p1,), l1 = _softmax_parts([_dot_nt(q2, k) * (DB_HALF ** -0.5)])
        w = p0 / l0 - lam * (p1 / l1)
        o = _diff_finish(_dot(w.astype(BF16), v), gain_ref, lam_init)
        o_ref[:, MIX_A + h * HEAD_DIM: MIX_A + (h + 1) * HEAD_DIM] = o.astype(o_ref.dtype)

    t = z_ref.shape[0]
    for n in range(KV_C):
        q = jnp.concatenate([col(OFF_QC, n * G_C + g) for g in range(G_C)], axis=0).astype(BF16)
        k, v = col(OFF_KC, n).astype(BF16), col(OFF_VC, n).astype(BF16)
        sink = _sink_column(sink_ref, n, t)
        (p,), l = _softmax_parts([_dot_nt(q, k) * scale], extra=sink)
        o = _dot((p / l).astype(BF16), v)
        for g in range(G_C):
            c0 = MIX_A + MIX_B + (n * G_C + g) * HEAD_DIM
            o_ref[:, c0:c0 + HEAD_DIM] = o[g * t:(g + 1) * t].astype(o_ref.dtype)


def context_attention(z, seq, lam, gain, sink, layer):
    m = z.shape[0]
    return pl.pallas_call(
        functools.partial(_context_attn_kernel, layer=layer),
        grid=(m // seq,),
        in_specs=[
            pl.BlockSpec(memory_space=pltpu.SMEM),
            pl.BlockSpec((seq, QKV_W), lambda b: (b, 0)),
            pl.BlockSpec((4, DB_HALF), lambda b: (0, 0)),
            pl.BlockSpec((1, HEAD_DIM), lambda b: (0, 0)),
        ],
        out_specs=pl.BlockSpec((seq, D_MODEL), lambda b: (b, 0)),
        out_shape=jax.ShapeDtypeStruct((m, D_MODEL), BF16),
        compiler_params=_params("arbitrary"),
        name="context_attention",
    )(sink, z, lam, gain.reshape(1, HEAD_DIM))


def _na_bias_table(rpb_l):
    pad = GRID_W
    rp = jnp.pad(rpb_l.astype(F32) * LOG2E, ((0, 0), (0, 0), (pad, pad)))
    first = pad + NA_COLS - 1
    tz = jnp.stack([rp[:, :, first - qc:first - qc + GRID_W] for qc in range(GRID_W)], axis=2)
    qc, kc = np.arange(GRID_W)[:, None], np.arange(GRID_W)[None, :]
    ws = np.clip(qc - NA_COLS // 2, 0, GRID_W - NA_COLS)
    valid = (kc >= ws) & (kc < ws + NA_COLS)
    tz = jnp.where(jnp.asarray(valid)[None, None], tz, NEG_INF)
    return jnp.stack([jnp.concatenate([tz[:, e0 + j] for j in range(NA_ROWS)], axis=-1)
                      for e0 in range(NA_ROWS)], axis=1)


def _na_latent_kernel(q_ref, k_ref, v_ref, ck_ref, cv_ref, bias_ref, o_ref, *, rows):
    rb = pl.program_id(2)
    span = NA_ROWS * GRID_W
    q = q_ref[...]
    s_ctx = _dot_nt(q, ck_ref[...].astype(BF16))
    starts, s_loc = [], []
    for r in range(NA_QROWS):
        r_abs = rb * NA_QROWS + r
        rs = jnp.clip(r_abs - NA_ROWS // 2, 0, rows - NA_ROWS)
        e0 = rs - r_abs + NA_ROWS - 1
        ks = pl.multiple_of(rs * GRID_W, GRID_W)
        starts.append(ks)
        qr = q[r * GRID_W:(r + 1) * GRID_W]
        s_loc.append(_dot_nt(qr, k_ref[pl.ds(ks, span), :]) + bias_ref[e0])
    (p_loc, p_ctx), l = _softmax_parts([jnp.concatenate(s_loc, axis=0), s_ctx], exp=jnp.exp2)
    p_loc = p_loc.astype(BF16)
    o_loc = [_dot(p_loc[r * GRID_W:(r + 1) * GRID_W], v_ref[pl.ds(starts[r], span), :]) for r in range(NA_QROWS)]
    o = jnp.concatenate(o_loc, axis=0) + _dot(p_ctx.astype(BF16), cv_ref[...].astype(BF16))
    o_ref[...] = (o / l).astype(o_ref.dtype)


def na_latent(z, cache_k, cache_v, layer, bias, batch, seq):
    rows = seq // GRID_W
    assert rows % NA_QROWS == 0 and rows >= NA_ROWS
    nrb = rows // NA_QROWS
    tq = NA_QROWS * GRID_W
    n_ctx = cache_k.shape[2]
    ck = cache_k.reshape(batch, DEPTH, n_ctx, MIX_A)
    cv = cache_v.reshape(batch, DEPTH, n_ctx, MIX_A)
    ctx_spec = pl.BlockSpec((None, None, n_ctx, HEAD_DIM), lambda b, h, rb: (b, layer, 0, h))
    return pl.pallas_call(
        functools.partial(_na_latent_kernel, rows=rows),
        grid=(batch, H_A, nrb),
        in_specs=[
            pl.BlockSpec((tq, HEAD_DIM), lambda b, h, rb: (b * nrb + rb, OFF_QA // HEAD_DIM + h)),
            pl.BlockSpec((seq, HEAD_DIM), lambda b, h, rb: (b, OFF_KA // HEAD_DIM + h)),
            pl.BlockSpec((seq, HEAD_DIM), lambda b, h, rb: (b, OFF_VA // HEAD_DIM + h)),
            ctx_spec, ctx_spec,
            pl.BlockSpec((None, NA_ROWS, GRID_W, NA_ROWS * GRID_W), lambda b, h, rb: (h, 0, 0, 0)),
        ],
        out_specs=pl.BlockSpec((tq, HEAD_DIM), lambda b, h, rb: (b * nrb + rb, h)),
        out_shape=jax.ShapeDtypeStruct((batch * seq, MIX_A), BF16),
        compiler_params=_params("arbitrary", "arbitrary", "arbitrary"),
        name="na_latent",
    )(z, z, z, ck, cv, bias)


def _diff_latent_kernel(lam_ref, gain_ref, q_ref, k_ref, v_ref, ck_ref, cv_ref, o_ref, *, layer):
    lam, lam_init = _diff_lambda(lam_ref, layer)
    k = k_ref[...]
    ck = ck_ref[...].astype(BF16)
    cv = cv_ref[...].astype(BF16)
    nsub = q_ref.shape[0] // DIFF_TQ
    rows = [slice(u * DIFF_TQ, (u + 1) * DIFF_TQ) for u in range(nsub)]
    scores = [[[_dot_nt(qh, k), _dot_nt(qh, ck)] for qh in _split_halves(q_ref[r, :])] for r in rows]
    stats = [[_softmax_parts(sc, exp=jnp.exp2) for sc in su] for su in scores]
    for r, ((p0, l0), (p1, l1)) in zip(rows, stats):
        c = lam * l0 / l1
        wl = (p0[0] - p1[0] * c).astype(BF16)
        wc = (p0[1] - p1[1] * c).astype(BF16)
        o = (_dot(wl, v_ref[...]) + _dot(wc, cv)) / l0
        o_ref[r, :] = _diff_finish(o, gain_ref, lam_init).astype(o_ref.dtype)


def diff_latent(z, cache_k, cache_v, layer, lam, gain, batch, seq):
    tq = DIFF_TQ * DIFF_NSUB
    n_ctx = cache_k.shape[2]
    ck = cache_k.reshape(batch, DEPTH, n_ctx, MIX_B)
    cv = cache_v.reshape(batch, DEPTH, n_ctx, MIX_B)
    nqt = seq // tq
    ctx_spec = pl.BlockSpec((None, None, n_ctx, HEAD_DIM), lambda b, h, t: (b, layer, 0, h))
    return pl.pallas_call(
        functools.partial(_diff_latent_kernel, layer=layer),
        grid=(batch, H_B, nqt),
        in_specs=[
            pl.BlockSpec((4, DB_HALF), lambda b, h, t: (0, 0)),
            pl.BlockSpec((1, HEAD_DIM), lambda b, h, t: (0, 0)),
            pl.BlockSpec((tq, HEAD_DIM), lambda b, h, t: (b * nqt + t, OFF_QB // HEAD_DIM + h)),
            pl.BlockSpec((seq, HEAD_DIM), lambda b, h, t: (b, OFF_KB // HEAD_DIM + h)),
            pl.BlockSpec((seq, HEAD_DIM), lambda b, h, t: (b, OFF_VB // HEAD_DIM + h)),
            ctx_spec, ctx_spec,
        ],
        out_specs=pl.BlockSpec((tq, HEAD_DIM), lambda b, h, t: (b * nqt + t, h)),
        out_shape=jax.ShapeDtypeStruct((batch * seq, MIX_B), BF16),
        compiler_params=_params("arbitrary", "arbitrary", "arbitrary"),
        name="diff_latent",
    )(lam, gain.reshape(1, HEAD_DIM), z, z, z, ck, cv)


def _window_mask_table():
    i, j = np.arange(SW_SUB)[:, None], np.arange(SW_SPAN)[None, :]
    tabs = [np.where(np.abs(c * WINDOW_C + i - j) <= WINDOW_C, 0.0, NEG_INF) for c in range(3)]
    return jnp.asarray(np.stack(tabs), F32)


def _window_latent_kernel(sink_ref, mask_ref, q_ref, k_ref, v_ref, ck_ref, cv_ref, o_ref, *, seq):
    n = pl.program_id(1)
    t0 = pl.program_id(2) * SW_TQ
    ck = ck_ref[...].astype(BF16)
    cv = cv_ref[...].astype(BF16)
    sink = _sink_column(sink_ref, n, SW_SUB) * LOG2E
    for u in range(SW_TQ // SW_SUB):
        rows = slice(u * SW_SUB, (u + 1) * SW_SUB)
        tu = t0 + u * SW_SUB
        ks = pl.multiple_of(jnp.clip(tu - WINDOW_C, 0, seq - SW_SPAN), WINDOW_C)
        case = (tu - ks) // WINDOW_C
        q = jnp.concatenate([q_ref[rows, g * HEAD_DIM:(g + 1) * HEAD_DIM] for g in range(G_C)], axis=0)
        s_loc = _dot_nt(q, k_ref[pl.ds(ks, SW_SPAN), :]).reshape(G_C, SW_SUB, SW_SPAN) + mask_ref[case][None]
        s_loc = s_loc.reshape(G_C * SW_SUB, SW_SPAN)
        (p_loc, p_ctx), l = _softmax_parts([s_loc, _dot_nt(q, ck)], extra=sink, exp=jnp.exp2)
        o = (_dot(p_loc.astype(BF16), v_ref[pl.ds(ks, SW_SPAN), :]) + _dot(p_ctx.astype(BF16), cv)) / l
        for g in range(G_C):
            o_ref[rows, g * HEAD_DIM:(g + 1) * HEAD_DIM] = o[g * SW_SUB:(g + 1) * SW_SUB].astype(o_ref.dtype)


def window_latent(z, cache_k, cache_v, layer, sink, batch, seq):
    assert seq % SW_TQ == 0 and seq >= SW_SPAN and SW_TQ % SW_SUB == 0
    n_ctx = cache_k.shape[2]
    ck = cache_k.reshape(batch, DEPTH, n_ctx, KVW_C)
    cv = cache_v.reshape(batch, DEPTH, n_ctx, KVW_C)
    nqt = seq // SW_TQ
    gw = G_C * HEAD_DIM
    ctx_spec = pl.BlockSpec((None, None, n_ctx, HEAD_DIM), lambda b, n, t: (b, layer, 0, n))
    return pl.pallas_call(
        functools.partial(_window_latent_kernel, seq=seq),
        grid=(batch, KV_C, nqt),
        in_specs=[
            pl.BlockSpec(memory_space=pltpu.SMEM),
            pl.BlockSpec((3, SW_SUB, SW_SPAN), lambda b, n, t: (0, 0, 0)),
            pl.BlockSpec((SW_TQ, gw), lambda b, n, t: (b * nqt + t, OFF_QC // gw + n)),
            pl.BlockSpec((seq, HEAD_DIM), lambda b, n, t: (b, OFF_KC // HEAD_DIM + n)),
            pl.BlockSpec((seq, HEAD_DIM), lambda b, n, t: (b, OFF_VC // HEAD_DIM + n)),
            ctx_spec, ctx_spec,
        ],
        out_specs=pl.BlockSpec((SW_TQ, gw), lambda b, n, t: (b * nqt + t, n)),
        out_shape=jax.ShapeDtypeStruct((batch * seq, MIX_C), BF16),
        compiler_params=_params("arbitrary", "arbitrary", "arbitrary"),
        name="window_latent",
    )(sink, _window_mask_table(), z, z, z, ck, cv)


def _prompt_row(tm, i):
    return 0


def kernel(x_prompt, x_sample, cache_a_k, cache_a_v, cache_b_k, cache_b_v, cache_c_k, cache_c_v, c, c_ctx,
           w_ada, b_ada, w_in, rpb_a, lam_b, subln_b, sink_c, w_branch, w_out, ln_g, ln_b, w_gate_up, w_down):
    bp, tp, _ = x_prompt.shape
    bs, ts, _ = x_sample.shape

    def sample_row(tm, i):
        return 1 + (i * tm) // ts

    streams = {
        "p": dict(x=x_prompt.reshape(bp * tp, D_MODEL), rowfn=_prompt_row),
        "s": dict(x=x_sample.reshape(bs * ts, D_MODEL), rowfn=sample_row),
    }
    cc = jnp.concatenate([c_ctx[None, :], c], axis=0)
    cc = jnp.pad(cc, ((0, (-cc.shape[0]) % 8), (0, 0)))
    mod = ada_modulation_all(cc, w_ada, b_ada)

    for st in streams.values():
        st["h"] = modulate_cast(st["x"], mod, 0, 0, st["rowfn"])

    new_kv = [[] for _ in range(6)]
    for l in range(DEPTH):
        w_qkv = w_in[l, :, :QKV_W].astype(BF16)
        w_g = w_in[l, :, QKV_W:].astype(BF16)
        w_br = w_branch[l].astype(BF16)
        w_o = w_out[l].astype(BF16)
        w_gu = w_gate_up[l].astype(BF16)
        w_dn = w_down[l].astype(BF16)
        na_bias = _na_bias_table(rpb_a[l])

        sp = streams["p"]
        zp = matmul(sp["h"], w_qkv, F32, 1024, 1024, name="qkv_context")
        for lst, off, wdt, shp in (
                (new_kv[0], OFF_KA, MIX_A, (H_A, HEAD_DIM)), (new_kv[1], OFF_VA, MIX_A, (H_A, HEAD_DIM)),
                (new_kv[2], OFF_KB, MIX_B, (H_B, 2, DB_HALF)), (new_kv[3], OFF_VB, MIX_B, (H_B, HEAD_DIM)),
                (new_kv[4], OFF_KC, KVW_C, (KV_C, HEAD_DIM)), (new_kv[5], OFF_VC, KVW_C, (KV_C, HEAD_DIM))):
            lst.append(zp[:, off:off + wdt].reshape((bp, tp) + shp))
        op = context_attention(zp, tp, lam_b[l], subln_b[l], sink_c[l], l)
        sp["o"] = ((op, 0), (op, MIX_A // MIX_B), (op, (MIX_A + MIX_B) // MIX_C))

        ss = streams["s"]
        zs = qkv_latent(ss["h"], w_qkv, ts)
        oa = na_latent(zs, cache_a_k, cache_a_v, l, na_bias, bs, ts)
        ob = diff_latent(zs, cache_b_k, cache_b_v, l, lam_b[l], subln_b[l], bs, ts)
        oc = window_latent(zs, cache_c_k, cache_c_v, l, sink_c[l], bs, ts)
        ss["o"] = ((oa, 0), (ob, 0), (oc, 0))

        for st in streams.values():
            gates = matmul(st["h"], w_g, BF16, 1024, 1024, sigmoid=True, name="branch_gates")
            u = merge_branches(st["o"], gates, w_br)
            y = matmul(u, w_o, F32, 1024, 1024, name="mixer_out")
            st["x"], h2 = post_norm_residual(st["x"], y, mod, l, 0, ln_g[l, 0], ln_b[l, 0], st["rowfn"],
                                             next_mod=(l, 1))
            a = swiglu_up(h2, w_gu)
            y = matmul(a, w_dn, F32, 512, 512, name="ffn_down")
            nxt = (l + 1, 0) if l + 1 < DEPTH else None
            st["x"], st["h"] = post_norm_residual(st["x"], y, mod, l, 1, ln_g[l, 1], ln_b[l, 1], st["rowfn"],
                                                  next_mod=nxt)

    outs = [jnp.stack(lst, axis=1) for lst in new_kv]
    return (streams["p"]["x"].reshape(bp, tp, D_MODEL), streams["s"]["x"].reshape(bs, ts, D_MODEL), *outs)
```

```python
import functools
import math

import numpy as np
import jax
import jax.numpy as jnp
from jax import lax
from jax.experimental import pallas as pl
from jax.experimental.pallas import tpu as pltpu

D_MODEL = 4096
DEPTH = 2
GRID_W = 64
HEAD_DIM = 128
H_A = 8
MIX_A = H_A * HEAD_DIM
NA_ROWS = 8
NA_COLS = 16
H_B = 8
DB_HALF = HEAD_DIM // 2
MIX_B = H_B * HEAD_DIM
H_C = 16
KV_C = 4
G_C = H_C // KV_C
MIX_C = H_C * HEAD_DIM
KVW_C = KV_C * HEAD_DIM
WINDOW_C = 128
N_BRANCH = 3
QKV_W = 3 * MIX_A + 3 * MIX_B + MIX_C + 2 * KVW_C
D_FF = 11008
ROPE_BASE = 10000.0
LN_EPS = 1e-5
RMS_EPS = 1e-5
ALPHA = (2 * DEPTH) ** 0.25
NEG_INF = -1e30
LOG2E = math.log2(math.e)
BF16 = jnp.bfloat16
F32 = jnp.float32

OFF_QA, OFF_KA, OFF_VA = 0, MIX_A, 2 * MIX_A
OFF_QB, OFF_KB, OFF_VB = 3 * MIX_A, 3 * MIX_A + MIX_B, 3 * MIX_A + 2 * MIX_B
OFF_QC = 3 * MIX_A + 3 * MIX_B
OFF_KC = OFF_QC + MIX_C
OFF_VC = OFF_KC + KVW_C

V7X_VMEM_LIMIT = 56 * 1024 * 1024
MXU_COLS = 256

NA_QROWS = 8
SW_TQ = 512
SW_SUB = WINDOW_C
SW_SPAN = 3 * SW_SUB
DIFF_TQ = 128
DIFF_NSUB = 4


def _params(*sem):
    return pltpu.CompilerParams(dimension_semantics=sem, vmem_limit_bytes=V7X_VMEM_LIMIT)


def _dot(a, b):
    return jnp.dot(a, b, preferred_element_type=F32)


def _dot_nt(a, b):
    return lax.dot_general(a, b, (((1,), (1,)), ((), ())), preferred_element_type=F32)


def _cast_kernel(w_ref, o_ref):
    o_ref[...] = w_ref[...].astype(o_ref.dtype)


def cast_weight(w, layer, col0=0, ncols=None):
    _, r, c = w.shape
    ncols = c - col0 if ncols is None else ncols
    tc = 512 if r <= D_MODEL else HEAD_DIM
    assert col0 % tc == 0 and ncols % tc == 0
    return pl.pallas_call(
        _cast_kernel,
        grid=(ncols // tc,),
        in_specs=[pl.BlockSpec((None, r, tc), lambda j: (layer, 0, col0 // tc + j))],
        out_specs=pl.BlockSpec((r, tc), lambda j: (0, j)),
        out_shape=jax.ShapeDtypeStruct((r, ncols), BF16),
        compiler_params=_params("arbitrary"),
        name="cast_weight",
    )(w)


def _ada_kernel(c_ref, w_ref, b_ref, o_ref):
    cv = c_ref[...]
    s = (cv * jax.nn.sigmoid(cv)).astype(BF16)
    o_ref[...] = _dot(s, w_ref[...].astype(BF16)) + b_ref[...]


def ada_modulation_all(cc, w_ada, b_ada):
    r = cc.shape[0]
    n = 6 * D_MODEL
    tn = 512
    out = pl.pallas_call(
        _ada_kernel,
        grid=(DEPTH, n // tn),
        in_specs=[
            pl.BlockSpec((r, D_MODEL), lambda l, j: (0, 0)),
            pl.BlockSpec((None, D_MODEL, tn), lambda l, j: (l, 0, j)),
            pl.BlockSpec((None, 1, tn), lambda l, j: (l, 0, j)),
        ],
        out_specs=pl.BlockSpec((None, r, tn), lambda l, j: (l, 0, j)),
        out_shape=jax.ShapeDtypeStruct((DEPTH, r, n), F32),
        compiler_params=_params("arbitrary", "arbitrary"),
        name="ada_modulation",
    )(cc, w_ada, b_ada.reshape(DEPTH, 1, n))
    return out.reshape(DEPTH, r, 6, 1, D_MODEL)


def _mod_spec(layer, chunk, rowfn):
    return pl.BlockSpec((None, None, None, 1, D_MODEL), lambda i: (layer, rowfn(i), chunk, 0, 0))


def _modulate_kernel(x_ref, sc_ref, sh_ref, o_ref):
    o_ref[...] = (x_ref[...] * (1.0 + sc_ref[...]) + sh_ref[...]).astype(BF16)


def modulate_cast(x, mod, layer, idx, rowfn_of_tile, tm=512):
    m = x.shape[0]
    rowfn = functools.partial(rowfn_of_tile, tm)
    return pl.pallas_call(
        _modulate_kernel,
        grid=(m // tm,),
        in_specs=[
            pl.BlockSpec((tm, D_MODEL), lambda i: (i, 0)),
            _mod_spec(layer, 3 * idx + 1, rowfn),
            _mod_spec(layer, 3 * idx, rowfn),
        ],
        out_specs=pl.BlockSpec((tm, D_MODEL), lambda i: (i, 0)),
        out_shape=jax.ShapeDtypeStruct((m, D_MODEL), BF16),
        compiler_params=_params("arbitrary"),
        name="modulate",
    )(x, mod, mod)


def _ln_core(v_ref, g_ref, b_ref):
    v = v_ref[...]
    mu = jnp.mean(v, axis=-1, keepdims=True)
    d = v - mu
    var = jnp.mean(d * d, axis=-1, keepdims=True)
    return d * lax.rsqrt(var + LN_EPS) * g_ref[...] + b_ref[...]


def _ln_mod_kernel(v_ref, g_ref, b_ref, sc_ref, sh_ref, xo_ref, ho_ref):
    xn = _ln_core(v_ref, g_ref, b_ref)
    xo_ref[...] = xn
    ho_ref[...] = (xn * (1.0 + sc_ref[...]) + sh_ref[...]).astype(BF16)


def _ln_kernel(v_ref, g_ref, b_ref, xo_ref):
    xo_ref[...] = _ln_core(v_ref, g_ref, b_ref)


def post_norm(v, mod, ln_g, ln_b, rowfn_of_tile, next_mod=None, tm=256):
    m = v.shape[0]
    rowfn = functools.partial(rowfn_of_tile, tm)
    row_spec = pl.BlockSpec((tm, D_MODEL), lambda i: (i, 0))
    vec_spec = pl.BlockSpec((1, D_MODEL), lambda i: (0, 0))
    in_specs = [row_spec, vec_spec, vec_spec]
    args = [v, ln_g.reshape(1, D_MODEL), ln_b.reshape(1, D_MODEL)]
    if next_mod is None:
        return pl.pallas_call(
            _ln_kernel, grid=(m // tm,), in_specs=in_specs, out_specs=row_spec,
            out_shape=jax.ShapeDtypeStruct((m, D_MODEL), F32),
            compiler_params=_params("arbitrary"), name="post_norm",
        )(*args), None
    nl, nidx = next_mod
    in_specs += [_mod_spec(nl, 3 * nidx + 1, rowfn), _mod_spec(nl, 3 * nidx, rowfn)]
    args += [mod, mod]
    return pl.pallas_call(
        _ln_mod_kernel, grid=(m // tm,), in_specs=in_specs, out_specs=[row_spec, row_spec],
        out_shape=[jax.ShapeDtypeStruct((m, D_MODEL), F32), jax.ShapeDtypeStruct((m, D_MODEL), BF16)],
        compiler_params=_params("arbitrary"), name="post_norm_modulate",
    )(*args)


def _mm_kernel(x_ref, w_ref, o_ref, *, sigmoid):
    acc = _dot(x_ref[...], w_ref[...])
    if sigmoid:
        acc = jax.nn.sigmoid(acc)
    o_ref[...] = acc.astype(o_ref.dtype)


def matmul(x, w, out_dtype, tm, tn, sigmoid=False, name="matmul"):
    m, k = x.shape
    n = w.shape[1]
    return pl.pallas_call(
        functools.partial(_mm_kernel, sigmoid=sigmoid),
        grid=(m // tm, n // tn),
        in_specs=[pl.BlockSpec((tm, k), lambda i, j: (i, 0)), pl.BlockSpec((k, tn), lambda i, j: (0, j))],
        out_specs=pl.BlockSpec((tm, tn), lambda i, j: (i, j)),
        out_shape=jax.ShapeDtypeStruct((m, n), out_dtype),
        compiler_params=_params("arbitrary", "arbitrary"),
        name=name,
    )(x, w)


def _mm_residual_kernel(x_ref, w_ref, res_ref, gate_ref, o_ref):
    o_ref[...] = ALPHA * res_ref[...] + gate_ref[...] * _dot(x_ref[...], w_ref[...])


def matmul_residual(x, w, res, mod, layer, idx, rowfn_of_tile, tm, tn, name):
    m, k = x.shape
    n = w.shape[1]
    rowfn = functools.partial(rowfn_of_tile, tm)
    gate_spec = pl.BlockSpec((None, None, None, 1, tn), lambda i, j: (layer, rowfn(i), 3 * idx + 2, 0, j))
    tile_spec = pl.BlockSpec((tm, tn), lambda i, j: (i, j))
    return pl.pallas_call(
        _mm_residual_kernel,
        grid=(m // tm, n // tn),
        in_specs=[pl.BlockSpec((tm, k), lambda i, j: (i, 0)), pl.BlockSpec((k, tn), lambda i, j: (0, j)),
                  tile_spec, gate_spec],
        out_specs=tile_spec,
        out_shape=jax.ShapeDtypeStruct((m, n), F32),
        compiler_params=_params("arbitrary", "arbitrary"),
        name=name,
    )(x, w, res, mod)


def _rope_tables(t, width):
    half, quarter = width // 2, width // 4
    lane = np.arange(HEAD_DIM)
    e = lane % width
    pos = np.arange(t)
    p = np.where((e < half)[None, :], (pos // GRID_W)[:, None], (pos % GRID_W)[:, None]).astype(np.float64)
    inv = np.exp(-math.log(ROPE_BASE) * (e % quarter).astype(np.float64) / quarter)
    ang = (p.astype(np.float32) * inv.astype(np.float32)[None, :]).astype(np.float64)
    first = ((e % half) < quarter)[None, :]
    cos, sin = np.cos(ang), np.sin(ang)
    return (jnp.asarray(cos, F32), jnp.asarray(np.where(first, -sin, 0.0), F32),
            jnp.asarray(np.where(first, 0.0, sin), F32))


def _rope(x, cos, s_lo, s_hi, quarter):
    return x * cos + pltpu.roll(x, HEAD_DIM - quarter, 1) * s_lo + pltpu.roll(x, quarter, 1) * s_hi


def _qkv_latent_kernel(x_ref, w_ref, cb_ref, lb_ref, hb_ref, cc_ref, lc_ref, hc_ref, o_ref, *, tn):
    j = pl.program_id(1)
    acc = _dot(x_ref[...], w_ref[...])
    nchunk = tn // HEAD_DIM
    tiles_qb = (OFF_QB // tn,)
    tiles_kb = (OFF_KB // tn,)
    tiles_qc = tuple(range(OFF_QC // tn, OFF_KC // tn))
    tile_kcvc = OFF_KC // tn

    def is_in(tiles):
        c = j == tiles[0]
        for tt in tiles[1:]:
            c = jnp.logical_or(c, j == tt)
        return c

    def store(fn_of_chunk):
        for ch in range(nchunk):
            sl = slice(ch * HEAD_DIM, (ch + 1) * HEAD_DIM)
            o_ref[:, sl] = fn_of_chunk(ch, acc[:, sl]).astype(o_ref.dtype)

    rope_b = lambda x: _rope(x, cb_ref[...], lb_ref[...], hb_ref[...], DB_HALF // 4)
    rope_c = lambda x: _rope(x, cc_ref[...], lc_ref[...], hc_ref[...], HEAD_DIM // 4)
    tiles_qa = tuple(range(OFF_QA // tn, OFF_KA // tn))
    special = is_in(tiles_qa + tiles_qb + tiles_kb + tiles_qc + (tile_kcvc,))

    @pl.when(jnp.logical_not(special))
    def _():
        o_ref[...] = acc.astype(o_ref.dtype)

    @pl.when(is_in(tiles_qa))
    def _():
        o_ref[...] = (acc * (HEAD_DIM ** -0.5 * LOG2E)).astype(o_ref.dtype)

    @pl.when(is_in(tiles_qb))
    def _():
        store(lambda ch, x: rope_b(x) * (DB_HALF ** -0.5 * LOG2E))

    @pl.when(is_in(tiles_kb))
    def _():
        store(lambda ch, x: rope_b(x))

    @pl.when(is_in(tiles_qc))
    def _():
        store(lambda ch, x: rope_c(x) * (HEAD_DIM ** -0.5 * LOG2E))

    @pl.when(j == tile_kcvc)
    def _():
        store(lambda ch, x: rope_c(x) if ch < KVW_C // HEAD_DIM else x)


def qkv_latent(h, w_qkv, seq, tm=1024, tn=1024):
    m, k = h.shape
    assert tn == MIX_A and OFF_KC % tn == 0 and (OFF_VC - OFF_KC) == KVW_C and seq % tm == 0
    tabs = _rope_tables(seq, DB_HALF) + _rope_tables(seq, HEAD_DIM)
    per_seq = seq // tm
    tab_spec = pl.BlockSpec((tm, HEAD_DIM), lambda i, j: (i % per_seq, 0))
    return pl.pallas_call(
        functools.partial(_qkv_latent_kernel, tn=tn),
        grid=(m // tm, QKV_W // tn),
        in_specs=[pl.BlockSpec((tm, k), lambda i, j: (i, 0)), pl.BlockSpec((k, tn), lambda i, j: (0, j))]
        + [tab_spec] * 6,
        out_specs=pl.BlockSpec((tm, tn), lambda i, j: (i, j)),
        out_shape=jax.ShapeDtypeStruct((m, QKV_W), BF16),
        compiler_params=_params("arbitrary", "arbitrary"),
        name="qkv_latent",
    )(h, w_qkv, *tabs)


def _merge_kernel(oa_ref, ob_ref, oc_ref, wa_ref, wb_ref, wc_ref, ga_ref, gb_ref, gc_ref, o_ref):
    y = ga_ref[...].astype(F32) * _dot(oa_ref[...], wa_ref[...])
    y += gb_ref[...].astype(F32) * _dot(ob_ref[...], wb_ref[...])
    y += gc_ref[...].astype(F32) * _dot(oc_ref[...], wc_ref[...])
    o_ref[...] = y.astype(o_ref.dtype)


def merge_branches(o_parts, gates, w_branch, tm=1024, tn=512):
    (oa, ja), (ob, jb), (oc, jc) = o_parts
    m = gates.shape[0]
    nj = D_MODEL // tn
    return pl.pallas_call(
        _merge_kernel,
        grid=(m // tm, nj),
        in_specs=[
            pl.BlockSpec((tm, MIX_A), lambda i, j: (i, ja)),
            pl.BlockSpec((tm, MIX_B), lambda i, j: (i, jb)),
            pl.BlockSpec((tm, MIX_C), lambda i, j: (i, jc)),
            pl.BlockSpec((MIX_A, tn), lambda i, j: (0, j)),
            pl.BlockSpec((MIX_B, tn), lambda i, j: (MIX_A // MIX_B, j)),
            pl.BlockSpec((MIX_C, tn), lambda i, j: ((MIX_A + MIX_B) // MIX_C, j)),
            pl.BlockSpec((tm, tn), lambda i, j: (i, j)),
            pl.BlockSpec((tm, tn), lambda i, j: (i, nj + j)),
            pl.BlockSpec((tm, tn), lambda i, j: (i, 2 * nj + j)),
        ],
        out_specs=pl.BlockSpec((tm, tn), lambda i, j: (i, j)),
        out_shape=jax.ShapeDtypeStruct((m, D_MODEL), BF16),
        compiler_params=_params("arbitrary", "arbitrary"),
        name="merge_branches",
    )(oa, ob, oc, w_branch, w_branch, w_branch, gates, gates, gates)


def _swiglu_kernel(x_ref, wg_ref, wu_ref, o_ref):
    x = x_ref[...]
    g = _dot(x, wg_ref[...])
    u = _dot(x, wu_ref[...])
    o_ref[...] = (g * jax.nn.sigmoid(g) * u).astype(o_ref.dtype)


def swiglu_up(h, w_gate_up, tm=2048, tn=256):
    m, k = h.shape
    nj = D_FF // tn
    return pl.pallas_call(
        _swiglu_kernel,
        grid=(m // tm, nj),
        in_specs=[
            pl.BlockSpec((tm, k), lambda i, j: (i, 0)),
            pl.BlockSpec((k, tn), lambda i, j: (0, j)),
            pl.BlockSpec((k, tn), lambda i, j: (0, nj + j)),
        ],
        out_specs=pl.BlockSpec((tm, tn), lambda i, j: (i, j)),
        out_shape=jax.ShapeDtypeStruct((m, D_FF), BF16),
        compiler_params=_params("arbitrary", "arbitrary"),
        name="swiglu_up",
    )(h, w_gate_up, w_gate_up)


def _diff_lambda(lam_ref, layer):
    lam_init = 0.8 - 0.6 * math.exp(-0.3 * layer)
    lf = lam_ref[...]
    a = jnp.sum(lf[0:1, :] * lf[1:2, :], axis=-1, keepdims=True)
    b = jnp.sum(lf[2:3, :] * lf[3:4, :], axis=-1, keepdims=True)
    return jnp.exp(a) - jnp.exp(b) + lam_init, lam_init


def _diff_finish(o, gain_ref, lam_init):
    o = o * lax.rsqrt(jnp.mean(o * o, axis=-1, keepdims=True) + RMS_EPS)
    return o * gain_ref[...] * (1.0 - lam_init)


def _split_halves(q):
    lane = lax.broadcasted_iota(jnp.int32, (1, HEAD_DIM), 1)
    zero = jnp.zeros_like(q)
    return jnp.where(lane < DB_HALF, q, zero), jnp.where(lane >= DB_HALF, q, zero)


def _sink_column(sink_ref, n, t):
    row = lax.broadcasted_iota(jnp.int32, (G_C * t, 1), 0)
    col = jnp.full((G_C * t, 1), sink_ref[n * G_C + G_C - 1], F32)
    for g in range(G_C - 2, -1, -1):
        col = jnp.where(row < (g + 1) * t, sink_ref[n * G_C + g], col)
    return col


def _softmax_parts(parts, extra=None, exp=jnp.exp):
    m = functools.reduce(jnp.maximum, [jnp.max(s, axis=-1, keepdims=True) for s in parts])
    if extra is not None:
        m = jnp.maximum(m, extra)
    ps = [exp(s - m) for s in parts]
    l = functools.reduce(jnp.add, [jnp.sum(p, axis=-1, keepdims=True) for p in ps])
    if extra is not None:
        l = l + exp(extra - m)
    return ps, l


def _context_attn_kernel(sink_ref, z_ref, lam_ref, gain_ref, o_ref, *, layer):
    scale = HEAD_DIM ** -0.5

    def col(off, h, width=HEAD_DIM):
        return z_ref[:, off + h * width: off + (h + 1) * width]

    for h in range(H_A):
        q, k, v = col(OFF_QA, h).astype(BF16), col(OFF_KA, h).astype(BF16), col(OFF_VA, h).astype(BF16)
        (p,), l = _softmax_parts([_dot_nt(q, k) * scale])
        o = _dot((p / l).astype(BF16), v)
        o_ref[:, h * HEAD_DIM:(h + 1) * HEAD_DIM] = o.astype(o_ref.dtype)

    lam, lam_init = _diff_lambda(lam_ref, layer)
    for h in range(H_B):
        q, k, v = col(OFF_QB, h).astype(BF16), col(OFF_KB, h).astype(BF16), col(OFF_VB, h).astype(BF16)
        q1, q2 = _split_halves(q)
        (p0,), l0 = _softmax_parts([_dot_nt(q1, k) * (DB_HALF ** -0.5)])
        (p1,), l1 = _softmax_parts([_dot_nt(q2, k) * (DB_HALF ** -0.5)])
        w = p0 / l0 - lam * (p1 / l1)
        o = _diff_finish(_dot(w.astype(BF16), v), gain_ref, lam_init)
        o_ref[:, MIX_A + h * HEAD_DIM: MIX_A + (h + 1) * HEAD_DIM] = o.astype(o_ref.dtype)

    t = z_ref.shape[0]
    for n in range(KV_C):
        q = jnp.concatenate([col(OFF_QC, n * G_C + g) for g in range(G_C)], axis=0).astype(BF16)
        k, v = col(OFF_KC, n).astype(BF16), col(OFF_VC, n).astype(BF16)
        sink = _sink_column(sink_ref, n, t)
        (p,), l = _softmax_parts([_dot_nt(q, k) * scale], extra=sink)
        o = _dot((p / l).astype(BF16), v)
        for g in range(G_C):
            c0 = MIX_A + MIX_B + (n * G_C + g) * HEAD_DIM
            o_ref[:, c0:c0 + HEAD_DIM] = o[g * t:(g + 1) * t].astype(o_ref.dtype)


def context_attention(z, seq, lam, gain, sink, layer):
    m = z.shape[0]
    return pl.pallas_call(
        functools.partial(_context_attn_kernel, layer=layer),
        grid=(m // seq,),
        in_specs=[
            pl.BlockSpec(memory_space=pltpu.SMEM),
            pl.BlockSpec((seq, QKV_W), lambda b: (b, 0)),
            pl.BlockSpec((4, DB_HALF), lambda b: (0, 0)),
            pl.BlockSpec((1, HEAD_DIM), lambda b: (0, 0)),
        ],
        out_specs=pl.BlockSpec((seq, D_MODEL), lambda b: (b, 0)),
        out_shape=jax.ShapeDtypeStruct((m, D_MODEL), BF16),
        compiler_params=_params("arbitrary"),
        name="context_attention",
    )(sink, z, lam, gain.reshape(1, HEAD_DIM))


def _na_bias_table(rpb_l):
    pad = GRID_W
    rp = jnp.pad(rpb_l.astype(F32) * LOG2E, ((0, 0), (0, 0), (pad, pad)))
    first = pad + NA_COLS - 1
    tz = jnp.stack([rp[:, :, first - qc:first - qc + GRID_W] for qc in range(GRID_W)], axis=2)
    qc, kc = np.arange(GRID_W)[:, None], np.arange(GRID_W)[None, :]
    ws = np.clip(qc - NA_COLS // 2, 0, GRID_W - NA_COLS)
    valid = (kc >= ws) & (kc < ws + NA_COLS)
    tz = jnp.where(jnp.asarray(valid)[None, None], tz, NEG_INF)
    return jnp.stack([jnp.concatenate([tz[:, e0 + j] for j in range(NA_ROWS)], axis=-1)
                      for e0 in range(NA_ROWS)], axis=1)


def _na_latent_kernel(q_ref, k_ref, v_ref, ck_ref, cv_ref, bias_ref, o_ref, *, rows):
    rb = pl.program_id(2)
    span = NA_ROWS * GRID_W
    q = q_ref[...]
    s_ctx = _dot_nt(q, ck_ref[...].astype(BF16))
    starts, s_loc = [], []
    for r in range(NA_QROWS):
        r_abs = rb * NA_QROWS + r
        rs = jnp.clip(r_abs - NA_ROWS // 2, 0, rows - NA_ROWS)
        e0 = rs - r_abs + NA_ROWS - 1
        ks = pl.multiple_of(rs * GRID_W, GRID_W)
        starts.append(ks)
        qr = q[r * GRID_W:(r + 1) * GRID_W]
        s_loc.append(_dot_nt(qr, k_ref[pl.ds(ks, span), :]) + bias_ref[e0])
    (p_loc, p_ctx), l = _softmax_parts([jnp.concatenate(s_loc, axis=0), s_ctx], exp=jnp.exp2)
    p_loc = p_loc.astype(BF16)
    o_loc = [_dot(p_loc[r * GRID_W:(r + 1) * GRID_W], v_ref[pl.ds(starts[r], span), :]) for r in range(NA_QROWS)]
    o = jnp.concatenate(o_loc, axis=0) + _dot(p_ctx.astype(BF16), cv_ref[...].astype(BF16))
    o_ref[...] = (o / l).astype(o_ref.dtype)


def na_latent(z, cache_k, cache_v, layer, bias, batch, seq):
    rows = seq // GRID_W
    assert rows % NA_QROWS == 0 and rows >= NA_ROWS
    nrb = rows // NA_QROWS
    tq = NA_QROWS * GRID_W
    n_ctx = cache_k.shape[2]
    ck = cache_k.reshape(batch, DEPTH, n_ctx, MIX_A)
    cv = cache_v.reshape(batch, DEPTH, n_ctx, MIX_A)
    ctx_spec = pl.BlockSpec((None, None, n_ctx, HEAD_DIM), lambda b, h, rb: (b, layer, 0, h))
    return pl.pallas_call(
        functools.partial(_na_latent_kernel, rows=rows),
        grid=(batch, H_A, nrb),
        in_specs=[
            pl.BlockSpec((tq, HEAD_DIM), lambda b, h, rb: (b * nrb + rb, OFF_QA // HEAD_DIM + h)),
            pl.BlockSpec((seq, HEAD_DIM), lambda b, h, rb: (b, OFF_KA // HEAD_DIM + h)),
            pl.BlockSpec((seq, HEAD_DIM), lambda b, h, rb: (b, OFF_VA // HEAD_DIM + h)),
            ctx_spec, ctx_spec,
            pl.BlockSpec((None, NA_ROWS, GRID_W, NA_ROWS * GRID_W), lambda b, h, rb: (h, 0, 0, 0)),
        ],
        out_specs=pl.BlockSpec((tq, HEAD_DIM), lambda b, h, rb: (b * nrb + rb, h)),
        out_shape=jax.ShapeDtypeStruct((batch * seq, MIX_A), BF16),
        compiler_params=_params("arbitrary", "arbitrary", "arbitrary"),
        name="na_latent",
    )(z, z, z, ck, cv, bias)


def _diff_latent_kernel(lam_ref, gain_ref, q_ref, k_ref, v_ref, ck_ref, cv_ref, o_ref, *, layer):
    lam, lam_init = _diff_lambda(lam_ref, layer)
    k = k_ref[...]
    ck = ck_ref[...].astype(BF16)
    cv = cv_ref[...].astype(BF16)
    nsub = q_ref.shape[0] // DIFF_TQ
    rows = [slice(u * DIFF_TQ, (u + 1) * DIFF_TQ) for u in range(nsub)]
    scores = [[[_dot_nt(qh, k), _dot_nt(qh, ck)] for qh in _split_halves(q_ref[r, :])] for r in rows]
    stats = [[_softmax_parts(sc, exp=jnp.exp2) for sc in su] for su in scores]
    for r, ((p0, l0), (p1, l1)) in zip(rows, stats):
        c = lam * l0 / l1
        wl = (p0[0] - p1[0] * c).astype(BF16)
        wc = (p0[1] - p1[1] * c).astype(BF16)
        o = (_dot(wl, v_ref[...]) + _dot(wc, cv)) / l0
        o_ref[r, :] = _diff_finish(o, gain_ref, lam_init).astype(o_ref.dtype)


def diff_latent(z, cache_k, cache_v, layer, lam, gain, batch, seq):
    tq = DIFF_TQ * DIFF_NSUB
    n_ctx = cache_k.shape[2]
    ck = cache_k.reshape(batch, DEPTH, n_ctx, MIX_B)
    cv = cache_v.reshape(batch, DEPTH, n_ctx, MIX_B)
    nqt = seq // tq
    ctx_spec = pl.BlockSpec((None, None, n_ctx, HEAD_DIM), lambda b, h, t: (b, layer, 0, h))
    return pl.pallas_call(
        functools.partial(_diff_latent_kernel, layer=layer),
        grid=(batch, H_B, nqt),
        in_specs=[
            pl.BlockSpec((4, DB_HALF), lambda b, h, t: (0, 0)),
            pl.BlockSpec((1, HEAD_DIM), lambda b, h, t: (0, 0)),
            pl.BlockSpec((tq, HEAD_DIM), lambda b, h, t: (b * nqt + t, OFF_QB // HEAD_DIM + h)),
            pl.BlockSpec((seq, HEAD_DIM), lambda b, h, t: (b, OFF_KB // HEAD_DIM + h)),
            pl.BlockSpec((seq, HEAD_DIM), lambda b, h, t: (b, OFF_VB // HEAD_DIM + h)),
            ctx_spec, ctx_spec,
        ],
        out_specs=pl.BlockSpec((tq, HEAD_DIM), lambda b, h, t: (b * nqt + t, h)),
        out_shape=jax.ShapeDtypeStruct((batch * seq, MIX_B), BF16),
        compiler_params=_params("arbitrary", "arbitrary", "arbitrary"),
        name="diff_latent",
    )(lam, gain.reshape(1, HEAD_DIM), z, z, z, ck, cv)


def _window_mask_table():
    i, j = np.arange(SW_SUB)[:, None], np.arange(SW_SPAN)[None, :]
    tabs = [np.where(np.abs(c * WINDOW_C + i - j) <= WINDOW_C, 0.0, NEG_INF) for c in range(3)]
    return jnp.asarray(np.stack(tabs), F32)


def _window_latent_kernel(sink_ref, mask_ref, q_ref, k_ref, v_ref, ck_ref, cv_ref, o_ref, *, seq):
    n = pl.program_id(1)
    t0 = pl.program_id(2) * SW_TQ
    ck = ck_ref[...].astype(BF16)
    cv = cv_ref[...].astype(BF16)
    sink = _sink_column(sink_ref, n, SW_SUB) * LOG2E
    for u in range(SW_TQ // SW_SUB):
        rows = slice(u * SW_SUB, (u + 1) * SW_SUB)
        tu = t0 + u * SW_SUB
        ks = pl.multiple_of(jnp.clip(tu - WINDOW_C, 0, seq - SW_SPAN), WINDOW_C)
        case = (tu - ks) // WINDOW_C
        q = jnp.concatenate([q_ref[rows, g * HEAD_DIM:(g + 1) * HEAD_DIM] for g in range(G_C)], axis=0)
        s_loc = _dot_nt(q, k_ref[pl.ds(ks, SW_SPAN), :]).reshape(G_C, SW_SUB, SW_SPAN) + mask_ref[case][None]
        s_loc = s_loc.reshape(G_C * SW_SUB, SW_SPAN)
        (p_loc, p_ctx), l = _softmax_parts([s_loc, _dot_nt(q, ck)], extra=sink, exp=jnp.exp2)
        o = (_dot(p_loc.astype(BF16), v_ref[pl.ds(ks, SW_SPAN), :]) + _dot(p_ctx.astype(BF16), cv)) / l
        for g in range(G_C):
            o_ref[rows, g * HEAD_DIM:(g + 1) * HEAD_DIM] = o[g * SW_SUB:(g + 1) * SW_SUB].astype(o_ref.dtype)


def window_latent(z, cache_k, cache_v, layer, sink, batch, seq):
    assert seq % SW_TQ == 0 and seq >= SW_SPAN and SW_TQ % SW_SUB == 0
    n_ctx = cache_k.shape[2]
    ck = cache_k.reshape(batch, DEPTH, n_ctx, KVW_C)
    cv = cache_v.reshape(batch, DEPTH, n_ctx, KVW_C)
    nqt = seq // SW_TQ
    gw = G_C * HEAD_DIM
    ctx_spec = pl.BlockSpec((None, None, n_ctx, HEAD_DIM), lambda b, n, t: (b, layer, 0, n))
    return pl.pallas_call(
        functools.partial(_window_latent_kernel, seq=seq),
        grid=(batch, KV_C, nqt),
        in_specs=[
            pl.BlockSpec(memory_space=pltpu.SMEM),
            pl.BlockSpec((3, SW_SUB, SW_SPAN), lambda b, n, t: (0, 0, 0)),
            pl.BlockSpec((SW_TQ, gw), lambda b, n, t: (b * nqt + t, OFF_QC // gw + n)),
            pl.BlockSpec((seq, HEAD_DIM), lambda b, n, t: (b, OFF_KC // HEAD_DIM + n)),
            pl.BlockSpec((seq, HEAD_DIM), lambda b, n, t: (b, OFF_VC // HEAD_DIM + n)),
            ctx_spec, ctx_spec,
        ],
        out_specs=pl.BlockSpec((SW_TQ, gw), lambda b, n, t: (b * nqt + t, n)),
        out_shape=jax.ShapeDtypeStruct((batch * seq, MIX_C), BF16),
        compiler_params=_params("arbitrary", "arbitrary", "arbitrary"),
        name="window_latent",
    )(sink, _window_mask_table(), z, z, z, ck, cv)


def _prompt_row(tm, i):
    return 0


def kernel(x_prompt, x_sample, cache_a_k, cache_a_v, cache_b_k, cache_b_v, cache_c_k, cache_c_v, c, c_ctx,
           w_ada, b_ada, w_in, rpb_a, lam_b, subln_b, sink_c, w_branch, w_out, ln_g, ln_b, w_gate_up, w_down):
    bp, tp, _ = x_prompt.shape
    bs, ts, _ = x_sample.shape

    def sample_row(tm, i):
        return 1 + (i * tm) // ts

    streams = {
        "p": dict(x=x_prompt.reshape(bp * tp, D_MODEL), rowfn=_prompt_row),
        "s": dict(x=x_sample.reshape(bs * ts, D_MODEL), rowfn=sample_row),
    }
    cc = jnp.concatenate([c_ctx[None, :], c], axis=0)
    cc = jnp.pad(cc, ((0, (-cc.shape[0]) % 8), (0, 0)))
    mod = ada_modulation_all(cc, w_ada, b_ada)

    for st in streams.values():
        st["h"] = modulate_cast(st["x"], mod, 0, 0, st["rowfn"])

    new_kv = [[] for _ in range(6)]
    for l in range(DEPTH):
        w_qkv = cast_weight(w_in, l, 0, QKV_W)
        w_g = cast_weight(w_in, l, QKV_W)
        w_br = cast_weight(w_branch, l)
        w_o = cast_weight(w_out, l)
        w_gu = cast_weight(w_gate_up, l)
        w_dn = cast_weight(w_down, l)
        na_bias = _na_bias_table(rpb_a[l])

        sp = streams["p"]
        zp = matmul(sp["h"], w_qkv, F32, 1024, 1024, name="qkv_context")
        for lst, off, wdt, shp in (
                (new_kv[0], OFF_KA, MIX_A, (H_A, HEAD_DIM)), (new_kv[1], OFF_VA, MIX_A, (H_A, HEAD_DIM)),
                (new_kv[2], OFF_KB, MIX_B, (H_B, 2, DB_HALF)), (new_kv[3], OFF_VB, MIX_B, (H_B, HEAD_DIM)),
                (new_kv[4], OFF_KC, KVW_C, (KV_C, HEAD_DIM)), (new_kv[5], OFF_VC, KVW_C, (KV_C, HEAD_DIM))):
            lst.append(zp[:, off:off + wdt].reshape((bp, tp) + shp))
        op = context_attention(zp, tp, lam_b[l], subln_b[l], sink_c[l], l)
        sp["o"] = ((op, 0), (op, MIX_A // MIX_B), (op, (MIX_A + MIX_B) // MIX_C))

        ss = streams["s"]
        zs = qkv_latent(ss["h"], w_qkv, ts)
        oa = na_latent(zs, cache_a_k, cache_a_v, l, na_bias, bs, ts)
        ob = diff_latent(zs, cache_b_k, cache_b_v, l, lam_b[l], subln_b[l], bs, ts)
        oc = window_latent(zs, cache_c_k, cache_c_v, l, sink_c[l], bs, ts)
        ss["o"] = ((oa, 0), (ob, 0), (oc, 0))

        for st in streams.values():
            gates = matmul(st["h"], w_g, BF16, 1024, 1024, sigmoid=True, name="branch_gates")
            u = merge_branches(st["o"], gates, w_br)
            v = matmul_residual(u, w_o, st["x"], mod, l, 0, st["rowfn"], 1024, 1024, "mixer_out")
            st["x"], h2 = post_norm(v, mod, ln_g[l, 0], ln_b[l, 0], st["rowfn"], next_mod=(l, 1))
            a = swiglu_up(h2, w_gu)
            v = matmul_residual(a, w_dn, st["x"], mod, l, 1, st["rowfn"], 512, 512, "ffn_down")
            nxt = (l + 1, 0) if l + 1 < DEPTH else None
            st["x"], st["h"] = post_norm(v, mod, ln_g[l, 1], ln_b[l, 1], st["rowfn"], next_mod=nxt)

    outs = [jnp.stack(lst, axis=1) for lst in new_kv]
    return (streams["p"]["x"].reshape(bp, tp, D_MODEL), streams["s"]["x"].reshape(bs, ts, D_MODEL), *outs)
```

```python
import functools
import math

import numpy as np
import jax
import jax.numpy as jnp
from jax import lax
from jax.experimental import pallas as pl
from jax.experimental.pallas import tpu as pltpu

D_MODEL = 4096
DEPTH = 2
GRID_W = 64
HEAD_DIM = 128
H_A = 8
MIX_A = H_A * HEAD_DIM
NA_ROWS = 8
NA_COLS = 16
H_B = 8
DB_HALF = HEAD_DIM // 2
MIX_B = H_B * HEAD_DIM
H_C = 16
KV_C = 4
G_C = H_C // KV_C
MIX_C = H_C * HEAD_DIM
KVW_C = KV_C * HEAD_DIM
WINDOW_C = 128
N_BRANCH = 3
QKV_W = 3 * MIX_A + 3 * MIX_B + MIX_C + 2 * KVW_C
D_FF = 11008
ROPE_BASE = 10000.0
LN_EPS = 1e-5
RMS_EPS = 1e-5
ALPHA = (2 * DEPTH) ** 0.25
NEG_INF = -1e30
LOG2E = math.log2(math.e)
BF16 = jnp.bfloat16
F32 = jnp.float32

OFF_QA, OFF_KA, OFF_VA = 0, MIX_A, 2 * MIX_A
OFF_QB, OFF_KB, OFF_VB = 3 * MIX_A, 3 * MIX_A + MIX_B, 3 * MIX_A + 2 * MIX_B
OFF_QC = 3 * MIX_A + 3 * MIX_B
OFF_KC = OFF_QC + MIX_C
OFF_VC = OFF_KC + KVW_C

V7X_VMEM_LIMIT = 56 * 1024 * 1024
MXU_COLS = 256

NA_QROWS = 8
NA_HEADS = 4
SW_TQ = 1024
SW_SUB = WINDOW_C
SW_SPAN = 3 * SW_SUB
DIFF_TQ = 128
DIFF_NSUB = 8


def _params(*sem):
    return pltpu.CompilerParams(dimension_semantics=sem, vmem_limit_bytes=V7X_VMEM_LIMIT)


def _dot(a, b):
    return jnp.dot(a, b, preferred_element_type=F32)


def _dot_nt(a, b):
    return lax.dot_general(a, b, (((1,), (1,)), ((), ())), preferred_element_type=F32)


def _cast_kernel(w_ref, o_ref):
    o_ref[...] = w_ref[...].astype(o_ref.dtype)


def cast_weight(w, layer, col0=0, ncols=None):
    _, r, c = w.shape
    ncols = c - col0 if ncols is None else ncols
    tc = 512 if r <= D_MODEL else HEAD_DIM
    assert col0 % tc == 0 and ncols % tc == 0
    return pl.pallas_call(
        _cast_kernel,
        grid=(ncols // tc,),
        in_specs=[pl.BlockSpec((None, r, tc), lambda j: (layer, 0, col0 // tc + j))],
        out_specs=pl.BlockSpec((r, tc), lambda j: (0, j)),
        out_shape=jax.ShapeDtypeStruct((r, ncols), BF16),
        compiler_params=_params("arbitrary"),
        name="cast_weight",
    )(w)


def _ada_kernel(c_ref, w_ref, b_ref, o_ref):
    cv = c_ref[...]
    s = (cv * jax.nn.sigmoid(cv)).astype(BF16)
    o_ref[...] = _dot(s, w_ref[...].astype(BF16)) + b_ref[...]


def ada_modulation_all(cc, w_ada, b_ada):
    r = cc.shape[0]
    n = 6 * D_MODEL
    tn = 512
    out = pl.pallas_call(
        _ada_kernel,
        grid=(DEPTH, n // tn),
        in_specs=[
            pl.BlockSpec((r, D_MODEL), lambda l, j: (0, 0)),
            pl.BlockSpec((None, D_MODEL, tn), lambda l, j: (l, 0, j)),
            pl.BlockSpec((None, 1, tn), lambda l, j: (l, 0, j)),
        ],
        out_specs=pl.BlockSpec((None, r, tn), lambda l, j: (l, 0, j)),
        out_shape=jax.ShapeDtypeStruct((DEPTH, r, n), F32),
        compiler_params=_params("arbitrary", "arbitrary"),
        name="ada_modulation",
    )(cc, w_ada, b_ada.reshape(DEPTH, 1, n))
    return out.reshape(DEPTH, r, 6, 1, D_MODEL)


def _mod_spec(layer, chunk, rowfn):
    return pl.BlockSpec((None, None, None, 1, D_MODEL), lambda i: (layer, rowfn(i), chunk, 0, 0))


def _modulate_kernel(x_ref, sc_ref, sh_ref, o_ref):
    o_ref[...] = (x_ref[...] * (1.0 + sc_ref[...]) + sh_ref[...]).astype(BF16)


def modulate_cast(x, mod, layer, idx, rowfn_of_tile, tm=512):
    m = x.shape[0]
    rowfn = functools.partial(rowfn_of_tile, tm)
    return pl.pallas_call(
        _modulate_kernel,
        grid=(m // tm,),
        in_specs=[
            pl.BlockSpec((tm, D_MODEL), lambda i: (i, 0)),
            _mod_spec(layer, 3 * idx + 1, rowfn),
            _mod_spec(layer, 3 * idx, rowfn),
        ],
        out_specs=pl.BlockSpec((tm, D_MODEL), lambda i: (i, 0)),
        out_shape=jax.ShapeDtypeStruct((m, D_MODEL), BF16),
        compiler_params=_params("arbitrary"),
        name="modulate",
    )(x, mod, mod)


def _ln_core(v_ref, g_ref, b_ref):
    v = v_ref[...]
    mu = jnp.mean(v, axis=-1, keepdims=True)
    d = v - mu
    var = jnp.mean(d * d, axis=-1, keepdims=True)
    return d * lax.rsqrt(var + LN_EPS) * g_ref[...] + b_ref[...]


def _ln_mod_kernel(v_ref, g_ref, b_ref, sc_ref, sh_ref, xo_ref, ho_ref):
    xn = _ln_core(v_ref, g_ref, b_ref)
    xo_ref[...] = xn
    ho_ref[...] = (xn * (1.0 + sc_ref[...]) + sh_ref[...]).astype(BF16)


def _ln_kernel(v_ref, g_ref, b_ref, xo_ref):
    xo_ref[...] = _ln_core(v_ref, g_ref, b_ref)


def post_norm(v, mod, ln_g, ln_b, rowfn_of_tile, next_mod=None, tm=256):
    m = v.shape[0]
    rowfn = functools.partial(rowfn_of_tile, tm)
    row_spec = pl.BlockSpec((tm, D_MODEL), lambda i: (i, 0))
    vec_spec = pl.BlockSpec((1, D_MODEL), lambda i: (0, 0))
    in_specs = [row_spec, vec_spec, vec_spec]
    args = [v, ln_g.reshape(1, D_MODEL), ln_b.reshape(1, D_MODEL)]
    if next_mod is None:
        return pl.pallas_call(
            _ln_kernel, grid=(m // tm,), in_specs=in_specs, out_specs=row_spec,
            out_shape=jax.ShapeDtypeStruct((m, D_MODEL), F32),
            compiler_params=_params("arbitrary"), name="post_norm",
        )(*args), None
    nl, nidx = next_mod
    in_specs += [_mod_spec(nl, 3 * nidx + 1, rowfn), _mod_spec(nl, 3 * nidx, rowfn)]
    args += [mod, mod]
    return pl.pallas_call(
        _ln_mod_kernel, grid=(m // tm,), in_specs=in_specs, out_specs=[row_spec, row_spec],
        out_shape=[jax.ShapeDtypeStruct((m, D_MODEL), F32), jax.ShapeDtypeStruct((m, D_MODEL), BF16)],
        compiler_params=_params("arbitrary"), name="post_norm_modulate",
    )(*args)


def _mm_kernel(x_ref, w_ref, o_ref, *, sigmoid):
    acc = _dot(x_ref[...], w_ref[...])
    if sigmoid:
        acc = jax.nn.sigmoid(acc)
    o_ref[...] = acc.astype(o_ref.dtype)


def matmul(x, w, out_dtype, tm, tn, sigmoid=False, name="matmul"):
    m, k = x.shape
    n = w.shape[1]
    return pl.pallas_call(
        functools.partial(_mm_kernel, sigmoid=sigmoid),
        grid=(m // tm, n // tn),
        in_specs=[pl.BlockSpec((tm, k), lambda i, j: (i, 0)), pl.BlockSpec((k, tn), lambda i, j: (0, j))],
        out_specs=pl.BlockSpec((tm, tn), lambda i, j: (i, j)),
        out_shape=jax.ShapeDtypeStruct((m, n), out_dtype),
        compiler_params=_params("arbitrary", "arbitrary"),
        name=name,
    )(x, w)


def _mm_residual_kernel(x_ref, w_ref, res_ref, gate_ref, o_ref):
    o_ref[...] = ALPHA * res_ref[...] + gate_ref[...] * _dot(x_ref[...], w_ref[...])


def matmul_residual(x, w, res, mod, layer, idx, rowfn_of_tile, tm, tn, name):
    m, k = x.shape
    n = w.shape[1]
    rowfn = functools.partial(rowfn_of_tile, tm)
    gate_spec = pl.BlockSpec((None, None, None, 1, tn), lambda i, j: (layer, rowfn(i), 3 * idx + 2, 0, j))
    tile_spec = pl.BlockSpec((tm, tn), lambda i, j: (i, j))
    return pl.pallas_call(
        _mm_residual_kernel,
        grid=(m // tm, n // tn),
        in_specs=[pl.BlockSpec((tm, k), lambda i, j: (i, 0)), pl.BlockSpec((k, tn), lambda i, j: (0, j)),
                  tile_spec, gate_spec],
        out_specs=tile_spec,
        out_shape=jax.ShapeDtypeStruct((m, n), F32),
        compiler_params=_params("arbitrary", "arbitrary"),
        name=name,
    )(x, w, res, mod)


def _rope_tables(t, width):
    half, quarter = width // 2, width // 4
    lane = np.arange(HEAD_DIM)
    e = lane % width
    pos = np.arange(t)
    p = np.where((e < half)[None, :], (pos // GRID_W)[:, None], (pos % GRID_W)[:, None]).astype(np.float64)
    inv = np.exp(-math.log(ROPE_BASE) * (e % quarter).astype(np.float64) / quarter)
    ang = (p.astype(np.float32) * inv.astype(np.float32)[None, :]).astype(np.float64)
    first = ((e % half) < quarter)[None, :]
    cos, sin = np.cos(ang), np.sin(ang)
    return (jnp.asarray(cos, F32), jnp.asarray(np.where(first, -sin, 0.0), F32),
            jnp.asarray(np.where(first, 0.0, sin), F32))


def _rope(x, cos, s_lo, s_hi, quarter):
    return x * cos + pltpu.roll(x, HEAD_DIM - quarter, 1) * s_lo + pltpu.roll(x, quarter, 1) * s_hi


def _qkv_latent_kernel(x_ref, w_ref, cb_ref, lb_ref, hb_ref, cc_ref, lc_ref, hc_ref, o_ref, *, tn):
    j = pl.program_id(1)
    acc = _dot(x_ref[...], w_ref[...])
    nchunk = tn // HEAD_DIM
    tiles_qb = (OFF_QB // tn,)
    tiles_kb = (OFF_KB // tn,)
    tiles_qc = tuple(range(OFF_QC // tn, OFF_KC // tn))
    tile_kcvc = OFF_KC // tn

    def is_in(tiles):
        c = j == tiles[0]
        for tt in tiles[1:]:
            c = jnp.logical_or(c, j == tt)
        return c

    def store(fn_of_chunk):
        for ch in range(nchunk):
            sl = slice(ch * HEAD_DIM, (ch + 1) * HEAD_DIM)
            o_ref[:, sl] = fn_of_chunk(ch, acc[:, sl]).astype(o_ref.dtype)

    rope_b = lambda x: _rope(x, cb_ref[...], lb_ref[...], hb_ref[...], DB_HALF // 4)
    rope_c = lambda x: _rope(x, cc_ref[...], lc_ref[...], hc_ref[...], HEAD_DIM // 4)
    tiles_qa = tuple(range(OFF_QA // tn, OFF_KA // tn))
    special = is_in(tiles_qa + tiles_qb + tiles_kb + tiles_qc + (tile_kcvc,))

    @pl.when(jnp.logical_not(special))
    def _():
        o_ref[...] = acc.astype(o_ref.dtype)

    @pl.when(is_in(tiles_qa))
    def _():
        o_ref[...] = (acc * (HEAD_DIM ** -0.5 * LOG2E)).astype(o_ref.dtype)

    @pl.when(is_in(tiles_qb))
    def _():
        store(lambda ch, x: rope_b(x) * (DB_HALF ** -0.5 * LOG2E))

    @pl.when(is_in(tiles_kb))
    def _():
        store(lambda ch, x: rope_b(x))

    @pl.when(is_in(tiles_qc))
    def _():
        store(lambda ch, x: rope_c(x) * (HEAD_DIM ** -0.5 * LOG2E))

    @pl.when(j == tile_kcvc)
    def _():
        store(lambda ch, x: rope_c(x) if ch < KVW_C // HEAD_DIM else x)


def qkv_latent(h, w_qkv, seq, tm=1024, tn=1024):
    m, k = h.shape
    assert tn == MIX_A and OFF_KC % tn == 0 and (OFF_VC - OFF_KC) == KVW_C and seq % tm == 0
    tabs = _rope_tables(seq, DB_HALF) + _rope_tables(seq, HEAD_DIM)
    per_seq = seq // tm
    tab_spec = pl.BlockSpec((tm, HEAD_DIM), lambda i, j: (i % per_seq, 0))
    return pl.pallas_call(
        functools.partial(_qkv_latent_kernel, tn=tn),
        grid=(m // tm, QKV_W // tn),
        in_specs=[pl.BlockSpec((tm, k), lambda i, j: (i, 0)), pl.BlockSpec((k, tn), lambda i, j: (0, j))]
        + [tab_spec] * 6,
        out_specs=pl.BlockSpec((tm, tn), lambda i, j: (i, j)),
        out_shape=jax.ShapeDtypeStruct((m, QKV_W), BF16),
        compiler_params=_params("arbitrary", "arbitrary"),
        name="qkv_latent",
    )(h, w_qkv, *tabs)


def _merge_kernel(oa_ref, ob_ref, oc_ref, wa_ref, wb_ref, wc_ref, ga_ref, gb_ref, gc_ref, o_ref):
    y = ga_ref[...].astype(F32) * _dot(oa_ref[...], wa_ref[...])
    y += gb_ref[...].astype(F32) * _dot(ob_ref[...], wb_ref[...])
    y += gc_ref[...].astype(F32) * _dot(oc_ref[...], wc_ref[...])
    o_ref[...] = y.astype(o_ref.dtype)


def merge_branches(o_parts, gates, w_branch, tm=1024, tn=1024):
    (oa, ja), (ob, jb), (oc, jc) = o_parts
    m = gates.shape[0]
    nj = D_MODEL // tn
    return pl.pallas_call(
        _merge_kernel,
        grid=(m // tm, nj),
        in_specs=[
            pl.BlockSpec((tm, MIX_A), lambda i, j: (i, ja)),
            pl.BlockSpec((tm, MIX_B), lambda i, j: (i, jb)),
            pl.BlockSpec((tm, MIX_C), lambda i, j: (i, jc)),
            pl.BlockSpec((MIX_A, tn), lambda i, j: (0, j)),
            pl.BlockSpec((MIX_B, tn), lambda i, j: (MIX_A // MIX_B, j)),
            pl.BlockSpec((MIX_C, tn), lambda i, j: ((MIX_A + MIX_B) // MIX_C, j)),
            pl.BlockSpec((tm, tn), lambda i, j: (i, j)),
            pl.BlockSpec((tm, tn), lambda i, j: (i, nj + j)),
            pl.BlockSpec((tm, tn), lambda i, j: (i, 2 * nj + j)),
        ],
        out_specs=pl.BlockSpec((tm, tn), lambda i, j: (i, j)),
        out_shape=jax.ShapeDtypeStruct((m, D_MODEL), BF16),
        compiler_params=_params("arbitrary", "arbitrary"),
        name="merge_branches",
    )(oa, ob, oc, w_branch, w_branch, w_branch, gates, gates, gates)


def _swiglu_kernel(x_ref, wg_ref, wu_ref, o_ref):
    x = x_ref[...]
    g = _dot(x, wg_ref[...])
    u = _dot(x, wu_ref[...])
    o_ref[...] = (g * jax.nn.sigmoid(g) * u).astype(o_ref.dtype)


def swiglu_up(h, w_gate_up, tm=2048, tn=256):
    m, k = h.shape
    nj = D_FF // tn
    return pl.pallas_call(
        _swiglu_kernel,
        grid=(m // tm, nj),
        in_specs=[
            pl.BlockSpec((tm, k), lambda i, j: (i, 0)),
            pl.BlockSpec((k, tn), lambda i, j: (0, j)),
            pl.BlockSpec((k, tn), lambda i, j: (0, nj + j)),
        ],
        out_specs=pl.BlockSpec((tm, tn), lambda i, j: (i, j)),
        out_shape=jax.ShapeDtypeStruct((m, D_FF), BF16),
        compiler_params=_params("arbitrary", "arbitrary"),
        name="swiglu_up",
    )(h, w_gate_up, w_gate_up)


def _diff_lambda(lam_ref, layer):
    lam_init = 0.8 - 0.6 * math.exp(-0.3 * layer)
    lf = lam_ref[...]
    a = jnp.sum(lf[0:1, :] * lf[1:2, :], axis=-1, keepdims=True)
    b = jnp.sum(lf[2:3, :] * lf[3:4, :], axis=-1, keepdims=True)
    return jnp.exp(a) - jnp.exp(b) + lam_init, lam_init


def _diff_finish(o, gain_ref, lam_init):
    o = o * lax.rsqrt(jnp.mean(o * o, axis=-1, keepdims=True) + RMS_EPS)
    return o * gain_ref[...] * (1.0 - lam_init)


def _split_halves(q):
    lane = lax.broadcasted_iota(jnp.int32, (1, HEAD_DIM), 1)
    zero = jnp.zeros_like(q)
    return jnp.where(lane < DB_HALF, q, zero), jnp.where(lane >= DB_HALF, q, zero)


def _sink_column(sink_ref, n, t):
    row = lax.broadcasted_iota(jnp.int32, (G_C * t, 1), 0)
    col = jnp.full((G_C * t, 1), sink_ref[n * G_C + G_C - 1], F32)
    for g in range(G_C - 2, -1, -1):
        col = jnp.where(row < (g + 1) * t, sink_ref[n * G_C + g], col)
    return col


def _softmax_parts(parts, extra=None, exp=jnp.exp):
    m = functools.reduce(jnp.maximum, [jnp.max(s, axis=-1, keepdims=True) for s in parts])
    if extra is not None:
        m = jnp.maximum(m, extra)
    ps = [exp(s - m) for s in parts]
    l = functools.reduce(jnp.add, [jnp.sum(p, axis=-1, keepdims=True) for p in ps])
    if extra is not None:
        l = l + exp(extra - m)
    return ps, l


def _context_attn_kernel(sink_ref, z_ref, lam_ref, gain_ref, o_ref, *, layer):
    scale = HEAD_DIM ** -0.5

    def col(off, h, width=HEAD_DIM):
        return z_ref[:, off + h * width: off + (h + 1) * width]

    for h in range(H_A):
        q, k, v = col(OFF_QA, h).astype(BF16), col(OFF_KA, h).astype(BF16), col(OFF_VA, h).astype(BF16)
        (p,), l = _softmax_parts([_dot_nt(q, k) * scale])
        o = _dot((p / l).astype(BF16), v)
        o_ref[:, h * HEAD_DIM:(h + 1) * HEAD_DIM] = o.astype(o_ref.dtype)

    lam, lam_init = _diff_lambda(lam_ref, layer)
    for h in range(H_B):
        q, k, v = col(OFF_QB, h).astype(BF16), col(OFF_KB, h).astype(BF16), col(OFF_VB, h).astype(BF16)
        q1, q2 = _split_halves(q)
        (p0,), l0 = _softmax_parts([_dot_nt(q1, k) * (DB_HALF ** -0.5)])
        (p1,), l1 = _softmax_parts([_dot_nt(q2, k) * (DB_HALF ** -0.5)])
        w = p0 / l0 - lam * (p1 / l1)
        o = _diff_finish(_dot(w.astype(BF16), v), gain_ref, lam_init)
        o_ref[:, MIX_A + h * HEAD_DIM: MIX_A + (h + 1) * HEAD_DIM] = o.astype(o_ref.dtype)

    t = z_ref.shape[0]
    for n in range(KV_C):
        q = jnp.concatenate([col(OFF_QC, n * G_C + g) for g in range(G_C)], axis=0).astype(BF16)
        k, v = col(OFF_KC, n).astype(BF16), col(OFF_VC, n).astype(BF16)
        sink = _sink_column(sink_ref, n, t)
        (p,), l = _softmax_parts([_dot_nt(q, k) * scale], extra=sink)
        o = _dot((p / l).astype(BF16), v)
        for g in range(G_C):
            c0 = MIX_A + MIX_B + (n * G_C + g) * HEAD_DIM
            o_ref[:, c0:c0 + HEAD_DIM] = o[g * t:(g + 1) * t].astype(o_ref.dtype)


def context_attention(z, seq, lam, gain, sink, layer):
    m = z.shape[0]
    return pl.pallas_call(
        functools.partial(_context_attn_kernel, layer=layer),
        grid=(m // seq,),
        in_specs=[
            pl.BlockSpec(memory_space=pltpu.SMEM),
            pl.BlockSpec((seq, QKV_W), lambda b: (b, 0)),
            pl.BlockSpec((4, DB_HALF), lambda b: (0, 0)),
            pl.BlockSpec((1, HEAD_DIM), lambda b: (0, 0)),
        ],
        out_specs=pl.BlockSpec((seq, D_MODEL), lambda b: (b, 0)),
        out_shape=jax.ShapeDtypeStruct((m, D_MODEL), BF16),
        compiler_params=_params("arbitrary"),
        name="context_attention",
    )(sink, z, lam, gain.reshape(1, HEAD_DIM))


def _na_bias_table(rpb_l):
    pad = GRID_W
    rp = jnp.pad(rpb_l.astype(F32) * LOG2E, ((0, 0), (0, 0), (pad, pad)))
    first = pad + NA_COLS - 1
    tz = jnp.stack([rp[:, :, first - qc:first - qc + GRID_W] for qc in range(GRID_W)], axis=2)
    qc, kc = np.arange(GRID_W)[:, None], np.arange(GRID_W)[None, :]
    ws = np.clip(qc - NA_COLS // 2, 0, GRID_W - NA_COLS)
    valid = (kc >= ws) & (kc < ws + NA_COLS)
    tz = jnp.where(jnp.asarray(valid)[None, None], tz, NEG_INF)
    return jnp.stack([jnp.concatenate([tz[:, e0 + j] for j in range(NA_ROWS)], axis=-1)
                      for e0 in range(NA_ROWS)], axis=1)


def _na_latent_kernel(q_ref, k_ref, v_ref, ck_ref, cv_ref, bias_ref, o_ref, *, rows):
    rb = pl.program_id(2)
    span = NA_ROWS * GRID_W
    starts, biases = [], []
    for r in range(NA_QROWS):
        r_abs = rb * NA_QROWS + r
        rs = jnp.clip(r_abs - NA_ROWS // 2, 0, rows - NA_ROWS)
        starts.append(pl.multiple_of(rs * GRID_W, GRID_W))
        biases.append(rs - r_abs + NA_ROWS - 1)
    heads = [slice(hh * HEAD_DIM, (hh + 1) * HEAD_DIM) for hh in range(NA_HEADS)]
    scores = []
    for hh, cols in enumerate(heads):
        q = q_ref[:, cols]
        s_loc = [_dot_nt(q[r * GRID_W:(r + 1) * GRID_W], k_ref[pl.ds(starts[r], span), cols]) + bias_ref[hh, biases[r]]
                 for r in range(NA_QROWS)]
        scores.append([jnp.concatenate(s_loc, axis=0), _dot_nt(q, ck_ref[:, cols].astype(BF16))])
    stats = [_softmax_parts(sc, exp=jnp.exp2) for sc in scores]
    for cols, ((p_loc, p_ctx), l) in zip(heads, stats):
        p_loc = p_loc.astype(BF16)
        o_loc = [_dot(p_loc[r * GRID_W:(r + 1) * GRID_W], v_ref[pl.ds(starts[r], span), cols]) for r in range(NA_QROWS)]
        o = jnp.concatenate(o_loc, axis=0) + _dot(p_ctx.astype(BF16), cv_ref[:, cols].astype(BF16))
        o_ref[:, cols] = (o / l).astype(o_ref.dtype)


def na_latent(z, cache_k, cache_v, layer, bias, batch, seq):
    rows = seq // GRID_W
    assert rows % NA_QROWS == 0 and rows >= NA_ROWS and H_A % NA_HEADS == 0
    nrb = rows // NA_QROWS
    tq = NA_QROWS * GRID_W
    hw = NA_HEADS * HEAD_DIM
    n_ctx = cache_k.shape[2]
    ck = cache_k.reshape(batch, DEPTH, n_ctx, MIX_A)
    cv = cache_v.reshape(batch, DEPTH, n_ctx, MIX_A)
    ctx_spec = pl.BlockSpec((None, None, n_ctx, hw), lambda b, h, rb: (b, layer, 0, h))
    return pl.pallas_call(
        functools.partial(_na_latent_kernel, rows=rows),
        grid=(batch, H_A // NA_HEADS, nrb),
        in_specs=[
            pl.BlockSpec((tq, hw), lambda b, h, rb: (b * nrb + rb, OFF_QA // hw + h)),
            pl.BlockSpec((seq, hw), lambda b, h, rb: (b, OFF_KA // hw + h)),
            pl.BlockSpec((seq, hw), lambda b, h, rb: (b, OFF_VA // hw + h)),
            ctx_spec, ctx_spec,
            pl.BlockSpec((NA_HEADS, NA_ROWS, GRID_W, NA_ROWS * GRID_W), lambda b, h, rb: (h, 0, 0, 0)),
        ],
        out_specs=pl.BlockSpec((tq, hw), lambda b, h, rb: (b * nrb + rb, h)),
        out_shape=jax.ShapeDtypeStruct((batch * seq, MIX_A), BF16),
        compiler_params=_params("arbitrary", "arbitrary", "arbitrary"),
        name="na_latent",
    )(z, z, z, ck, cv, bias)


def _diff_latent_kernel(lam_ref, gain_ref, q_ref, k_ref, v_ref, ck_ref, cv_ref, o_ref, *, layer):
    lam, lam_init = _diff_lambda(lam_ref, layer)
    k = k_ref[...]
    ck = ck_ref[...].astype(BF16)
    cv = cv_ref[...].astype(BF16)
    nsub = q_ref.shape[0] // DIFF_TQ
    rows = [slice(u * DIFF_TQ, (u + 1) * DIFF_TQ) for u in range(nsub)]
    scores = [[[_dot_nt(qh, k), _dot_nt(qh, ck)] for qh in _split_halves(q_ref[r, :])] for r in rows]
    stats = [[_softmax_parts(sc, exp=jnp.exp2) for sc in su] for su in scores]
    for r, ((p0, l0), (p1, l1)) in zip(rows, stats):
        c = lam * l0 / l1
        wl = (p0[0] - p1[0] * c).astype(BF16)
        wc = (p0[1] - p1[1] * c).astype(BF16)
        o = (_dot(wl, v_ref[...]) + _dot(wc, cv)) / l0
        o_ref[r, :] = _diff_finish(o, gain_ref, lam_init).astype(o_ref.dtype)


def diff_latent(z, cache_k, cache_v, layer, lam, gain, batch, seq):
    tq = DIFF_TQ * DIFF_NSUB
    n_ctx = cache_k.shape[2]
    ck = cache_k.reshape(batch, DEPTH, n_ctx, MIX_B)
    cv = cache_v.reshape(batch, DEPTH, n_ctx, MIX_B)
    nqt = seq // tq
    ctx_spec = pl.BlockSpec((None, None, n_ctx, HEAD_DIM), lambda b, h, t: (b, layer, 0, h))
    return pl.pallas_call(
        functools.partial(_diff_latent_kernel, layer=layer),
        grid=(batch, H_B, nqt),
        in_specs=[
            pl.BlockSpec((4, DB_HALF), lambda b, h, t: (0, 0)),
            pl.BlockSpec((1, HEAD_DIM), lambda b, h, t: (0, 0)),
            pl.BlockSpec((tq, HEAD_DIM), lambda b, h, t: (b * nqt + t, OFF_QB // HEAD_DIM + h)),
            pl.BlockSpec((seq, HEAD_DIM), lambda b, h, t: (b, OFF_KB // HEAD_DIM + h)),
            pl.BlockSpec((seq, HEAD_DIM), lambda b, h, t: (b, OFF_VB // HEAD_DIM + h)),
            ctx_spec, ctx_spec,
        ],
        out_specs=pl.BlockSpec((tq, HEAD_DIM), lambda b, h, t: (b * nqt + t, h)),
        out_shape=jax.ShapeDtypeStruct((batch * seq, MIX_B), BF16),
        compiler_params=_params("arbitrary", "arbitrary", "arbitrary"),
        name="diff_latent",
    )(lam, gain.reshape(1, HEAD_DIM), z, z, z, ck, cv)


def _window_mask_table():
    i, j = np.arange(SW_SUB)[:, None], np.arange(SW_SPAN)[None, :]
    tabs = [np.where(np.abs(c * WINDOW_C + i - j) <= WINDOW_C, 0.0, NEG_INF) for c in range(3)]
    return jnp.asarray(np.stack(tabs), F32)


def _window_latent_kernel(sink_ref, mask_ref, q_ref, k_ref, v_ref, ck_ref, cv_ref, o_ref, *, seq):
    n = pl.program_id(1)
    t0 = pl.program_id(2) * SW_TQ
    ck = ck_ref[...].astype(BF16)
    cv = cv_ref[...].astype(BF16)
    sink = _sink_column(sink_ref, n, SW_SUB) * LOG2E
    nsub = SW_TQ // SW_SUB
    rows = [slice(u * SW_SUB, (u + 1) * SW_SUB) for u in range(nsub)]
    starts, scores = [], []
    for u in range(nsub):
        tu = t0 + u * SW_SUB
        ks = pl.multiple_of(jnp.clip(tu - WINDOW_C, 0, seq - SW_SPAN), WINDOW_C)
        case = (tu - ks) // WINDOW_C
        q = jnp.concatenate([q_ref[rows[u], g * HEAD_DIM:(g + 1) * HEAD_DIM] for g in range(G_C)], axis=0)
        s_loc = _dot_nt(q, k_ref[pl.ds(ks, SW_SPAN), :]).reshape(G_C, SW_SUB, SW_SPAN) + mask_ref[case][None]
        starts.append(ks)
        scores.append([s_loc.reshape(G_C * SW_SUB, SW_SPAN), _dot_nt(q, ck)])
    stats = [_softmax_parts(sc, extra=sink, exp=jnp.exp2) for sc in scores]
    for u, ((p_loc, p_ctx), l) in enumerate(stats):
        o = (_dot(p_loc.astype(BF16), v_ref[pl.ds(starts[u], SW_SPAN), :]) + _dot(p_ctx.astype(BF16), cv)) / l
        for g in range(G_C):
            o_ref[rows[u], g * HEAD_DIM:(g + 1) * HEAD_DIM] = o[g * SW_SUB:(g + 1) * SW_SUB].astype(o_ref.dtype)


def window_latent(z, cache_k, cache_v, layer, sink, batch, seq):
    assert seq % SW_TQ == 0 and seq >= SW_SPAN and SW_TQ % SW_SUB == 0
    n_ctx = cache_k.shape[2]
    ck = cache_k.reshape(batch, DEPTH, n_ctx, KVW_C)
    cv = cache_v.reshape(batch, DEPTH, n_ctx, KVW_C)
    nqt = seq // SW_TQ
    gw = G_C * HEAD_DIM
    ctx_spec = pl.BlockSpec((None, None, n_ctx, HEAD_DIM), lambda b, n, t: (b, layer, 0, n))
    return pl.pallas_call(
        functools.partial(_window_latent_kernel, seq=seq),
        grid=(batch, KV_C, nqt),
        in_specs=[
            pl.BlockSpec(memory_space=pltpu.SMEM),
            pl.BlockSpec((3, SW_SUB, SW_SPAN), lambda b, n, t: (0, 0, 0)),
            pl.BlockSpec((SW_TQ, gw), lambda b, n, t: (b * nqt + t, OFF_QC // gw + n)),
            pl.BlockSpec((seq, HEAD_DIM), lambda b, n, t: (b, OFF_KC // HEAD_DIM + n)),
            pl.BlockSpec((seq, HEAD_DIM), lambda b, n, t: (b, OFF_VC // HEAD_DIM + n)),
            ctx_spec, ctx_spec,
        ],
        out_specs=pl.BlockSpec((SW_TQ, gw), lambda b, n, t: (b * nqt + t, n)),
        out_shape=jax.ShapeDtypeStruct((batch * seq, MIX_C), BF16),
        compiler_params=_params("arbitrary", "arbitrary", "arbitrary"),
        name="window_latent",
    )(sink, _window_mask_table(), z, z, z, ck, cv)


def _prompt_row(tm, i):
    return 0


def kernel(x_prompt, x_sample, cache_a_k, cache_a_v, cache_b_k, cache_b_v, cache_c_k, cache_c_v, c, c_ctx,
           w_ada, b_ada, w_in, rpb_a, lam_b, subln_b, sink_c, w_branch, w_out, ln_g, ln_b, w_gate_up, w_down):
    bp, tp, _ = x_prompt.shape
    bs, ts, _ = x_sample.shape

    def sample_row(tm, i):
        return 1 + (i * tm) // ts

    streams = {
        "p": dict(x=x_prompt.reshape(bp * tp, D_MODEL), rowfn=_prompt_row),
        "s": dict(x=x_sample.reshape(bs * ts, D_MODEL), rowfn=sample_row),
    }
    cc = jnp.concatenate([c_ctx[None, :], c], axis=0)
    cc = jnp.pad(cc, ((0, (-cc.shape[0]) % 8), (0, 0)))
    mod = ada_modulation_all(cc, w_ada, b_ada)

    for st in streams.values():
        st["h"] = modulate_cast(st["x"], mod, 0, 0, st["rowfn"])

    new_kv = [[] for _ in range(6)]
    for l in range(DEPTH):
        w_qkv = cast_weight(w_in, l, 0, QKV_W)
        w_g = cast_weight(w_in, l, QKV_W)
        w_br = cast_weight(w_branch, l)
        w_o = cast_weight(w_out, l)
        w_gu = cast_weight(w_gate_up, l)
        w_dn = cast_weight(w_down, l)
        na_bias = _na_bias_table(rpb_a[l])

        sp = streams["p"]
        zp = matmul(sp["h"], w_qkv, F32, 1024, 1024, name="qkv_context")
        for lst, off, wdt, shp in (
                (new_kv[0], OFF_KA, MIX_A, (H_A, HEAD_DIM)), (new_kv[1], OFF_VA, MIX_A, (H_A, HEAD_DIM)),
                (new_kv[2], OFF_KB, MIX_B, (H_B, 2, DB_HALF)), (new_kv[3], OFF_VB, MIX_B, (H_B, HEAD_DIM)),
                (new_kv[4], OFF_KC, KVW_C, (KV_C, HEAD_DIM)), (new_kv[5], OFF_VC, KVW_C, (KV_C, HEAD_DIM))):
            lst.append(zp[:, off:off + wdt].reshape((bp, tp) + shp))
        op = context_attention(zp, tp, lam_b[l], subln_b[l], sink_c[l], l)
        sp["o"] = ((op, 0), (op, MIX_A // MIX_B), (op, (MIX_A + MIX_B) // MIX_C))

        ss = streams["s"]
        zs = qkv_latent(ss["h"], w_qkv, ts)
        oa = na_latent(zs, cache_a_k, cache_a_v, l, na_bias, bs, ts)
        ob = diff_latent(zs, cache_b_k, cache_b_v, l, lam_b[l], subln_b[l], bs, ts)
        oc = window_latent(zs, cache_c_k, cache_c_v, l, sink_c[l], bs, ts)
        ss["o"] = ((oa, 0), (ob, 0), (oc, 0))

        for st in streams.values():
            gates = matmul(st["h"], w_g, BF16, 1024, 1024, sigmoid=True, name="branch_gates")
            u = merge_branches(st["o"], gates, w_br)
            v = matmul_residual(u, w_o, st["x"], mod, l, 0, st["rowfn"], 1024, 1024, "mixer_out")
            st["x"], h2 = post_norm(v, mod, ln_g[l, 0], ln_b[l, 0], st["rowfn"], next_mod=(l, 1))
            a = swiglu_up(h2, w_gu)
            v = matmul_residual(a, w_dn, st["x"], mod, l, 1, st["rowfn"], 512, 512, "ffn_down")
            nxt = (l + 1, 0) if l + 1 < DEPTH else None
            st["x"], st["h"] = post_norm(v, mod, ln_g[l, 1], ln_b[l, 1], st["rowfn"], next_mod=nxt)

    outs = [jnp.stack(lst, axis=1) for lst in new_kv]
    return (streams["p"]["x"].reshape(bp, tp, D_MODEL), streams["s"]["x"].reshape(bs, ts, D_MODEL), *outs)
```

```python
import functools
import math

import numpy as np
import jax
import jax.numpy as jnp
from jax import lax
from jax.experimental import pallas as pl
from jax.experimental.pallas import tpu as pltpu

D_MODEL = 4096
DEPTH = 2
GRID_W = 64
HEAD_DIM = 128
H_A = 8
MIX_A = H_A * HEAD_DIM
NA_ROWS = 8
NA_COLS = 16
H_B = 8
DB_HALF = HEAD_DIM // 2
MIX_B = H_B * HEAD_DIM
H_C = 16
KV_C = 4
G_C = H_C // KV_C
MIX_C = H_C * HEAD_DIM
KVW_C = KV_C * HEAD_DIM
WINDOW_C = 128
N_BRANCH = 3
QKV_W = 3 * MIX_A + 3 * MIX_B + MIX_C + 2 * KVW_C
D_FF = 11008
ROPE_BASE = 10000.0
LN_EPS = 1e-5
RMS_EPS = 1e-5
ALPHA = (2 * DEPTH) ** 0.25
NEG_INF = -1e30
LOG2E = math.log2(math.e)
BF16 = jnp.bfloat16
F32 = jnp.float32

OFF_QA, OFF_KA, OFF_VA = 0, MIX_A, 2 * MIX_A
OFF_QB, OFF_KB, OFF_VB = 3 * MIX_A, 3 * MIX_A + MIX_B, 3 * MIX_A + 2 * MIX_B
OFF_QC = 3 * MIX_A + 3 * MIX_B
OFF_KC = OFF_QC + MIX_C
OFF_VC = OFF_KC + KVW_C

V7X_VMEM_LIMIT = 56 * 1024 * 1024
V7X_MXU_COLS = 256

PROJ_TILE = 1024
SWIGLU_TM, SWIGLU_TN = 2048, V7X_MXU_COLS
DOWN_TILE = 512
MODULATE_TM, NORM_TM = 512, 256

NA_QROWS = 8
NA_HEADS = 4
SW_TQ = 1024
SW_SUB = WINDOW_C
SW_SPAN = 3 * SW_SUB
DIFF_TQ = 128
DIFF_NSUB = 8


def _params(*sem):
    return pltpu.CompilerParams(dimension_semantics=sem, vmem_limit_bytes=V7X_VMEM_LIMIT)


def _dot(a, b):
    return jnp.dot(a, b, preferred_element_type=F32)


def _dot_nt(a, b):
    return lax.dot_general(a, b, (((1,), (1,)), ((), ())), preferred_element_type=F32)


def _cast_kernel(w_ref, o_ref):
    o_ref[...] = w_ref[...].astype(o_ref.dtype)


def cast_weight(w, layer, col0=0, ncols=None):
    _, r, c = w.shape
    ncols = c - col0 if ncols is None else ncols
    tc = 512 if r <= D_MODEL else HEAD_DIM
    assert col0 % tc == 0 and ncols % tc == 0
    return pl.pallas_call(
        _cast_kernel,
        grid=(ncols // tc,),
        in_specs=[pl.BlockSpec((None, r, tc), lambda j: (layer, 0, col0 // tc + j))],
        out_specs=pl.BlockSpec((r, tc), lambda j: (0, j)),
        out_shape=jax.ShapeDtypeStruct((r, ncols), BF16),
        compiler_params=_params("arbitrary"),
        name="cast_weight",
    )(w)


def _ada_kernel(c_ref, w_ref, b_ref, o_ref):
    cv = c_ref[...]
    s = (cv * jax.nn.sigmoid(cv)).astype(BF16)
    o_ref[...] = _dot(s, w_ref[...].astype(BF16)) + b_ref[...]


def ada_modulation_all(cc, w_ada, b_ada):
    r = cc.shape[0]
    n = 6 * D_MODEL
    tn = 512
    out = pl.pallas_call(
        _ada_kernel,
        grid=(DEPTH, n // tn),
        in_specs=[
            pl.BlockSpec((r, D_MODEL), lambda l, j: (0, 0)),
            pl.BlockSpec((None, D_MODEL, tn), lambda l, j: (l, 0, j)),
            pl.BlockSpec((None, 1, tn), lambda l, j: (l, 0, j)),
        ],
        out_specs=pl.BlockSpec((None, r, tn), lambda l, j: (l, 0, j)),
        out_shape=jax.ShapeDtypeStruct((DEPTH, r, n), F32),
        compiler_params=_params("arbitrary", "arbitrary"),
        name="ada_modulation",
    )(cc, w_ada, b_ada.reshape(DEPTH, 1, n))
    return out.reshape(DEPTH, r, 6, 1, D_MODEL)


def _mod_spec(layer, chunk, rowfn):
    return pl.BlockSpec((None, None, None, 1, D_MODEL), lambda i: (layer, rowfn(i), chunk, 0, 0))


def _modulate_kernel(x_ref, sc_ref, sh_ref, o_ref):
    o_ref[...] = (x_ref[...] * (1.0 + sc_ref[...]) + sh_ref[...]).astype(BF16)


def modulate_cast(x, mod, layer, idx, rowfn_of_tile, tm=MODULATE_TM):
    m = x.shape[0]
    rowfn = functools.partial(rowfn_of_tile, tm)
    return pl.pallas_call(
        _modulate_kernel,
        grid=(m // tm,),
        in_specs=[
            pl.BlockSpec((tm, D_MODEL), lambda i: (i, 0)),
            _mod_spec(layer, 3 * idx + 1, rowfn),
            _mod_spec(layer, 3 * idx, rowfn),
        ],
        out_specs=pl.BlockSpec((tm, D_MODEL), lambda i: (i, 0)),
        out_shape=jax.ShapeDtypeStruct((m, D_MODEL), BF16),
        compiler_params=_params("arbitrary"),
        name="modulate",
    )(x, mod, mod)


def _ln_core(v_ref, g_ref, b_ref):
    v = v_ref[...]
    mu = jnp.mean(v, axis=-1, keepdims=True)
    d = v - mu
    var = jnp.mean(d * d, axis=-1, keepdims=True)
    return d * lax.rsqrt(var + LN_EPS) * g_ref[...] + b_ref[...]


def _ln_mod_kernel(v_ref, g_ref, b_ref, sc_ref, sh_ref, xo_ref, ho_ref):
    xn = _ln_core(v_ref, g_ref, b_ref)
    xo_ref[...] = xn
    ho_ref[...] = (xn * (1.0 + sc_ref[...]) + sh_ref[...]).astype(BF16)


def _ln_kernel(v_ref, g_ref, b_ref, xo_ref):
    xo_ref[...] = _ln_core(v_ref, g_ref, b_ref)


def post_norm(v, mod, ln_g, ln_b, rowfn_of_tile, next_mod=None, tm=NORM_TM):
    m = v.shape[0]
    rowfn = functools.partial(rowfn_of_tile, tm)
    row_spec = pl.BlockSpec((tm, D_MODEL), lambda i: (i, 0))
    vec_spec = pl.BlockSpec((1, D_MODEL), lambda i: (0, 0))
    in_specs = [row_spec, vec_spec, vec_spec]
    args = [v, ln_g.reshape(1, D_MODEL), ln_b.reshape(1, D_MODEL)]
    if next_mod is None:
        return pl.pallas_call(
            _ln_kernel, grid=(m // tm,), in_specs=in_specs, out_specs=row_spec,
            out_shape=jax.ShapeDtypeStruct((m, D_MODEL), F32),
            compiler_params=_params("arbitrary"), name="post_norm",
        )(*args), None
    nl, nidx = next_mod
    in_specs += [_mod_spec(nl, 3 * nidx + 1, rowfn), _mod_spec(nl, 3 * nidx, rowfn)]
    args += [mod, mod]
    return pl.pallas_call(
        _ln_mod_kernel, grid=(m // tm,), in_specs=in_specs, out_specs=[row_spec, row_spec],
        out_shape=[jax.ShapeDtypeStruct((m, D_MODEL), F32), jax.ShapeDtypeStruct((m, D_MODEL), BF16)],
        compiler_params=_params("arbitrary"), name="post_norm_modulate",
    )(*args)


def _mm_kernel(x_ref, w_ref, o_ref, *, sigmoid):
    acc = _dot(x_ref[...], w_ref[...])
    if sigmoid:
        acc = jax.nn.sigmoid(acc)
    o_ref[...] = acc.astype(o_ref.dtype)


def matmul(x, w, out_dtype, tm, tn, sigmoid=False, name="matmul"):
    m, k = x.shape
    n = w.shape[1]
    return pl.pallas_call(
        functools.partial(_mm_kernel, sigmoid=sigmoid),
        grid=(m // tm, n // tn),
        in_specs=[pl.BlockSpec((tm, k), lambda i, j: (i, 0)), pl.BlockSpec((k, tn), lambda i, j: (0, j))],
        out_specs=pl.BlockSpec((tm, tn), lambda i, j: (i, j)),
        out_shape=jax.ShapeDtypeStruct((m, n), out_dtype),
        compiler_params=_params("arbitrary", "arbitrary"),
        name=name,
    )(x, w)


def _mm_residual_kernel(x_ref, w_ref, res_ref, gate_ref, o_ref):
    o_ref[...] = ALPHA * res_ref[...] + gate_ref[...] * _dot(x_ref[...], w_ref[...])


def matmul_residual(x, w, res, mod, layer, idx, rowfn_of_tile, tm, tn, name):
    m, k = x.shape
    n = w.shape[1]
    rowfn = functools.partial(rowfn_of_tile, tm)
    gate_spec = pl.BlockSpec((None, None, None, 1, tn), lambda i, j: (layer, rowfn(i), 3 * idx + 2, 0, j))
    tile_spec = pl.BlockSpec((tm, tn), lambda i, j: (i, j))
    return pl.pallas_call(
        _mm_residual_kernel,
        grid=(m // tm, n // tn),
        in_specs=[pl.BlockSpec((tm, k), lambda i, j: (i, 0)), pl.BlockSpec((k, tn), lambda i, j: (0, j)),
                  tile_spec, gate_spec],
        out_specs=tile_spec,
        out_shape=jax.ShapeDtypeStruct((m, n), F32),
        compiler_params=_params("arbitrary", "arbitrary"),
        name=name,
    )(x, w, res, mod)


def _rope_tables(t, width):
    half, quarter = width // 2, width // 4
    lane = np.arange(HEAD_DIM)
    e = lane % width
    pos = np.arange(t)
    p = np.where((e < half)[None, :], (pos // GRID_W)[:, None], (pos % GRID_W)[:, None]).astype(np.float64)
    inv = np.exp(-math.log(ROPE_BASE) * (e % quarter).astype(np.float64) / quarter)
    ang = (p.astype(np.float32) * inv.astype(np.float32)[None, :]).astype(np.float64)
    first = ((e % half) < quarter)[None, :]
    cos, sin = np.cos(ang), np.sin(ang)
    return jnp.asarray(cos, F32), jnp.asarray(np.where(first, -sin, sin), F32)


def _rope(x, cos, ssin, quarter):
    partner = lax.broadcasted_iota(jnp.int32, x.shape, 1) ^ quarter
    return x * cos + jnp.take_along_axis(x, partner, axis=1) * ssin


def _qkv_latent_kernel(x_ref, w_ref, cb_ref, sb_ref, cc_ref, sc_ref, o_ref, *, tn):
    j = pl.program_id(1)
    acc = _dot(x_ref[...], w_ref[...])
    nchunk = tn // HEAD_DIM
    tiles_qb = (OFF_QB // tn,)
    tiles_kb = (OFF_KB // tn,)
    tiles_qc = tuple(range(OFF_QC // tn, OFF_KC // tn))
    tile_kcvc = OFF_KC // tn

    def is_in(tiles):
        c = j == tiles[0]
        for tt in tiles[1:]:
            c = jnp.logical_or(c, j == tt)
        return c

    def store(fn_of_chunk):
        for ch in range(nchunk):
            sl = slice(ch * HEAD_DIM, (ch + 1) * HEAD_DIM)
            o_ref[:, sl] = fn_of_chunk(ch, acc[:, sl]).astype(o_ref.dtype)

    rope_b = lambda x: _rope(x, cb_ref[...], sb_ref[...], DB_HALF // 4)
    rope_c = lambda x: _rope(x, cc_ref[...], sc_ref[...], HEAD_DIM // 4)
    tiles_qa = tuple(range(OFF_QA // tn, OFF_KA // tn))
    special = is_in(tiles_qa + tiles_qb + tiles_kb + tiles_qc + (tile_kcvc,))

    @pl.when(jnp.logical_not(special))
    def _():
        o_ref[...] = acc.astype(o_ref.dtype)

    @pl.when(is_in(tiles_qa))
    def _():
        o_ref[...] = (acc * (HEAD_DIM ** -0.5 * LOG2E)).astype(o_ref.dtype)

    @pl.when(is_in(tiles_qb))
    def _():
        store(lambda ch, x: rope_b(x) * (DB_HALF ** -0.5 * LOG2E))

    @pl.when(is_in(tiles_kb))
    def _():
        store(lambda ch, x: rope_b(x))

    @pl.when(is_in(tiles_qc))
    def _():
        store(lambda ch, x: rope_c(x) * (HEAD_DIM ** -0.5 * LOG2E))

    @pl.when(j == tile_kcvc)
    def _():
        store(lambda ch, x: rope_c(x) if ch < KVW_C // HEAD_DIM else x)


def qkv_latent(h, w_qkv, seq, tm=PROJ_TILE, tn=PROJ_TILE):
    m, k = h.shape
    assert tn == MIX_A and OFF_KC % tn == 0 and (OFF_VC - OFF_KC) == KVW_C and seq % tm == 0
    tabs = _rope_tables(seq, DB_HALF) + _rope_tables(seq, HEAD_DIM)
    per_seq = seq // tm
    tab_spec = pl.BlockSpec((tm, HEAD_DIM), lambda i, j: (i % per_seq, 0))
    return pl.pallas_call(
        functools.partial(_qkv_latent_kernel, tn=tn),
        grid=(m // tm, QKV_W // tn),
        in_specs=[pl.BlockSpec((tm, k), lambda i, j: (i, 0)), pl.BlockSpec((k, tn), lambda i, j: (0, j))]
        + [tab_spec] * len(tabs),
        out_specs=pl.BlockSpec((tm, tn), lambda i, j: (i, j)),
        out_shape=jax.ShapeDtypeStruct((m, QKV_W), BF16),
        compiler_params=_params("arbitrary", "arbitrary"),
        name="qkv_latent",
    )(h, w_qkv, *tabs)


def _merge_kernel(oa_ref, ob_ref, oc_ref, wa_ref, wb_ref, wc_ref, ga_ref, gb_ref, gc_ref, o_ref):
    y = ga_ref[...].astype(F32) * _dot(oa_ref[...], wa_ref[...])
    y += gb_ref[...].astype(F32) * _dot(ob_ref[...], wb_ref[...])
    y += gc_ref[...].astype(F32) * _dot(oc_ref[...], wc_ref[...])
    o_ref[...] = y.astype(o_ref.dtype)


def merge_branches(o_parts, gates, w_branch, tm=PROJ_TILE, tn=PROJ_TILE):
    (oa, ja), (ob, jb), (oc, jc) = o_parts
    m = gates.shape[0]
    nj = D_MODEL // tn
    return pl.pallas_call(
        _merge_kernel,
        grid=(m // tm, nj),
        in_specs=[
            pl.BlockSpec((tm, MIX_A), lambda i, j: (i, ja)),
            pl.BlockSpec((tm, MIX_B), lambda i, j: (i, jb)),
            pl.BlockSpec((tm, MIX_C), lambda i, j: (i, jc)),
            pl.BlockSpec((MIX_A, tn), lambda i, j: (0, j)),
            pl.BlockSpec((MIX_B, tn), lambda i, j: (MIX_A // MIX_B, j)),
            pl.BlockSpec((MIX_C, tn), lambda i, j: ((MIX_A + MIX_B) // MIX_C, j)),
            pl.BlockSpec((tm, tn), lambda i, j: (i, j)),
            pl.BlockSpec((tm, tn), lambda i, j: (i, nj + j)),
            pl.BlockSpec((tm, tn), lambda i, j: (i, 2 * nj + j)),
        ],
        out_specs=pl.BlockSpec((tm, tn), lambda i, j: (i, j)),
        out_shape=jax.ShapeDtypeStruct((m, D_MODEL), BF16),
        compiler_params=_params("arbitrary", "arbitrary"),
        name="merge_branches",
    )(oa, ob, oc, w_branch, w_branch, w_branch, gates, gates, gates)


def _swiglu_kernel(x_ref, wg_ref, wu_ref, o_ref):
    x = x_ref[...]
    g = _dot(x, wg_ref[...])
    u = _dot(x, wu_ref[...])
    o_ref[...] = (g * jax.nn.sigmoid(g) * u).astype(o_ref.dtype)


def swiglu_up(h, w_gate_up, tm=SWIGLU_TM, tn=SWIGLU_TN):
    m, k = h.shape
    nj = D_FF // tn
    return pl.pallas_call(
        _swiglu_kernel,
        grid=(m // tm, nj),
        in_specs=[
            pl.BlockSpec((tm, k), lambda i, j: (i, 0)),
            pl.BlockSpec((k, tn), lambda i, j: (0, j)),
            pl.BlockSpec((k, tn), lambda i, j: (0, nj + j)),
        ],
        out_specs=pl.BlockSpec((tm, tn), lambda i, j: (i, j)),
        out_shape=jax.ShapeDtypeStruct((m, D_FF), BF16),
        compiler_params=_params("arbitrary", "arbitrary"),
        name="swiglu_up",
    )(h, w_gate_up, w_gate_up)


def _diff_lambda(lam_ref, layer):
    lam_init = 0.8 - 0.6 * math.exp(-0.3 * layer)
    lf = lam_ref[...]
    a = jnp.sum(lf[0:1, :] * lf[1:2, :], axis=-1, keepdims=True)
    b = jnp.sum(lf[2:3, :] * lf[3:4, :], axis=-1, keepdims=True)
    return jnp.exp(a) - jnp.exp(b) + lam_init, lam_init


def _diff_finish(o, gain_ref, lam_init):
    o = o * lax.rsqrt(jnp.mean(o * o, axis=-1, keepdims=True) + RMS_EPS)
    return o * gain_ref[...] * (1.0 - lam_init)


def _split_halves(q):
    lane = lax.broadcasted_iota(jnp.int32, (1, HEAD_DIM), 1)
    zero = jnp.zeros_like(q)
    return jnp.where(lane < DB_HALF, q, zero), jnp.where(lane >= DB_HALF, q, zero)


def _sink_column(sink_ref, n, t):
    row = lax.broadcasted_iota(jnp.int32, (G_C * t, 1), 0)
    col = jnp.full((G_C * t, 1), sink_ref[n * G_C + G_C - 1], F32)
    for g in range(G_C - 2, -1, -1):
        col = jnp.where(row < (g + 1) * t, sink_ref[n * G_C + g], col)
    return col


def _softmax_parts(parts, extra=None, exp=jnp.exp):
    m = functools.reduce(jnp.maximum, [jnp.max(s, axis=-1, keepdims=True) for s in parts])
    if extra is not None:
        m = jnp.maximum(m, extra)
    ps = [exp(s - m) for s in parts]
    l = functools.reduce(jnp.add, [jnp.sum(p, axis=-1, keepdims=True) for p in ps])
    if extra is not None:
        l = l + exp(extra - m)
    return ps, l


def _context_attn_kernel(sink_ref, z_ref, lam_ref, gain_ref, o_ref, *, layer):
    scale = HEAD_DIM ** -0.5

    def col(off, h, width=HEAD_DIM):
        return z_ref[:, off + h * width: off + (h + 1) * width]

    for h in range(H_A):
        q, k, v = col(OFF_QA, h).astype(BF16), col(OFF_KA, h).astype(BF16), col(OFF_VA, h).astype(BF16)
        (p,), l = _softmax_parts([_dot_nt(q, k) * scale])
        o = _dot((p / l).astype(BF16), v)
        o_ref[:, h * HEAD_DIM:(h + 1) * HEAD_DIM] = o.astype(o_ref.dtype)

    lam, lam_init = _diff_lambda(lam_ref, layer)
    for h in range(H_B):
        q, k, v = col(OFF_QB, h).astype(BF16), col(OFF_KB, h).astype(BF16), col(OFF_VB, h).astype(BF16)
        q1, q2 = _split_halves(q)
        (p0,), l0 = _softmax_parts([_dot_nt(q1, k) * (DB_HALF ** -0.5)])
        (p1,), l1 = _softmax_parts([_dot_nt(q2, k) * (DB_HALF ** -0.5)])
        w = p0 / l0 - lam * (p1 / l1)
        o = _diff_finish(_dot(w.astype(BF16), v), gain_ref, lam_init)
        o_ref[:, MIX_A + h * HEAD_DIM: MIX_A + (h + 1) * HEAD_DIM] = o.astype(o_ref.dtype)

    t = z_ref.shape[0]
    for n in range(KV_C):
        q = jnp.concatenate([col(OFF_QC, n * G_C + g) for g in range(G_C)], axis=0).astype(BF16)
        k, v = col(OFF_KC, n).astype(BF16), col(OFF_VC, n).astype(BF16)
        sink = _sink_column(sink_ref, n, t)
        (p,), l = _softmax_parts([_dot_nt(q, k) * scale], extra=sink)
        o = _dot((p / l).astype(BF16), v)
        for g in range(G_C):
            c0 = MIX_A + MIX_B + (n * G_C + g) * HEAD_DIM
            o_ref[:, c0:c0 + HEAD_DIM] = o[g * t:(g + 1) * t].astype(o_ref.dtype)


def context_attention(z, seq, lam, gain, sink, layer):
    m = z.shape[0]
    return pl.pallas_call(
        functools.partial(_context_attn_kernel, layer=layer),
        grid=(m // seq,),
        in_specs=[
            pl.BlockSpec(memory_space=pltpu.SMEM),
            pl.BlockSpec((seq, QKV_W), lambda b: (b, 0)),
            pl.BlockSpec((4, DB_HALF), lambda b: (0, 0)),
            pl.BlockSpec((1, HEAD_DIM), lambda b: (0, 0)),
        ],
        out_specs=pl.BlockSpec((seq, D_MODEL), lambda b: (b, 0)),
        out_shape=jax.ShapeDtypeStruct((m, D_MODEL), BF16),
        compiler_params=_params("arbitrary"),
        name="context_attention",
    )(sink, z, lam, gain.reshape(1, HEAD_DIM))


def _na_bias_table(rpb_l):
    pad = GRID_W
    rp = jnp.pad(rpb_l.astype(F32) * LOG2E, ((0, 0), (0, 0), (pad, pad)))
    first = pad + NA_COLS - 1
    tz = jnp.stack([rp[:, :, first - qc:first - qc + GRID_W] for qc in range(GRID_W)], axis=2)
    qc, kc = np.arange(GRID_W)[:, None], np.arange(GRID_W)[None, :]
    ws = np.clip(qc - NA_COLS // 2, 0, GRID_W - NA_COLS)
    valid = (kc >= ws) & (kc < ws + NA_COLS)
    tz = jnp.where(jnp.asarray(valid)[None, None], tz, NEG_INF)
    return jnp.stack([jnp.concatenate([tz[:, e0 + j] for j in range(NA_ROWS)], axis=-1)
                      for e0 in range(NA_ROWS)], axis=1)


def _na_latent_kernel(q_ref, k_ref, v_ref, ck_ref, cv_ref, bias_ref, o_ref, *, rows):
    rb = pl.program_id(2)
    span = NA_ROWS * GRID_W
    starts, biases = [], []
    for r in range(NA_QROWS):
        r_abs = rb * NA_QROWS + r
        rs = jnp.clip(r_abs - NA_ROWS // 2, 0, rows - NA_ROWS)
        starts.append(pl.multiple_of(rs * GRID_W, GRID_W))
        biases.append(rs - r_abs + NA_ROWS - 1)
    heads = [slice(hh * HEAD_DIM, (hh + 1) * HEAD_DIM) for hh in range(NA_HEADS)]
    scores = []
    for hh, cols in enumerate(heads):
        q = q_ref[:, cols]
        s_loc = [_dot_nt(q[r * GRID_W:(r + 1) * GRID_W], k_ref[pl.ds(starts[r], span), cols]) + bias_ref[hh, biases[r]]
                 for r in range(NA_QROWS)]
        scores.append([jnp.concatenate(s_loc, axis=0), _dot_nt(q, ck_ref[:, cols].astype(BF16))])
    stats = [_softmax_parts(sc, exp=jnp.exp2) for sc in scores]
    for cols, ((p_loc, p_ctx), l) in zip(heads, stats):
        p_loc = p_loc.astype(BF16)
        o_loc = [_dot(p_loc[r * GRID_W:(r + 1) * GRID_W], v_ref[pl.ds(starts[r], span), cols]) for r in range(NA_QROWS)]
        o = jnp.concatenate(o_loc, axis=0) + _dot(p_ctx.astype(BF16), cv_ref[:, cols].astype(BF16))
        o_ref[:, cols] = (o / l).astype(o_ref.dtype)


def na_latent(z, cache_k, cache_v, layer, bias, batch, seq):
    rows = seq // GRID_W
    assert rows % NA_QROWS == 0 and rows >= NA_ROWS and H_A % NA_HEADS == 0
    nrb = rows // NA_QROWS
    tq = NA_QROWS * GRID_W
    hw = NA_HEADS * HEAD_DIM
    n_ctx = cache_k.shape[2]
    ck = cache_k.reshape(batch, DEPTH, n_ctx, MIX_A)
    cv = cache_v.reshape(batch, DEPTH, n_ctx, MIX_A)
    ctx_spec = pl.BlockSpec((None, None, n_ctx, hw), lambda b, h, rb: (b, layer, 0, h))
    return pl.pallas_call(
        functools.partial(_na_latent_kernel, rows=rows),
        grid=(batch, H_A // NA_HEADS, nrb),
        in_specs=[
            pl.BlockSpec((tq, hw), lambda b, h, rb: (b * nrb + rb, OFF_QA // hw + h)),
            pl.BlockSpec((seq, hw), lambda b, h, rb: (b, OFF_KA // hw + h)),
            pl.BlockSpec((seq, hw), lambda b, h, rb: (b, OFF_VA // hw + h)),
            ctx_spec, ctx_spec,
            pl.BlockSpec((NA_HEADS, NA_ROWS, GRID_W, NA_ROWS * GRID_W), lambda b, h, rb: (h, 0, 0, 0)),
        ],
        out_specs=pl.BlockSpec((tq, hw), lambda b, h, rb: (b * nrb + rb, h)),
        out_shape=jax.ShapeDtypeStruct((batch * seq, MIX_A), BF16),
        compiler_params=_params("arbitrary", "arbitrary", "arbitrary"),
        name="na_latent",
    )(z, z, z, ck, cv, bias)


def _diff_latent_kernel(lam_ref, gain_ref, q_ref, k_ref, v_ref, ck_ref, cv_ref, o_ref, *, layer):
    lam, lam_init = _diff_lambda(lam_ref, layer)
    k = k_ref[...]
    ck = ck_ref[...].astype(BF16)
    cv = cv_ref[...].astype(BF16)
    nsub = q_ref.shape[0] // DIFF_TQ
    rows = [slice(u * DIFF_TQ, (u + 1) * DIFF_TQ) for u in range(nsub)]
    scores = [[[_dot_nt(qh, k), _dot_nt(qh, ck)] for qh in _split_halves(q_ref[r, :])] for r in rows]
    stats = [[_softmax_parts(sc, exp=jnp.exp2) for sc in su] for su in scores]
    for r, ((p0, l0), (p1, l1)) in zip(rows, stats):
        c = lam * l0 / l1
        wl = (p0[0] - p1[0] * c).astype(BF16)
        wc = (p0[1] - p1[1] * c).astype(BF16)
        o = (_dot(wl, v_ref[...]) + _dot(wc, cv)) / l0
        o_ref[r, :] = _diff_finish(o, gain_ref, lam_init).astype(o_ref.dtype)


def diff_latent(z, cache_k, cache_v, layer, lam, gain, batch, seq):
    tq = DIFF_TQ * DIFF_NSUB
    n_ctx = cache_k.shape[2]
    ck = cache_k.reshape(batch, DEPTH, n_ctx, MIX_B)
    cv = cache_v.reshape(batch, DEPTH, n_ctx, MIX_B)
    nqt = seq // tq
    ctx_spec = pl.BlockSpec((None, None, n_ctx, HEAD_DIM), lambda b, h, t: (b, layer, 0, h))
    return pl.pallas_call(
        functools.partial(_diff_latent_kernel, layer=layer),
        grid=(batch, H_B, nqt),
        in_specs=[
            pl.BlockSpec((4, DB_HALF), lambda b, h, t: (0, 0)),
            pl.BlockSpec((1, HEAD_DIM), lambda b, h, t: (0, 0)),
            pl.BlockSpec((tq, HEAD_DIM), lambda b, h, t: (b * nqt + t, OFF_QB // HEAD_DIM + h)),
            pl.BlockSpec((seq, HEAD_DIM), lambda b, h, t: (b, OFF_KB // HEAD_DIM + h)),
            pl.BlockSpec((seq, HEAD_DIM), lambda b, h, t: (b, OFF_VB // HEAD_DIM + h)),
            ctx_spec, ctx_spec,
        ],
        out_specs=pl.BlockSpec((tq, HEAD_DIM), lambda b, h, t: (b * nqt + t, h)),
        out_shape=jax.ShapeDtypeStruct((batch * seq, MIX_B), BF16),
        compiler_params=_params("arbitrary", "arbitrary", "arbitrary"),
        name="diff_latent",
    )(lam, gain.reshape(1, HEAD_DIM), z, z, z, ck, cv)


def _window_mask_table():
    i, j = np.arange(SW_SUB)[:, None], np.arange(SW_SPAN)[None, :]
    tabs = [np.where(np.abs(c * WINDOW_C + i - j) <= WINDOW_C, 0.0, NEG_INF) for c in range(3)]
    return jnp.asarray(np.stack(tabs), F32)


def _window_latent_kernel(sink_ref, mask_ref, q_ref, k_ref, v_ref, ck_ref, cv_ref, o_ref, *, seq):
    n = pl.program_id(1)
    t0 = pl.program_id(2) * SW_TQ
    ck = ck_ref[...].astype(BF16)
    cv = cv_ref[...].astype(BF16)
    sink = _sink_column(sink_ref, n, SW_SUB) * LOG2E
    nsub = SW_TQ // SW_SUB
    rows = [slice(u * SW_SUB, (u + 1) * SW_SUB) for u in range(nsub)]
    starts, scores = [], []
    for u in range(nsub):
        tu = t0 + u * SW_SUB
        ks = pl.multiple_of(jnp.clip(tu - WINDOW_C, 0, seq - SW_SPAN), WINDOW_C)
        case = (tu - ks) // WINDOW_C
        q = jnp.concatenate([q_ref[rows[u], g * HEAD_DIM:(g + 1) * HEAD_DIM] for g in range(G_C)], axis=0)
        s_loc = _dot_nt(q, k_ref[pl.ds(ks, SW_SPAN), :]).reshape(G_C, SW_SUB, SW_SPAN) + mask_ref[case][None]
        starts.append(ks)
        scores.append([s_loc.reshape(G_C * SW_SUB, SW_SPAN), _dot_nt(q, ck)])
    stats = [_softmax_parts(sc, extra=sink, exp=jnp.exp2) for sc in scores]
    for u, ((p_loc, p_ctx), l) in enumerate(stats):
        o = (_dot(p_loc.astype(BF16), v_ref[pl.ds(starts[u], SW_SPAN), :]) + _dot(p_ctx.astype(BF16), cv)) / l
        for g in range(G_C):
            o_ref[rows[u], g * HEAD_DIM:(g + 1) * HEAD_DIM] = o[g * SW_SUB:(g + 1) * SW_SUB].astype(o_ref.dtype)


def window_latent(z, cache_k, cache_v, layer, sink, batch, seq):
    assert seq % SW_TQ == 0 and seq >= SW_SPAN and SW_TQ % SW_SUB == 0
    n_ctx = cache_k.shape[2]
    ck = cache_k.reshape(batch, DEPTH, n_ctx, KVW_C)
    cv = cache_v.reshape(batch, DEPTH, n_ctx, KVW_C)
    nqt = seq // SW_TQ
    gw = G_C * HEAD_DIM
    ctx_spec = pl.BlockSpec((None, None, n_ctx, HEAD_DIM), lambda b, n, t: (b, layer, 0, n))
    return pl.pallas_call(
        functools.partial(_window_latent_kernel, seq=seq),
        grid=(batch, KV_C, nqt),
        in_specs=[
            pl.BlockSpec(memory_space=pltpu.SMEM),
            pl.BlockSpec((3, SW_SUB, SW_SPAN), lambda b, n, t: (0, 0, 0)),
            pl.BlockSpec((SW_TQ, gw), lambda b, n, t: (b * nqt + t, OFF_QC // gw + n)),
            pl.BlockSpec((seq, HEAD_DIM), lambda b, n, t: (b, OFF_KC // HEAD_DIM + n)),
            pl.BlockSpec((seq, HEAD_DIM), lambda b, n, t: (b, OFF_VC // HEAD_DIM + n)),
            ctx_spec, ctx_spec,
        ],
        out_specs=pl.BlockSpec((SW_TQ, gw), lambda b, n, t: (b * nqt + t, n)),
        out_shape=jax.ShapeDtypeStruct((batch * seq, MIX_C), BF16),
        compiler_params=_params("arbitrary", "arbitrary", "arbitrary"),
        name="window_latent",
    )(sink, _window_mask_table(), z, z, z, ck, cv)


def _prompt_row(tm, i):
    return 0


def kernel(x_prompt, x_sample, cache_a_k, cache_a_v, cache_b_k, cache_b_v, cache_c_k, cache_c_v, c, c_ctx,
           w_ada, b_ada, w_in, rpb_a, lam_b, subln_b, sink_c, w_branch, w_out, ln_g, ln_b, w_gate_up, w_down):
    bp, tp, _ = x_prompt.shape
    bs, ts, _ = x_sample.shape

    def sample_row(tm, i):
        return 1 + (i * tm) // ts

    streams = {
        "p": dict(x=x_prompt.reshape(bp * tp, D_MODEL), rowfn=_prompt_row),
        "s": dict(x=x_sample.reshape(bs * ts, D_MODEL), rowfn=sample_row),
    }
    cc = jnp.concatenate([c_ctx[None, :], c], axis=0)
    cc = jnp.pad(cc, ((0, (-cc.shape[0]) % 8), (0, 0)))
    mod = ada_modulation_all(cc, w_ada, b_ada)

    for st in streams.values():
        st["h"] = modulate_cast(st["x"], mod, 0, 0, st["rowfn"])

    new_kv = [[] for _ in range(6)]
    for l in range(DEPTH):
        w_qkv = cast_weight(w_in, l, 0, QKV_W)
        w_g = cast_weight(w_in, l, QKV_W)
        w_br = cast_weight(w_branch, l)
        w_o = cast_weight(w_out, l)
        w_gu = cast_weight(w_gate_up, l)
        w_dn = cast_weight(w_down, l)
        na_bias = _na_bias_table(rpb_a[l])

        sp = streams["p"]
        zp = matmul(sp["h"], w_qkv, F32, PROJ_TILE, PROJ_TILE, name="qkv_context")
        for lst, off, wdt, shp in (
                (new_kv[0], OFF_KA, MIX_A, (H_A, HEAD_DIM)), (new_kv[1], OFF_VA, MIX_A, (H_A, HEAD_DIM)),
                (new_kv[2], OFF_KB, MIX_B, (H_B, 2, DB_HALF)), (new_kv[3], OFF_VB, MIX_B, (H_B, HEAD_DIM)),
                (new_kv[4], OFF_KC, KVW_C, (KV_C, HEAD_DIM)), (new_kv[5], OFF_VC, KVW_C, (KV_C, HEAD_DIM))):
            lst.append(zp[:, off:off + wdt].reshape((bp, tp) + shp))
        op = context_attention(zp, tp, lam_b[l], subln_b[l], sink_c[l], l)
        sp["o"] = ((op, 0), (op, MIX_A // MIX_B), (op, (MIX_A + MIX_B) // MIX_C))

        ss = streams["s"]
        zs = qkv_latent(ss["h"], w_qkv, ts)
        oa = na_latent(zs, cache_a_k, cache_a_v, l, na_bias, bs, ts)
        ob = diff_latent(zs, cache_b_k, cache_b_v, l, lam_b[l], subln_b[l], bs, ts)
        oc = window_latent(zs, cache_c_k, cache_c_v, l, sink_c[l], bs, ts)
        ss["o"] = ((oa, 0), (ob, 0), (oc, 0))

        for st in streams.values():
            gates = matmul(st["h"], w_g, BF16, PROJ_TILE, PROJ_TILE, sigmoid=True, name="branch_gates")
            u = merge_branches(st["o"], gates, w_br)
            v = matmul_residual(u, w_o, st["x"], mod, l, 0, st["rowfn"], PROJ_TILE, PROJ_TILE, "mixer_out")
            st["x"], h2 = post_norm(v, mod, ln_g[l, 0], ln_b[l, 0], st["rowfn"], next_mod=(l, 1))
            a = swiglu_up(h2, w_gu)
            v = matmul_residual(a, w_dn, st["x"], mod, l, 1, st["rowfn"], DOWN_TILE, DOWN_TILE, "ffn_down")
            nxt = (l + 1, 0) if l + 1 < DEPTH else None
            st["x"], st["h"] = post_norm(v, mod, ln_g[l, 1], ln_b[l, 1], st["rowfn"], next_mod=nxt)

    outs = [jnp.stack(lst, axis=1) for lst in new_kv]
    return (streams["p"]["x"].reshape(bp, tp, D_MODEL), streams["s"]["x"].reshape(bs, ts, D_MODEL), *outs)
```

```python
import functools
import math

import numpy as np
import jax
import jax.numpy as jnp
from jax import lax
from jax.experimental import pallas as pl
from jax.experimental.pallas import tpu as pltpu

D_MODEL = 4096
DEPTH = 2
GRID_W = 64
HEAD_DIM = 128
H_A = 8
MIX_A = H_A * HEAD_DIM
NA_ROWS = 8
NA_COLS = 16
H_B = 8
DB_HALF = HEAD_DIM // 2
MIX_B = H_B * HEAD_DIM
H_C = 16
KV_C = 4
G_C = H_C // KV_C
MIX_C = H_C * HEAD_DIM
KVW_C = KV_C * HEAD_DIM
WINDOW_C = 128
N_BRANCH = 3
QKV_W = 3 * MIX_A + 3 * MIX_B + MIX_C + 2 * KVW_C
D_FF = 11008
ROPE_BASE = 10000.0
LN_EPS = 1e-5
RMS_EPS = 1e-5
ALPHA = (2 * DEPTH) ** 0.25
NEG_INF = -1e30
LOG2E = math.log2(math.e)
BF16 = jnp.bfloat16
F32 = jnp.float32

OFF_QA, OFF_KA, OFF_VA = 0, MIX_A, 2 * MIX_A
OFF_QB, OFF_KB, OFF_VB = 3 * MIX_A, 3 * MIX_A + MIX_B, 3 * MIX_A + 2 * MIX_B
OFF_QC = 3 * MIX_A + 3 * MIX_B
OFF_KC = OFF_QC + MIX_C
OFF_VC = OFF_KC + KVW_C

V7X_VMEM_LIMIT = 56 * 1024 * 1024
V7X_MXU_COLS = 256

PROJ_TILE = 1024
SWIGLU_TM, SWIGLU_TN = 2048, V7X_MXU_COLS
DOWN_TILE = 512
MODULATE_TM, NORM_TM = 512, 512

NA_QROWS = 8
NA_HEADS = 4
SW_TQ = 1024
SW_SUB = WINDOW_C
SW_SPAN = 3 * SW_SUB
DIFF_TQ = 128
DIFF_NSUB = 8


def _params(*sem):
    return pltpu.CompilerParams(dimension_semantics=sem, vmem_limit_bytes=V7X_VMEM_LIMIT)


def _dot(a, b):
    return jnp.dot(a, b, preferred_element_type=F32)


def _dot_nt(a, b):
    return lax.dot_general(a, b, (((1,), (1,)), ((), ())), preferred_element_type=F32)


def _sigmoid(x):
    return 0.5 * jnp.tanh(0.5 * x) + 0.5


def _cast_kernel(w_ref, o_ref):
    o_ref[...] = w_ref[...].astype(o_ref.dtype)


def cast_weight(w, layer, col0=0, ncols=None):
    _, r, c = w.shape
    ncols = c - col0 if ncols is None else ncols
    tc = 512 if r <= D_MODEL else HEAD_DIM
    assert col0 % tc == 0 and ncols % tc == 0
    return pl.pallas_call(
        _cast_kernel,
        grid=(ncols // tc,),
        in_specs=[pl.BlockSpec((None, r, tc), lambda j: (layer, 0, col0 // tc + j))],
        out_specs=pl.BlockSpec((r, tc), lambda j: (0, j)),
        out_shape=jax.ShapeDtypeStruct((r, ncols), BF16),
        compiler_params=_params("arbitrary"),
        name="cast_weight",
    )(w)


def _ada_kernel(c_ref, w_ref, b_ref, o_ref):
    cv = c_ref[...]
    s = (cv * _sigmoid(cv)).astype(BF16)
    o_ref[...] = _dot(s, w_ref[...].astype(BF16)) + b_ref[...]


def ada_modulation_all(cc, w_ada, b_ada):
    r = cc.shape[0]
    n = 6 * D_MODEL
    tn = 512
    out = pl.pallas_call(
        _ada_kernel,
        grid=(DEPTH, n // tn),
        in_specs=[
            pl.BlockSpec((r, D_MODEL), lambda l, j: (0, 0)),
            pl.BlockSpec((None, D_MODEL, tn), lambda l, j: (l, 0, j)),
            pl.BlockSpec((None, 1, tn), lambda l, j: (l, 0, j)),
        ],
        out_specs=pl.BlockSpec((None, r, tn), lambda l, j: (l, 0, j)),
        out_shape=jax.ShapeDtypeStruct((DEPTH, r, n), F32),
        compiler_params=_params("arbitrary", "arbitrary"),
        name="ada_modulation",
    )(cc, w_ada, b_ada.reshape(DEPTH, 1, n))
    return out.reshape(DEPTH, r, 6, 1, D_MODEL)


def _mod_spec(layer, chunk, rowfn):
    return pl.BlockSpec((None, None, None, 1, D_MODEL), lambda i: (layer, rowfn(i), chunk, 0, 0))


def _modulate_kernel(x_ref, sc_ref, sh_ref, o_ref):
    o_ref[...] = (x_ref[...] * (1.0 + sc_ref[...]) + sh_ref[...]).astype(BF16)


def modulate_cast(x, mod, layer, idx, rowfn_of_tile, tm=MODULATE_TM):
    m = x.shape[0]
    rowfn = functools.partial(rowfn_of_tile, tm)
    return pl.pallas_call(
        _modulate_kernel,
        grid=(m // tm,),
        in_specs=[
            pl.BlockSpec((tm, D_MODEL), lambda i: (i, 0)),
            _mod_spec(layer, 3 * idx + 1, rowfn),
            _mod_spec(layer, 3 * idx, rowfn),
        ],
        out_specs=pl.BlockSpec((tm, D_MODEL), lambda i: (i, 0)),
        out_shape=jax.ShapeDtypeStruct((m, D_MODEL), BF16),
        compiler_params=_params("arbitrary"),
        name="modulate",
    )(x, mod, mod)


def _ln_core(v_ref, g_ref, b_ref):
    v = v_ref[...]
    mu = jnp.mean(v, axis=-1, keepdims=True)
    d = v - mu
    var = jnp.mean(d * d, axis=-1, keepdims=True)
    return d * lax.rsqrt(var + LN_EPS) * g_ref[...] + b_ref[...]


def _ln_mod_kernel(v_ref, g_ref, b_ref, sc_ref, sh_ref, xo_ref, ho_ref):
    xn = _ln_core(v_ref, g_ref, b_ref)
    xo_ref[...] = xn
    ho_ref[...] = (xn * (1.0 + sc_ref[...]) + sh_ref[...]).astype(BF16)


def _ln_kernel(v_ref, g_ref, b_ref, xo_ref):
    xo_ref[...] = _ln_core(v_ref, g_ref, b_ref)


def post_norm(v, mod, ln_g, ln_b, rowfn_of_tile, next_mod=None, tm=NORM_TM):
    m = v.shape[0]
    rowfn = functools.partial(rowfn_of_tile, tm)
    row_spec = pl.BlockSpec((tm, D_MODEL), lambda i: (i, 0))
    vec_spec = pl.BlockSpec((1, D_MODEL), lambda i: (0, 0))
    in_specs = [row_spec, vec_spec, vec_spec]
    args = [v, ln_g.reshape(1, D_MODEL), ln_b.reshape(1, D_MODEL)]
    if next_mod is None:
        return pl.pallas_call(
            _ln_kernel, grid=(m // tm,), in_specs=in_specs, out_specs=row_spec,
            out_shape=jax.ShapeDtypeStruct((m, D_MODEL), F32),
            compiler_params=_params("arbitrary"), name="post_norm",
        )(*args), None
    nl, nidx = next_mod
    in_specs += [_mod_spec(nl, 3 * nidx + 1, rowfn), _mod_spec(nl, 3 * nidx, rowfn)]
    args += [mod, mod]
    return pl.pallas_call(
        _ln_mod_kernel, grid=(m // tm,), in_specs=in_specs, out_specs=[row_spec, row_spec],
        out_shape=[jax.ShapeDtypeStruct((m, D_MODEL), F32), jax.ShapeDtypeStruct((m, D_MODEL), BF16)],
        compiler_params=_params("arbitrary"), name="post_norm_modulate",
    )(*args)


def _mm_kernel(x_ref, w_ref, o_ref, *, sigmoid):
    acc = _dot(x_ref[...], w_ref[...])
    if sigmoid:
        acc = _sigmoid(acc)
    o_ref[...] = acc.astype(o_ref.dtype)


def matmul(x, w, out_dtype, tm, tn, sigmoid=False, name="matmul"):
    m, k = x.shape
    n = w.shape[1]
    return pl.pallas_call(
        functools.partial(_mm_kernel, sigmoid=sigmoid),
        grid=(m // tm, n // tn),
        in_specs=[pl.BlockSpec((tm, k), lambda i, j: (i, 0)), pl.BlockSpec((k, tn), lambda i, j: (0, j))],
        out_specs=pl.BlockSpec((tm, tn), lambda i, j: (i, j)),
        out_shape=jax.ShapeDtypeStruct((m, n), out_dtype),
        compiler_params=_params("arbitrary", "arbitrary"),
        name=name,
    )(x, w)


def _mm_residual_kernel(x_ref, w_ref, res_ref, gate_ref, o_ref):
    o_ref[...] = ALPHA * res_ref[...] + gate_ref[...] * _dot(x_ref[...], w_ref[...])


def matmul_residual(x, w, res, mod, layer, idx, rowfn_of_tile, tm, tn, name):
    m, k = x.shape
    n = w.shape[1]
    rowfn = functools.partial(rowfn_of_tile, tm)
    gate_spec = pl.BlockSpec((None, None, None, 1, tn), lambda i, j: (layer, rowfn(i), 3 * idx + 2, 0, j))
    tile_spec = pl.BlockSpec((tm, tn), lambda i, j: (i, j))
    return pl.pallas_call(
        _mm_residual_kernel,
        grid=(m // tm, n // tn),
        in_specs=[pl.BlockSpec((tm, k), lambda i, j: (i, 0)), pl.BlockSpec((k, tn), lambda i, j: (0, j)),
                  tile_spec, gate_spec],
        out_specs=tile_spec,
        out_shape=jax.ShapeDtypeStruct((m, n), F32),
        compiler_params=_params("arbitrary", "arbitrary"),
        name=name,
    )(x, w, res, mod)


def _rope_tables(t, width):
    half, quarter = width // 2, width // 4
    lane = np.arange(HEAD_DIM)
    e = lane % width
    pos = np.arange(t)
    p = np.where((e < half)[None, :], (pos // GRID_W)[:, None], (pos % GRID_W)[:, None]).astype(np.float64)
    inv = np.exp(-math.log(ROPE_BASE) * (e % quarter).astype(np.float64) / quarter)
    ang = (p.astype(np.float32) * inv.astype(np.float32)[None, :]).astype(np.float64)
    first = ((e % half) < quarter)[None, :]
    cos, sin = np.cos(ang), np.sin(ang)
    return jnp.asarray(cos, F32), jnp.asarray(np.where(first, -sin, sin), F32)


def _rope(x, cos, ssin, quarter):
    partner = lax.broadcasted_iota(jnp.int32, x.shape, 1) ^ quarter
    return x * cos + jnp.take_along_axis(x, partner, axis=1) * ssin


def _qkv_latent_kernel(x_ref, w_ref, cb_ref, sb_ref, cc_ref, sc_ref, o_ref, *, tn):
    j = pl.program_id(1)
    acc = _dot(x_ref[...], w_ref[...])
    nchunk = tn // HEAD_DIM
    tiles_qb = (OFF_QB // tn,)
    tiles_kb = (OFF_KB // tn,)
    tiles_qc = tuple(range(OFF_QC // tn, OFF_KC // tn))
    tile_kcvc = OFF_KC // tn

    def is_in(tiles):
        c = j == tiles[0]
        for tt in tiles[1:]:
            c = jnp.logical_or(c, j == tt)
        return c

    def store(fn_of_chunk):
        for ch in range(nchunk):
            sl = slice(ch * HEAD_DIM, (ch + 1) * HEAD_DIM)
            o_ref[:, sl] = fn_of_chunk(ch, acc[:, sl]).astype(o_ref.dtype)

    rope_b = lambda x: _rope(x, cb_ref[...], sb_ref[...], DB_HALF // 4)
    rope_c = lambda x: _rope(x, cc_ref[...], sc_ref[...], HEAD_DIM // 4)
    tiles_qa = tuple(range(OFF_QA // tn, OFF_KA // tn))
    special = is_in(tiles_qa + tiles_qb + tiles_kb + tiles_qc + (tile_kcvc,))

    @pl.when(jnp.logical_not(special))
    def _():
        o_ref[...] = acc.astype(o_ref.dtype)

    @pl.when(is_in(tiles_qa))
    def _():
        o_ref[...] = (acc * (HEAD_DIM ** -0.5 * LOG2E)).astype(o_ref.dtype)

    @pl.when(is_in(tiles_qb))
    def _():
        store(lambda ch, x: rope_b(x) * (DB_HALF ** -0.5 * LOG2E))

    @pl.when(is_in(tiles_kb))
    def _():
        store(lambda ch, x: rope_b(x))

    @pl.when(is_in(tiles_qc))
    def _():
        store(lambda ch, x: rope_c(x) * (HEAD_DIM ** -0.5 * LOG2E))

    @pl.when(j == tile_kcvc)
    def _():
        store(lambda ch, x: rope_c(x) if ch < KVW_C // HEAD_DIM else x)


def qkv_latent(h, w_qkv, seq, tm=PROJ_TILE, tn=PROJ_TILE):
    m, k = h.shape
    assert tn == MIX_A and OFF_KC % tn == 0 and (OFF_VC - OFF_KC) == KVW_C and seq % tm == 0
    tabs = _rope_tables(seq, DB_HALF) + _rope_tables(seq, HEAD_DIM)
    per_seq = seq // tm
    tab_spec = pl.BlockSpec((tm, HEAD_DIM), lambda i, j: (i % per_seq, 0))
    return pl.pallas_call(
        functools.partial(_qkv_latent_kernel, tn=tn),
        grid=(m // tm, QKV_W // tn),
        in_specs=[pl.BlockSpec((tm, k), lambda i, j: (i, 0)), pl.BlockSpec((k, tn), lambda i, j: (0, j))]
        + [tab_spec] * len(tabs),
        out_specs=pl.BlockSpec((tm, tn), lambda i, j: (i, j)),
        out_shape=jax.ShapeDtypeStruct((m, QKV_W), BF16),
        compiler_params=_params("arbitrary", "arbitrary"),
        name="qkv_latent",
    )(h, w_qkv, *tabs)


def _merge_kernel(oa_ref, ob_ref, oc_ref, wa_ref, wb_ref, wc_ref, ga_ref, gb_ref, gc_ref, o_ref):
    y = ga_ref[...].astype(F32) * _dot(oa_ref[...], wa_ref[...])
    y += gb_ref[...].astype(F32) * _dot(ob_ref[...], wb_ref[...])
    y += gc_ref[...].astype(F32) * _dot(oc_ref[...], wc_ref[...])
    o_ref[...] = y.astype(o_ref.dtype)


def merge_branches(o_parts, gates, w_branch, tm=PROJ_TILE, tn=PROJ_TILE):
    (oa, ja), (ob, jb), (oc, jc) = o_parts
    m = gates.shape[0]
    nj = D_MODEL // tn
    return pl.pallas_call(
        _merge_kernel,
        grid=(m // tm, nj),
        in_specs=[
            pl.BlockSpec((tm, MIX_A), lambda i, j: (i, ja)),
            pl.BlockSpec((tm, MIX_B), lambda i, j: (i, jb)),
            pl.BlockSpec((tm, MIX_C), lambda i, j: (i, jc)),
            pl.BlockSpec((MIX_A, tn), lambda i, j: (0, j)),
            pl.BlockSpec((MIX_B, tn), lambda i, j: (MIX_A // MIX_B, j)),
            pl.BlockSpec((MIX_C, tn), lambda i, j: ((MIX_A + MIX_B) // MIX_C, j)),
            pl.BlockSpec((tm, tn), lambda i, j: (i, j)),
            pl.BlockSpec((tm, tn), lambda i, j: (i, nj + j)),
            pl.BlockSpec((tm, tn), lambda i, j: (i, 2 * nj + j)),
        ],
        out_specs=pl.BlockSpec((tm, tn), lambda i, j: (i, j)),
        out_shape=jax.ShapeDtypeStruct((m, D_MODEL), BF16),
        compiler_params=_params("arbitrary", "arbitrary"),
        name="merge_branches",
    )(oa, ob, oc, w_branch, w_branch, w_branch, gates, gates, gates)


def _swiglu_kernel(x_ref, wg_ref, wu_ref, o_ref):
    x = x_ref[...]
    g = _dot(x, wg_ref[...])
    u = _dot(x, wu_ref[...])
    o_ref[...] = (g * _sigmoid(g) * u).astype(o_ref.dtype)


def swiglu_up(h, w_gate_up, tm=SWIGLU_TM, tn=SWIGLU_TN):
    m, k = h.shape
    nj = D_FF // tn
    return pl.pallas_call(
        _swiglu_kernel,
        grid=(m // tm, nj),
        in_specs=[
            pl.BlockSpec((tm, k), lambda i, j: (i, 0)),
            pl.BlockSpec((k, tn), lambda i, j: (0, j)),
            pl.BlockSpec((k, tn), lambda i, j: (0, nj + j)),
        ],
        out_specs=pl.BlockSpec((tm, tn), lambda i, j: (i, j)),
        out_shape=jax.ShapeDtypeStruct((m, D_FF), BF16),
        compiler_params=_params("arbitrary", "arbitrary"),
        name="swiglu_up",
    )(h, w_gate_up, w_gate_up)


def _diff_lambda(lam_ref, layer):
    lam_init = 0.8 - 0.6 * math.exp(-0.3 * layer)
    lf = lam_ref[...]
    a = jnp.sum(lf[0:1, :] * lf[1:2, :], axis=-1, keepdims=True)
    b = jnp.sum(lf[2:3, :] * lf[3:4, :], axis=-1, keepdims=True)
    return jnp.exp(a) - jnp.exp(b) + lam_init, lam_init


def _diff_finish(o, gain_ref, lam_init):
    o = o * lax.rsqrt(jnp.mean(o * o, axis=-1, keepdims=True) + RMS_EPS)
    return o * gain_ref[...] * (1.0 - lam_init)


def _split_halves(q):
    lane = lax.broadcasted_iota(jnp.int32, (1, HEAD_DIM), 1)
    zero = jnp.zeros_like(q)
    return jnp.where(lane < DB_HALF, q, zero), jnp.where(lane >= DB_HALF, q, zero)


def _sink_column(sink_ref, n, t):
    row = lax.broadcasted_iota(jnp.int32, (G_C * t, 1), 0)
    col = jnp.full((G_C * t, 1), sink_ref[n * G_C + G_C - 1], F32)
    for g in range(G_C - 2, -1, -1):
        col = jnp.where(row < (g + 1) * t, sink_ref[n * G_C + g], col)
    return col


def _softmax_parts(parts, extra=None, exp=jnp.exp):
    m = functools.reduce(jnp.maximum, [jnp.max(s, axis=-1, keepdims=True) for s in parts])
    if extra is not None:
        m = jnp.maximum(m, extra)
    ps = [exp(s - m) for s in parts]
    l = functools.reduce(jnp.add, [jnp.sum(p, axis=-1, keepdims=True) for p in ps])
    if extra is not None:
        l = l + exp(extra - m)
    return ps, l


def _context_attn_kernel(sink_ref, z_ref, lam_ref, gain_ref, o_ref, *, layer):
    scale = HEAD_DIM ** -0.5

    def col(off, h, width=HEAD_DIM):
        return z_ref[:, off + h * width: off + (h + 1) * width]

    for h in range(H_A):
        q, k, v = col(OFF_QA, h).astype(BF16), col(OFF_KA, h).astype(BF16), col(OFF_VA, h).astype(BF16)
        (p,), l = _softmax_parts([_dot_nt(q, k) * scale])
        o = _dot((p / l).astype(BF16), v)
        o_ref[:, h * HEAD_DIM:(h + 1) * HEAD_DIM] = o.astype(o_ref.dtype)

    lam, lam_init = _diff_lambda(lam_ref, layer)
    for h in range(H_B):
        q, k, v = col(OFF_QB, h).astype(BF16), col(OFF_KB, h).astype(BF16), col(OFF_VB, h).astype(BF16)
        q1, q2 = _split_halves(q)
        (p0,), l0 = _softmax_parts([_dot_nt(q1, k) * (DB_HALF ** -0.5)])
        (p1,), l1 = _softmax_parts([_dot_nt(q2, k) * (DB_HALF ** -0.5)])
        w = p0 / l0 - lam * (p1 / l1)
        o = _diff_finish(_dot(w.astype(BF16), v), gain_ref, lam_init)
        o_ref[:, MIX_A + h * HEAD_DIM: MIX_A + (h + 1) * HEAD_DIM] = o.astype(o_ref.dtype)

    t = z_ref.shape[0]
    for n in range(KV_C):
        q = jnp.concatenate([col(OFF_QC, n * G_C + g) for g in range(G_C)], axis=0).astype(BF16)
        k, v = col(OFF_KC, n).astype(BF16), col(OFF_VC, n).astype(BF16)
        sink = _sink_column(sink_ref, n, t)
        (p,), l = _softmax_parts([_dot_nt(q, k) * scale], extra=sink)
        o = _dot((p / l).astype(BF16), v)
        for g in range(G_C):
            c0 = MIX_A + MIX_B + (n * G_C + g) * HEAD_DIM
            o_ref[:, c0:c0 + HEAD_DIM] = o[g * t:(g + 1) * t].astype(o_ref.dtype)


def context_attention(z, seq, lam, gain, sink, layer):
    m = z.shape[0]
    return pl.pallas_call(
        functools.partial(_context_attn_kernel, layer=layer),
        grid=(m // seq,),
        in_specs=[
            pl.BlockSpec(memory_space=pltpu.SMEM),
            pl.BlockSpec((seq, QKV_W), lambda b: (b, 0)),
            pl.BlockSpec((4, DB_HALF), lambda b: (0, 0)),
            pl.BlockSpec((1, HEAD_DIM), lambda b: (0, 0)),
        ],
        out_specs=pl.BlockSpec((seq, D_MODEL), lambda b: (b, 0)),
        out_shape=jax.ShapeDtypeStruct((m, D_MODEL), BF16),
        compiler_params=_params("arbitrary"),
        name="context_attention",
    )(sink, z, lam, gain.reshape(1, HEAD_DIM))


def _na_bias_table(rpb_l):
    pad = GRID_W
    rp = jnp.pad(rpb_l.astype(F32) * LOG2E, ((0, 0), (0, 0), (pad, pad)))
    first = pad + NA_COLS - 1
    tz = jnp.stack([rp[:, :, first - qc:first - qc + GRID_W] for qc in range(GRID_W)], axis=2)
    qc, kc = np.arange(GRID_W)[:, None], np.arange(GRID_W)[None, :]
    ws = np.clip(qc - NA_COLS // 2, 0, GRID_W - NA_COLS)
    valid = (kc >= ws) & (kc < ws + NA_COLS)
    tz = jnp.where(jnp.asarray(valid)[None, None], tz, NEG_INF)
    return jnp.stack([jnp.concatenate([tz[:, e0 + j] for j in range(NA_ROWS)], axis=-1)
                      for e0 in range(NA_ROWS)], axis=1)


def _na_latent_kernel(q_ref, k_ref, v_ref, ck_ref, cv_ref, bias_ref, o_ref, *, rows):
    rb = pl.program_id(2)
    span = NA_ROWS * GRID_W
    starts, biases = [], []
    for r in range(NA_QROWS):
        r_abs = rb * NA_QROWS + r
        rs = jnp.clip(r_abs - NA_ROWS // 2, 0, rows - NA_ROWS)
        starts.append(pl.multiple_of(rs * GRID_W, GRID_W))
        biases.append(rs - r_abs + NA_ROWS - 1)
    heads = [slice(hh * HEAD_DIM, (hh + 1) * HEAD_DIM) for hh in range(NA_HEADS)]
    scores = []
    for hh, cols in enumerate(heads):
        q = q_ref[:, cols]
        s_loc = [_dot_nt(q[r * GRID_W:(r + 1) * GRID_W], k_ref[pl.ds(starts[r], span), cols]) + bias_ref[hh, biases[r]]
                 for r in range(NA_QROWS)]
        scores.append([jnp.concatenate(s_loc, axis=0), _dot_nt(q, ck_ref[:, cols].astype(BF16))])
    stats = [_softmax_parts(sc, exp=jnp.exp2) for sc in scores]
    for cols, ((p_loc, p_ctx), l) in zip(heads, stats):
        p_loc = p_loc.astype(BF16)
        o_loc = [_dot(p_loc[r * GRID_W:(r + 1) * GRID_W], v_ref[pl.ds(starts[r], span), cols]) for r in range(NA_QROWS)]
        o = jnp.concatenate(o_loc, axis=0) + _dot(p_ctx.astype(BF16), cv_ref[:, cols].astype(BF16))
        o_ref[:, cols] = (o / l).astype(o_ref.dtype)


def na_latent(z, cache_k, cache_v, layer, bias, batch, seq):
    rows = seq // GRID_W
    assert rows % NA_QROWS == 0 and rows >= NA_ROWS and H_A % NA_HEADS == 0
    nrb = rows // NA_QROWS
    tq = NA_QROWS * GRID_W
    hw = NA_HEADS * HEAD_DIM
    n_ctx = cache_k.shape[2]
    ck = cache_k.reshape(batch, DEPTH, n_ctx, MIX_A)
    cv = cache_v.reshape(batch, DEPTH, n_ctx, MIX_A)
    ctx_spec = pl.BlockSpec((None, None, n_ctx, hw), lambda b, h, rb: (b, layer, 0, h))
    return pl.pallas_call(
        functools.partial(_na_latent_kernel, rows=rows),
        grid=(batch, H_A // NA_HEADS, nrb),
        in_specs=[
            pl.BlockSpec((tq, hw), lambda b, h, rb: (b * nrb + rb, OFF_QA // hw + h)),
            pl.BlockSpec((seq, hw), lambda b, h, rb: (b, OFF_KA // hw + h)),
            pl.BlockSpec((seq, hw), lambda b, h, rb: (b, OFF_VA // hw + h)),
            ctx_spec, ctx_spec,
            pl.BlockSpec((NA_HEADS, NA_ROWS, GRID_W, NA_ROWS * GRID_W), lambda b, h, rb: (h, 0, 0, 0)),
        ],
        out_specs=pl.BlockSpec((tq, hw), lambda b, h, rb: (b * nrb + rb, h)),
        out_shape=jax.ShapeDtypeStruct((batch * seq, MIX_A), BF16),
        compiler_params=_params("arbitrary", "arbitrary", "arbitrary"),
        name="na_latent",
    )(z, z, z, ck, cv, bias)


def _diff_latent_kernel(lam_ref, gain_ref, q_ref, k_ref, v_ref, ck_ref, cv_ref, o_ref, *, layer):
    lam, lam_init = _diff_lambda(lam_ref, layer)
    k = k_ref[...]
    ck = ck_ref[...].astype(BF16)
    cv = cv_ref[...].astype(BF16)
    nsub = q_ref.shape[0] // DIFF_TQ
    rows = [slice(u * DIFF_TQ, (u + 1) * DIFF_TQ) for u in range(nsub)]
    scores = [[[_dot_nt(qh, k), _dot_nt(qh, ck)] for qh in _split_halves(q_ref[r, :])] for r in rows]
    stats = [[_softmax_parts(sc, exp=jnp.exp2) for sc in su] for su in scores]
    for r, ((p0, l0), (p1, l1)) in zip(rows, stats):
        c = lam * l0 / l1
        wl = (p0[0] - p1[0] * c).astype(BF16)
        wc = (p0[1] - p1[1] * c).astype(BF16)
        o = (_dot(wl, v_ref[...]) + _dot(wc, cv)) / l0
        o_ref[r, :] = _diff_finish(o, gain_ref, lam_init).astype(o_ref.dtype)


def diff_latent(z, cache_k, cache_v, layer, lam, gain, batch, seq):
    tq = DIFF_TQ * DIFF_NSUB
    n_ctx = cache_k.shape[2]
    ck = cache_k.reshape(batch, DEPTH, n_ctx, MIX_B)
    cv = cache_v.reshape(batch, DEPTH, n_ctx, MIX_B)
    nqt = seq // tq
    ctx_spec = pl.BlockSpec((None, None, n_ctx, HEAD_DIM), lambda b, h, t: (b, layer, 0, h))
    return pl.pallas_call(
        functools.partial(_diff_latent_kernel, layer=layer),
        grid=(batch, H_B, nqt),
        in_specs=[
            pl.BlockSpec((4, DB_HALF), lambda b, h, t: (0, 0)),
            pl.BlockSpec((1, HEAD_DIM), lambda b, h, t: (0, 0)),
            pl.BlockSpec((tq, HEAD_DIM), lambda b, h, t: (b * nqt + t, OFF_QB // HEAD_DIM + h)),
            pl.BlockSpec((seq, HEAD_DIM), lambda b, h, t: (b, OFF_KB // HEAD_DIM + h)),
            pl.BlockSpec((seq, HEAD_DIM), lambda b, h, t: (b, OFF_VB // HEAD_DIM + h)),
            ctx_spec, ctx_spec,
        ],
        out_specs=pl.BlockSpec((tq, HEAD_DIM), lambda b, h, t: (b * nqt + t, h)),
        out_shape=jax.ShapeDtypeStruct((batch * seq, MIX_B), BF16),
        compiler_params=_params("arbitrary", "arbitrary", "arbitrary"),
        name="diff_latent",
    )(lam, gain.reshape(1, HEAD_DIM), z, z, z, ck, cv)


def _window_mask_table():
    i, j = np.arange(SW_SUB)[:, None], np.arange(SW_SPAN)[None, :]
    tabs = [np.where(np.abs(c * WINDOW_C + i - j) <= WINDOW_C, 0.0, NEG_INF) for c in range(3)]
    return jnp.asarray(np.stack(tabs), F32)


def _window_latent_kernel(sink_ref, mask_ref, q_ref, k_ref, v_ref, ck_ref, cv_ref, o_ref, *, seq):
    n = pl.program_id(1)
    t0 = pl.program_id(2) * SW_TQ
    ck = ck_ref[...].astype(BF16)
    cv = cv_ref[...].astype(BF16)
    sink = _sink_column(sink_ref, n, SW_SUB) * LOG2E
    nsub = SW_TQ // SW_SUB
    rows = [slice(u * SW_SUB, (u + 1) * SW_SUB) for u in range(nsub)]
    starts, scores = [], []
    for u in range(nsub):
        tu = t0 + u * SW_SUB
        ks = pl.multiple_of(jnp.clip(tu - WINDOW_C, 0, seq - SW_SPAN), WINDOW_C)
        case = (tu - ks) // WINDOW_C
        q = jnp.concatenate([q_ref[rows[u], g * HEAD_DIM:(g + 1) * HEAD_DIM] for g in range(G_C)], axis=0)
        s_loc = _dot_nt(q, k_ref[pl.ds(ks, SW_SPAN), :]).reshape(G_C, SW_SUB, SW_SPAN) + mask_ref[case][None]
        starts.append(ks)
        scores.append([s_loc.reshape(G_C * SW_SUB, SW_SPAN), _dot_nt(q, ck)])
    stats = [_softmax_parts(sc, extra=sink, exp=jnp.exp2) for sc in scores]
    for u, ((p_loc, p_ctx), l) in enumerate(stats):
        o = (_dot(p_loc.astype(BF16), v_ref[pl.ds(starts[u], SW_SPAN), :]) + _dot(p_ctx.astype(BF16), cv)) / l
        for g in range(G_C):
            o_ref[rows[u], g * HEAD_DIM:(g + 1) * HEAD_DIM] = o[g * SW_SUB:(g + 1) * SW_SUB].astype(o_ref.dtype)


def window_latent(z, cache_k, cache_v, layer, sink, batch, seq):
    assert seq % SW_TQ == 0 and seq >= SW_SPAN and SW_TQ % SW_SUB == 0
    n_ctx = cache_k.shape[2]
    ck = cache_k.reshape(batch, DEPTH, n_ctx, KVW_C)
    cv = cache_v.reshape(batch, DEPTH, n_ctx, KVW_C)
    nqt = seq // SW_TQ
    gw = G_C * HEAD_DIM
    ctx_spec = pl.BlockSpec((None, None, n_ctx, HEAD_DIM), lambda b, n, t: (b, layer, 0, n))
    return pl.pallas_call(
        functools.partial(_window_latent_kernel, seq=seq),
        grid=(batch, KV_C, nqt),
        in_specs=[
            pl.BlockSpec(memory_space=pltpu.SMEM),
            pl.BlockSpec((3, SW_SUB, SW_SPAN), lambda b, n, t: (0, 0, 0)),
            pl.BlockSpec((SW_TQ, gw), lambda b, n, t: (b * nqt + t, OFF_QC // gw + n)),
            pl.BlockSpec((seq, HEAD_DIM), lambda b, n, t: (b, OFF_KC // HEAD_DIM + n)),
            pl.BlockSpec((seq, HEAD_DIM), lambda b, n, t: (b, OFF_VC // HEAD_DIM + n)),
            ctx_spec, ctx_spec,
        ],
        out_specs=pl.BlockSpec((SW_TQ, gw), lambda b, n, t: (b * nqt + t, n)),
        out_shape=jax.ShapeDtypeStruct((batch * seq, MIX_C), BF16),
        compiler_params=_params("arbitrary", "arbitrary", "arbitrary"),
        name="window_latent",
    )(sink, _window_mask_table(), z, z, z, ck, cv)


def _prompt_row(tm, i):
    return 0


def kernel(x_prompt, x_sample, cache_a_k, cache_a_v, cache_b_k, cache_b_v, cache_c_k, cache_c_v, c, c_ctx,
           w_ada, b_ada, w_in, rpb_a, lam_b, subln_b, sink_c, w_branch, w_out, ln_g, ln_b, w_gate_up, w_down):
    bp, tp, _ = x_prompt.shape
    bs, ts, _ = x_sample.shape

    def sample_row(tm, i):
        return 1 + (i * tm) // ts

    streams = {
        "p": dict(x=x_prompt.reshape(bp * tp, D_MODEL), rowfn=_prompt_row),
        "s": dict(x=x_sample.reshape(bs * ts, D_MODEL), rowfn=sample_row),
    }
    cc = jnp.concatenate([c_ctx[None, :], c], axis=0)
    cc = jnp.pad(cc, ((0, (-cc.shape[0]) % 8), (0, 0)))
    mod = ada_modulation_all(cc, w_ada, b_ada)

    for st in streams.values():
        st["h"] = modulate_cast(st["x"], mod, 0, 0, st["rowfn"])

    new_kv = [[] for _ in range(6)]
    for l in range(DEPTH):
        w_qkv = cast_weight(w_in, l, 0, QKV_W)
        w_g = cast_weight(w_in, l, QKV_W)
        w_br = cast_weight(w_branch, l)
        w_o = cast_weight(w_out, l)
        w_gu = cast_weight(w_gate_up, l)
        w_dn = cast_weight(w_down, l)
        na_bias = _na_bias_table(rpb_a[l])

        sp = streams["p"]
        zp = matmul(sp["h"], w_qkv, F32, PROJ_TILE, PROJ_TILE, name="qkv_context")
        for lst, off, wdt, shp in (
                (new_kv[0], OFF_KA, MIX_A, (H_A, HEAD_DIM)), (new_kv[1], OFF_VA, MIX_A, (H_A, HEAD_DIM)),
                (new_kv[2], OFF_KB, MIX_B, (H_B, 2, DB_HALF)), (new_kv[3], OFF_VB, MIX_B, (H_B, HEAD_DIM)),
                (new_kv[4], OFF_KC, KVW_C, (KV_C, HEAD_DIM)), (new_kv[5], OFF_VC, KVW_C, (KV_C, HEAD_DIM))):
            lst.append(zp[:, off:off + wdt].reshape((bp, tp) + shp))
        op = context_attention(zp, tp, lam_b[l], subln_b[l], sink_c[l], l)
        sp["o"] = ((op, 0), (op, MIX_A // MIX_B), (op, (MIX_A + MIX_B) // MIX_C))

        ss = streams["s"]
        zs = qkv_latent(ss["h"], w_qkv, ts)
        oa = na_latent(zs, cache_a_k, cache_a_v, l, na_bias, bs, ts)
        ob = diff_latent(zs, cache_b_k, cache_b_v, l, lam_b[l], subln_b[l], bs, ts)
        oc = window_latent(zs, cache_c_k, cache_c_v, l, sink_c[l], bs, ts)
        ss["o"] = ((oa, 0), (ob, 0), (oc, 0))

        for st in streams.values():
            gates = matmul(st["h"], w_g, BF16, PROJ_TILE, PROJ_TILE, sigmoid=True, name="branch_gates")
            u = merge_branches(st["o"], gates, w_br)
            v = matmul_residual(u, w_o, st["x"], mod, l, 0, st["rowfn"], PROJ_TILE, PROJ_TILE, "mixer_out")
            st["x"], h2 = post_norm(v, mod, ln_g[l, 0], ln_b[l, 0], st["rowfn"], next_mod=(l, 1))
            a = swiglu_up(h2, w_gu)
            v = matmul_residual(a, w_dn, st["x"], mod, l, 1, st["rowfn"], DOWN_TILE, DOWN_TILE, "ffn_down")
            nxt = (l + 1, 0) if l + 1 < DEPTH else None
            st["x"], st["h"] = post_norm(v, mod, ln_g[l, 1], ln_b[l, 1], st["rowfn"], next_mod=nxt)

    outs = [jnp.stack(lst, axis=1) for lst in new_kv]
    return (streams["p"]["x"].reshape(bp, tp, D_MODEL), streams["s"]["x"].reshape(bs, ts, D_MODEL), *outs)
```

```python
import functools
import math

import numpy as np
import jax
import jax.numpy as jnp
from jax import lax
from jax.experimental import pallas as pl
from jax.experimental.pallas import tpu as pltpu

D_MODEL = 4096
DEPTH = 2
GRID_W = 64
HEAD_DIM = 128
H_A = 8
MIX_A = H_A * HEAD_DIM
NA_ROWS = 8
NA_COLS = 16
H_B = 8
DB_HALF = HEAD_DIM // 2
MIX_B = H_B * HEAD_DIM
H_C = 16
KV_C = 4
G_C = H_C // KV_C
MIX_C = H_C * HEAD_DIM
KVW_C = KV_C * HEAD_DIM
WINDOW_C = 128
N_BRANCH = 3
QKV_W = 3 * MIX_A + 3 * MIX_B + MIX_C + 2 * KVW_C
D_FF = 11008
ROPE_BASE = 10000.0
LN_EPS = 1e-5
RMS_EPS = 1e-5
ALPHA = (2 * DEPTH) ** 0.25
NEG_INF = -1e30
LOG2E = math.log2(math.e)
BF16 = jnp.bfloat16
F32 = jnp.float32

OFF_QA, OFF_KA, OFF_VA = 0, MIX_A, 2 * MIX_A
OFF_QB, OFF_KB, OFF_VB = 3 * MIX_A, 3 * MIX_A + MIX_B, 3 * MIX_A + 2 * MIX_B
OFF_QC = 3 * MIX_A + 3 * MIX_B
OFF_KC = OFF_QC + MIX_C
OFF_VC = OFF_KC + KVW_C

V7X_VMEM_LIMIT = 56 * 1024 * 1024
V7X_MXU_COLS = 256

PROJ_TILE = 1024
SWIGLU_TM, SWIGLU_TN = 4096, V7X_MXU_COLS
DOWN_TILE = 512
MODULATE_TM, NORM_TM = 512, 512

NA_QROWS = 8
NA_HEADS = 4
SW_TQ = 1024
SW_SUB = WINDOW_C
SW_SPAN = 3 * SW_SUB
DIFF_TQ = 128
DIFF_NSUB = 8


def _params(*sem):
    return pltpu.CompilerParams(dimension_semantics=sem, vmem_limit_bytes=V7X_VMEM_LIMIT)


def _dot(a, b):
    return jnp.dot(a, b, preferred_element_type=F32)


def _dot_nt(a, b):
    return lax.dot_general(a, b, (((1,), (1,)), ((), ())), preferred_element_type=F32)


def _sigmoid(x):
    return 0.5 * jnp.tanh(0.5 * x) + 0.5


def _cast_kernel(w_ref, o_ref):
    o_ref[...] = w_ref[...].astype(o_ref.dtype)


def cast_weight(w, layer, col0=0, ncols=None):
    _, r, c = w.shape
    ncols = c - col0 if ncols is None else ncols
    tc = 512 if r <= D_MODEL else HEAD_DIM
    assert col0 % tc == 0 and ncols % tc == 0
    return pl.pallas_call(
        _cast_kernel,
        grid=(ncols // tc,),
        in_specs=[pl.BlockSpec((None, r, tc), lambda j: (layer, 0, col0 // tc + j))],
        out_specs=pl.BlockSpec((r, tc), lambda j: (0, j)),
        out_shape=jax.ShapeDtypeStruct((r, ncols), BF16),
        compiler_params=_params("arbitrary"),
        name="cast_weight",
    )(w)


def _ada_kernel(c_ref, w_ref, b_ref, o_ref):
    cv = c_ref[...]
    s = (cv * _sigmoid(cv)).astype(BF16)
    o_ref[...] = _dot(s, w_ref[...].astype(BF16)) + b_ref[...]


def ada_modulation_all(cc, w_ada, b_ada):
    r = cc.shape[0]
    n = 6 * D_MODEL
    tn = 512
    out = pl.pallas_call(
        _ada_kernel,
        grid=(DEPTH, n // tn),
        in_specs=[
            pl.BlockSpec((r, D_MODEL), lambda l, j: (0, 0)),
            pl.BlockSpec((None, D_MODEL, tn), lambda l, j: (l, 0, j)),
            pl.BlockSpec((None, 1, tn), lambda l, j: (l, 0, j)),
        ],
        out_specs=pl.BlockSpec((None, r, tn), lambda l, j: (l, 0, j)),
        out_shape=jax.ShapeDtypeStruct((DEPTH, r, n), F32),
        compiler_params=_params("arbitrary", "arbitrary"),
        name="ada_modulation",
    )(cc, w_ada, b_ada.reshape(DEPTH, 1, n))
    return out.reshape(DEPTH, r, 6, 1, D_MODEL)


def _mod_spec(layer, chunk, rowfn):
    return pl.BlockSpec((None, None, None, 1, D_MODEL), lambda i: (layer, rowfn(i), chunk, 0, 0))


def _modulate_kernel(x_ref, sc_ref, sh_ref, o_ref):
    o_ref[...] = (x_ref[...] * (1.0 + sc_ref[...]) + sh_ref[...]).astype(BF16)


def modulate_cast(x, mod, layer, idx, rowfn_of_tile, tm=MODULATE_TM):
    m = x.shape[0]
    rowfn = functools.partial(rowfn_of_tile, tm)
    return pl.pallas_call(
        _modulate_kernel,
        grid=(m // tm,),
        in_specs=[
            pl.BlockSpec((tm, D_MODEL), lambda i: (i, 0)),
            _mod_spec(layer, 3 * idx + 1, rowfn),
            _mod_spec(layer, 3 * idx, rowfn),
        ],
        out_specs=pl.BlockSpec((tm, D_MODEL), lambda i: (i, 0)),
        out_shape=jax.ShapeDtypeStruct((m, D_MODEL), BF16),
        compiler_params=_params("arbitrary"),
        name="modulate",
    )(x, mod, mod)


def _ln_core(v_ref, g_ref, b_ref):
    v = v_ref[...]
    mu = jnp.mean(v, axis=-1, keepdims=True)
    d = v - mu
    var = jnp.mean(d * d, axis=-1, keepdims=True)
    return d * lax.rsqrt(var + LN_EPS) * g_ref[...] + b_ref[...]


def _ln_mod_kernel(v_ref, g_ref, b_ref, sc_ref, sh_ref, xo_ref, ho_ref):
    xn = _ln_core(v_ref, g_ref, b_ref)
    xo_ref[...] = xn
    ho_ref[...] = (xn * (1.0 + sc_ref[...]) + sh_ref[...]).astype(BF16)


def _ln_kernel(v_ref, g_ref, b_ref, xo_ref):
    xo_ref[...] = _ln_core(v_ref, g_ref, b_ref)


def post_norm(v, mod, ln_g, ln_b, rowfn_of_tile, next_mod=None, tm=NORM_TM):
    m = v.shape[0]
    rowfn = functools.partial(rowfn_of_tile, tm)
    row_spec = pl.BlockSpec((tm, D_MODEL), lambda i: (i, 0))
    vec_spec = pl.BlockSpec((1, D_MODEL), lambda i: (0, 0))
    in_specs = [row_spec, vec_spec, vec_spec]
    args = [v, ln_g.reshape(1, D_MODEL), ln_b.reshape(1, D_MODEL)]
    if next_mod is None:
        return pl.pallas_call(
            _ln_kernel, grid=(m // tm,), in_specs=in_specs, out_specs=row_spec,
            out_shape=jax.ShapeDtypeStruct((m, D_MODEL), F32),
            compiler_params=_params("arbitrary"), name="post_norm",
        )(*args), None
    nl, nidx = next_mod
    in_specs += [_mod_spec(nl, 3 * nidx + 1, rowfn), _mod_spec(nl, 3 * nidx, rowfn)]
    args += [mod, mod]
    return pl.pallas_call(
        _ln_mod_kernel, grid=(m // tm,), in_specs=in_specs, out_specs=[row_spec, row_spec],
        out_shape=[jax.ShapeDtypeStruct((m, D_MODEL), F32), jax.ShapeDtypeStruct((m, D_MODEL), BF16)],
        compiler_params=_params("arbitrary"), name="post_norm_modulate",
    )(*args)


def _mm_kernel(x_ref, w_ref, o_ref, *, sigmoid):
    acc = _dot(x_ref[...], w_ref[...])
    if sigmoid:
        acc = _sigmoid(acc)
    o_ref[...] = acc.astype(o_ref.dtype)


def matmul(x, w, out_dtype, tm, tn, sigmoid=False, name="matmul"):
    m, k = x.shape
    n = w.shape[1]
    return pl.pallas_call(
        functools.partial(_mm_kernel, sigmoid=sigmoid),
        grid=(m // tm, n // tn),
        in_specs=[pl.BlockSpec((tm, k), lambda i, j: (i, 0)), pl.BlockSpec((k, tn), lambda i, j: (0, j))],
        out_specs=pl.BlockSpec((tm, tn), lambda i, j: (i, j)),
        out_shape=jax.ShapeDtypeStruct((m, n), out_dtype),
        compiler_params=_params("arbitrary", "arbitrary"),
        name=name,
    )(x, w)


def _mm_residual_kernel(x_ref, w_ref, res_ref, gate_ref, o_ref):
    o_ref[...] = ALPHA * res_ref[...] + gate_ref[...] * _dot(x_ref[...], w_ref[...])


def matmul_residual(x, w, res, mod, layer, idx, rowfn_of_tile, tm, tn, name):
    m, k = x.shape
    n = w.shape[1]
    rowfn = functools.partial(rowfn_of_tile, tm)
    gate_spec = pl.BlockSpec((None, None, None, 1, tn), lambda i, j: (layer, rowfn(i), 3 * idx + 2, 0, j))
    tile_spec = pl.BlockSpec((tm, tn), lambda i, j: (i, j))
    return pl.pallas_call(
        _mm_residual_kernel,
        grid=(m // tm, n // tn),
        in_specs=[pl.BlockSpec((tm, k), lambda i, j: (i, 0)), pl.BlockSpec((k, tn), lambda i, j: (0, j)),
                  tile_spec, gate_spec],
        out_specs=tile_spec,
        out_shape=jax.ShapeDtypeStruct((m, n), F32),
        compiler_params=_params("arbitrary", "arbitrary"),
        name=name,
    )(x, w, res, mod)


def _rope_tables(t, width):
    half, quarter = width // 2, width // 4
    lane = np.arange(HEAD_DIM)
    e = lane % width
    pos = np.arange(t)
    p = np.where((e < half)[None, :], (pos // GRID_W)[:, None], (pos % GRID_W)[:, None]).astype(np.float64)
    inv = np.exp(-math.log(ROPE_BASE) * (e % quarter).astype(np.float64) / quarter)
    ang = (p.astype(np.float32) * inv.astype(np.float32)[None, :]).astype(np.float64)
    first = ((e % half) < quarter)[None, :]
    cos, sin = np.cos(ang), np.sin(ang)
    return jnp.asarray(cos, F32), jnp.asarray(np.where(first, -sin, sin), F32)


def _rope(x, cos, ssin, quarter):
    partner = lax.broadcasted_iota(jnp.int32, x.shape, 1) ^ quarter
    return x * cos + jnp.take_along_axis(x, partner, axis=1) * ssin


def _qkv_latent_kernel(x_ref, w_ref, cb_ref, sb_ref, cc_ref, sc_ref, o_ref, *, tn):
    j = pl.program_id(1)
    acc = _dot(x_ref[...], w_ref[...])
    nchunk = tn // HEAD_DIM
    tiles_qb = (OFF_QB // tn,)
    tiles_kb = (OFF_KB // tn,)
    tiles_qc = tuple(range(OFF_QC // tn, OFF_KC // tn))
    tile_kcvc = OFF_KC // tn

    def is_in(tiles):
        c = j == tiles[0]
        for tt in tiles[1:]:
            c = jnp.logical_or(c, j == tt)
        return c

    def store(fn_of_chunk):
        for ch in range(nchunk):
            sl = slice(ch * HEAD_DIM, (ch + 1) * HEAD_DIM)
            o_ref[:, sl] = fn_of_chunk(ch, acc[:, sl]).astype(o_ref.dtype)

    rope_b = lambda x: _rope(x, cb_ref[...], sb_ref[...], DB_HALF // 4)
    rope_c = lambda x: _rope(x, cc_ref[...], sc_ref[...], HEAD_DIM // 4)
    tiles_qa = tuple(range(OFF_QA // tn, OFF_KA // tn))
    special = is_in(tiles_qa + tiles_qb + tiles_kb + tiles_qc + (tile_kcvc,))

    @pl.when(jnp.logical_not(special))
    def _():
        o_ref[...] = acc.astype(o_ref.dtype)

    @pl.when(is_in(tiles_qa))
    def _():
        o_ref[...] = (acc * (HEAD_DIM ** -0.5 * LOG2E)).astype(o_ref.dtype)

    @pl.when(is_in(tiles_qb))
    def _():
        store(lambda ch, x: rope_b(x) * (DB_HALF ** -0.5 * LOG2E))

    @pl.when(is_in(tiles_kb))
    def _():
        store(lambda ch, x: rope_b(x))

    @pl.when(is_in(tiles_qc))
    def _():
        store(lambda ch, x: rope_c(x) * (HEAD_DIM ** -0.5 * LOG2E))

    @pl.when(j == tile_kcvc)
    def _():
        store(lambda ch, x: rope_c(x) if ch < KVW_C // HEAD_DIM else x)


def qkv_latent(h, w_qkv, seq, tm=PROJ_TILE, tn=PROJ_TILE):
    m, k = h.shape
    assert tn == MIX_A and OFF_KC % tn == 0 and (OFF_VC - OFF_KC) == KVW_C and seq % tm == 0
    tabs = _rope_tables(seq, DB_HALF) + _rope_tables(seq, HEAD_DIM)
    per_seq = seq // tm
    tab_spec = pl.BlockSpec((tm, HEAD_DIM), lambda i, j: (i % per_seq, 0))
    return pl.pallas_call(
        functools.partial(_qkv_latent_kernel, tn=tn),
        grid=(m // tm, QKV_W // tn),
        in_specs=[pl.BlockSpec((tm, k), lambda i, j: (i, 0)), pl.BlockSpec((k, tn), lambda i, j: (0, j))]
        + [tab_spec] * len(tabs),
        out_specs=pl.BlockSpec((tm, tn), lambda i, j: (i, j)),
        out_shape=jax.ShapeDtypeStruct((m, QKV_W), BF16),
        compiler_params=_params("arbitrary", "arbitrary"),
        name="qkv_latent",
    )(h, w_qkv, *tabs)


def _merge_kernel(oa_ref, ob_ref, oc_ref, wa_ref, wb_ref, wc_ref, ga_ref, gb_ref, gc_ref, o_ref):
    y = ga_ref[...].astype(F32) * _dot(oa_ref[...], wa_ref[...])
    y += gb_ref[...].astype(F32) * _dot(ob_ref[...], wb_ref[...])
    y += gc_ref[...].astype(F32) * _dot(oc_ref[...], wc_ref[...])
    o_ref[...] = y.astype(o_ref.dtype)


def merge_branches(o_parts, gates, w_branch, tm=PROJ_TILE, tn=PROJ_TILE):
    (oa, ja), (ob, jb), (oc, jc) = o_parts
    m = gates.shape[0]
    nj = D_MODEL // tn
    return pl.pallas_call(
        _merge_kernel,
        grid=(m // tm, nj),
        in_specs=[
            pl.BlockSpec((tm, MIX_A), lambda i, j: (i, ja)),
            pl.BlockSpec((tm, MIX_B), lambda i, j: (i, jb)),
            pl.BlockSpec((tm, MIX_C), lambda i, j: (i, jc)),
            pl.BlockSpec((MIX_A, tn), lambda i, j: (0, j)),
            pl.BlockSpec((MIX_B, tn), lambda i, j: (MIX_A // MIX_B, j)),
            pl.BlockSpec((MIX_C, tn), lambda i, j: ((MIX_A + MIX_B) // MIX_C, j)),
            pl.BlockSpec((tm, tn), lambda i, j: (i, j)),
            pl.BlockSpec((tm, tn), lambda i, j: (i, nj + j)),
            pl.BlockSpec((tm, tn), lambda i, j: (i, 2 * nj + j)),
        ],
        out_specs=pl.BlockSpec((tm, tn), lambda i, j: (i, j)),
        out_shape=jax.ShapeDtypeStruct((m, D_MODEL), BF16),
        compiler_params=_params("arbitrary", "arbitrary"),
        name="merge_branches",
    )(oa, ob, oc, w_branch, w_branch, w_branch, gates, gates, gates)


def _swiglu_kernel(x_ref, wg_ref, wu_ref, o_ref):
    x = x_ref[...]
    g = _dot(x, wg_ref[...])
    u = _dot(x, wu_ref[...])
    o_ref[...] = (g * _sigmoid(g) * u).astype(o_ref.dtype)


def swiglu_up(h, w_gate_up, tm=SWIGLU_TM, tn=SWIGLU_TN):
    m, k = h.shape
    nj = D_FF // tn
    return pl.pallas_call(
        _swiglu_kernel,
        grid=(m // tm, nj),
        in_specs=[
            pl.BlockSpec((tm, k), lambda i, j: (i, 0), pipeline_mode=pl.Buffered(1)),
            pl.BlockSpec((k, tn), lambda i, j: (0, j)),
            pl.BlockSpec((k, tn), lambda i, j: (0, nj + j)),
        ],
        out_specs=pl.BlockSpec((tm, tn), lambda i, j: (i, j)),
        out_shape=jax.ShapeDtypeStruct((m, D_FF), BF16),
        compiler_params=_params("arbitrary", "arbitrary"),
        name="swiglu_up",
    )(h, w_gate_up, w_gate_up)


def _diff_lambda(lam_ref, layer):
    lam_init = 0.8 - 0.6 * math.exp(-0.3 * layer)
    lf = lam_ref[...]
    a = jnp.sum(lf[0:1, :] * lf[1:2, :], axis=-1, keepdims=True)
    b = jnp.sum(lf[2:3, :] * lf[3:4, :], axis=-1, keepdims=True)
    return jnp.exp(a) - jnp.exp(b) + lam_init, lam_init


def _diff_finish(o, gain_ref, lam_init):
    o = o * lax.rsqrt(jnp.mean(o * o, axis=-1, keepdims=True) + RMS_EPS)
    return o * gain_ref[...] * (1.0 - lam_init)


def _split_halves(q):
    lane = lax.broadcasted_iota(jnp.int32, (1, HEAD_DIM), 1)
    zero = jnp.zeros_like(q)
    return jnp.where(lane < DB_HALF, q, zero), jnp.where(lane >= DB_HALF, q, zero)


def _sink_column(sink_ref, n, t):
    row = lax.broadcasted_iota(jnp.int32, (G_C * t, 1), 0)
    col = jnp.full((G_C * t, 1), sink_ref[n * G_C + G_C - 1], F32)
    for g in range(G_C - 2, -1, -1):
        col = jnp.where(row < (g + 1) * t, sink_ref[n * G_C + g], col)
    return col


def _softmax_parts(parts, extra=None, exp=jnp.exp):
    m = functools.reduce(jnp.maximum, [jnp.max(s, axis=-1, keepdims=True) for s in parts])
    if extra is not None:
        m = jnp.maximum(m, extra)
    ps = [exp(s - m) for s in parts]
    l = functools.reduce(jnp.add, [jnp.sum(p, axis=-1, keepdims=True) for p in ps])
    if extra is not None:
        l = l + exp(extra - m)
    return ps, l


def _context_attn_kernel(sink_ref, z_ref, lam_ref, gain_ref, o_ref, *, layer):
    scale = HEAD_DIM ** -0.5

    def col(off, h, width=HEAD_DIM):
        return z_ref[:, off + h * width: off + (h + 1) * width]

    for h in range(H_A):
        q, k, v = col(OFF_QA, h).astype(BF16), col(OFF_KA, h).astype(BF16), col(OFF_VA, h).astype(BF16)
        (p,), l = _softmax_parts([_dot_nt(q, k) * scale])
        o = _dot((p / l).astype(BF16), v)
        o_ref[:, h * HEAD_DIM:(h + 1) * HEAD_DIM] = o.astype(o_ref.dtype)

    lam, lam_init = _diff_lambda(lam_ref, layer)
    for h in range(H_B):
        q, k, v = col(OFF_QB, h).astype(BF16), col(OFF_KB, h).astype(BF16), col(OFF_VB, h).astype(BF16)
        q1, q2 = _split_halves(q)
        (p0,), l0 = _softmax_parts([_dot_nt(q1, k) * (DB_HALF ** -0.5)])
        (p1,), l1 = _softmax_parts([_dot_nt(q2, k) * (DB_HALF ** -0.5)])
        w = p0 / l0 - lam * (p1 / l1)
        o = _diff_finish(_dot(w.astype(BF16), v), gain_ref, lam_init)
        o_ref[:, MIX_A + h * HEAD_DIM: MIX_A + (h + 1) * HEAD_DIM] = o.astype(o_ref.dtype)

    t = z_ref.shape[0]
    for n in range(KV_C):
        q = jnp.concatenate([col(OFF_QC, n * G_C + g) for g in range(G_C)], axis=0).astype(BF16)
        k, v = col(OFF_KC, n).astype(BF16), col(OFF_VC, n).astype(BF16)
        sink = _sink_column(sink_ref, n, t)
        (p,), l = _softmax_parts([_dot_nt(q, k) * scale], extra=sink)
        o = _dot((p / l).astype(BF16), v)
        for g in range(G_C):
            c0 = MIX_A + MIX_B + (n * G_C + g) * HEAD_DIM
            o_ref[:, c0:c0 + HEAD_DIM] = o[g * t:(g + 1) * t].astype(o_ref.dtype)


def context_attention(z, seq, lam, gain, sink, layer):
    m = z.shape[0]
    return pl.pallas_call(
        functools.partial(_context_attn_kernel, layer=layer),
        grid=(m // seq,),
        in_specs=[
            pl.BlockSpec(memory_space=pltpu.SMEM),
            pl.BlockSpec((seq, QKV_W), lambda b: (b, 0)),
            pl.BlockSpec((4, DB_HALF), lambda b: (0, 0)),
            pl.BlockSpec((1, HEAD_DIM), lambda b: (0, 0)),
        ],
        out_specs=pl.BlockSpec((seq, D_MODEL), lambda b: (b, 0)),
        out_shape=jax.ShapeDtypeStruct((m, D_MODEL), BF16),
        compiler_params=_params("arbitrary"),
        name="context_attention",
    )(sink, z, lam, gain.reshape(1, HEAD_DIM))


def _na_bias_table(rpb_l):
    pad = GRID_W
    rp = jnp.pad(rpb_l.astype(F32) * LOG2E, ((0, 0), (0, 0), (pad, pad)))
    first = pad + NA_COLS - 1
    tz = jnp.stack([rp[:, :, first - qc:first - qc + GRID_W] for qc in range(GRID_W)], axis=2)
    qc, kc = np.arange(GRID_W)[:, None], np.arange(GRID_W)[None, :]
    ws = np.clip(qc - NA_COLS // 2, 0, GRID_W - NA_COLS)
    valid = (kc >= ws) & (kc < ws + NA_COLS)
    tz = jnp.where(jnp.asarray(valid)[None, None], tz, NEG_INF)
    return jnp.stack([jnp.concatenate([tz[:, e0 + j] for j in range(NA_ROWS)], axis=-1)
                      for e0 in range(NA_ROWS)], axis=1)


def _na_latent_kernel(q_ref, k_ref, v_ref, ck_ref, cv_ref, bias_ref, o_ref, *, rows):
    rb = pl.program_id(2)
    span = NA_ROWS * GRID_W
    starts, biases = [], []
    for r in range(NA_QROWS):
        r_abs = rb * NA_QROWS + r
        rs = jnp.clip(r_abs - NA_ROWS // 2, 0, rows - NA_ROWS)
        starts.append(pl.multiple_of(rs * GRID_W, GRID_W))
        biases.append(rs - r_abs + NA_ROWS - 1)
    heads = [slice(hh * HEAD_DIM, (hh + 1) * HEAD_DIM) for hh in range(NA_HEADS)]
    scores = []
    for hh, cols in enumerate(heads):
        q = q_ref[:, cols]
        s_loc = [_dot_nt(q[r * GRID_W:(r + 1) * GRID_W], k_ref[pl.ds(starts[r], span), cols]) + bias_ref[hh, biases[r]]
                 for r in range(NA_QROWS)]
        scores.append([jnp.concatenate(s_loc, axis=0), _dot_nt(q, ck_ref[:, cols].astype(BF16))])
    stats = [_softmax_parts(sc, exp=jnp.exp2) for sc in scores]
    for cols, ((p_loc, p_ctx), l) in zip(heads, stats):
        p_loc = p_loc.astype(BF16)
        o_loc = [_dot(p_loc[r * GRID_W:(r + 1) * GRID_W], v_ref[pl.ds(starts[r], span), cols]) for r in range(NA_QROWS)]
        o = jnp.concatenate(o_loc, axis=0) + _dot(p_ctx.astype(BF16), cv_ref[:, cols].astype(BF16))
        o_ref[:, cols] = (o / l).astype(o_ref.dtype)


def na_latent(z, cache_k, cache_v, layer, bias, batch, seq):
    rows = seq // GRID_W
    assert rows % NA_QROWS == 0 and rows >= NA_ROWS and H_A % NA_HEADS == 0
    nrb = rows // NA_QROWS
    tq = NA_QROWS * GRID_W
    hw = NA_HEADS * HEAD_DIM
    n_ctx = cache_k.shape[2]
    ck = cache_k.reshape(batch, DEPTH, n_ctx, MIX_A)
    cv = cache_v.reshape(batch, DEPTH, n_ctx, MIX_A)
    ctx_spec = pl.BlockSpec((None, None, n_ctx, hw), lambda b, h, rb: (b, layer, 0, h))
    return pl.pallas_call(
        functools.partial(_na_latent_kernel, rows=rows),
        grid=(batch, H_A // NA_HEADS, nrb),
        in_specs=[
            pl.BlockSpec((tq, hw), lambda b, h, rb: (b * nrb + rb, OFF_QA // hw + h)),
            pl.BlockSpec((seq, hw), lambda b, h, rb: (b, OFF_KA // hw + h)),
            pl.BlockSpec((seq, hw), lambda b, h, rb: (b, OFF_VA // hw + h)),
            ctx_spec, ctx_spec,
            pl.BlockSpec((NA_HEADS, NA_ROWS, GRID_W, NA_ROWS * GRID_W), lambda b, h, rb: (h, 0, 0, 0)),
        ],
        out_specs=pl.BlockSpec((tq, hw), lambda b, h, rb: (b * nrb + rb, h)),
        out_shape=jax.ShapeDtypeStruct((batch * seq, MIX_A), BF16),
        compiler_params=_params("arbitrary", "arbitrary", "arbitrary"),
        name="na_latent",
    )(z, z, z, ck, cv, bias)


def _diff_latent_kernel(lam_ref, gain_ref, q_ref, k_ref, v_ref, ck_ref, cv_ref, o_ref, *, layer):
    lam, lam_init = _diff_lambda(lam_ref, layer)
    k = k_ref[...]
    ck = ck_ref[...].astype(BF16)
    cv = cv_ref[...].astype(BF16)
    nsub = q_ref.shape[0] // DIFF_TQ
    rows = [slice(u * DIFF_TQ, (u + 1) * DIFF_TQ) for u in range(nsub)]
    scores = [[[_dot_nt(qh, k), _dot_nt(qh, ck)] for qh in _split_halves(q_ref[r, :])] for r in rows]
    stats = [[_softmax_parts(sc, exp=jnp.exp2) for sc in su] for su in scores]
    for r, ((p0, l0), (p1, l1)) in zip(rows, stats):
        c = lam * l0 / l1
        wl = (p0[0] - p1[0] * c).astype(BF16)
        wc = (p0[1] - p1[1] * c).astype(BF16)
        o = (_dot(wl, v_ref[...]) + _dot(wc, cv)) / l0
        o_ref[r, :] = _diff_finish(o, gain_ref, lam_init).astype(o_ref.dtype)


def diff_latent(z, cache_k, cache_v, layer, lam, gain, batch, seq):
    tq = DIFF_TQ * DIFF_NSUB
    n_ctx = cache_k.shape[2]
    ck = cache_k.reshape(batch, DEPTH, n_ctx, MIX_B)
    cv = cache_v.reshape(batch, DEPTH, n_ctx, MIX_B)
    nqt = seq // tq
    ctx_spec = pl.BlockSpec((None, None, n_ctx, HEAD_DIM), lambda b, h, t: (b, layer, 0, h))
    return pl.pallas_call(
        functools.partial(_diff_latent_kernel, layer=layer),
        grid=(batch, H_B, nqt),
        in_specs=[
            pl.BlockSpec((4, DB_HALF), lambda b, h, t: (0, 0)),
            pl.BlockSpec((1, HEAD_DIM), lambda b, h, t: (0, 0)),
            pl.BlockSpec((tq, HEAD_DIM), lambda b, h, t: (b * nqt + t, OFF_QB // HEAD_DIM + h)),
            pl.BlockSpec((seq, HEAD_DIM), lambda b, h, t: (b, OFF_KB // HEAD_DIM + h)),
            pl.BlockSpec((seq, HEAD_DIM), lambda b, h, t: (b, OFF_VB // HEAD_DIM + h)),
            ctx_spec, ctx_spec,
        ],
        out_specs=pl.BlockSpec((tq, HEAD_DIM), lambda b, h, t: (b * nqt + t, h)),
        out_shape=jax.ShapeDtypeStruct((batch * seq, MIX_B), BF16),
        compiler_params=_params("arbitrary", "arbitrary", "arbitrary"),
        name="diff_latent",
    )(lam, gain.reshape(1, HEAD_DIM), z, z, z, ck, cv)


def _window_mask_table():
    i, j = np.arange(SW_SUB)[:, None], np.arange(SW_SPAN)[None, :]
    tabs = [np.where(np.abs(c * WINDOW_C + i - j) <= WINDOW_C, 0.0, NEG_INF) for c in range(3)]
    return jnp.asarray(np.stack(tabs), F32)


def _window_latent_kernel(sink_ref, mask_ref, q_ref, k_ref, v_ref, ck_ref, cv_ref, o_ref, *, seq):
    n = pl.program_id(1)
    t0 = pl.program_id(2) * SW_TQ
    ck = ck_ref[...].astype(BF16)
    cv = cv_ref[...].astype(BF16)
    sink = _sink_column(sink_ref, n, SW_SUB) * LOG2E
    nsub = SW_TQ // SW_SUB
    rows = [slice(u * SW_SUB, (u + 1) * SW_SUB) for u in range(nsub)]
    starts, scores = [], []
    for u in range(nsub):
        tu = t0 + u * SW_SUB
        ks = pl.multiple_of(jnp.clip(tu - WINDOW_C, 0, seq - SW_SPAN), WINDOW_C)
        case = (tu - ks) // WINDOW_C
        q = jnp.concatenate([q_ref[rows[u], g * HEAD_DIM:(g + 1) * HEAD_DIM] for g in range(G_C)], axis=0)
        s_loc = _dot_nt(q, k_ref[pl.ds(ks, SW_SPAN), :]).reshape(G_C, SW_SUB, SW_SPAN) + mask_ref[case][None]
        starts.append(ks)
        scores.append([s_loc.reshape(G_C * SW_SUB, SW_SPAN), _dot_nt(q, ck)])
    stats = [_softmax_parts(sc, extra=sink, exp=jnp.exp2) for sc in scores]
    for u, ((p_loc, p_ctx), l) in enumerate(stats):
        o = (_dot(p_loc.astype(BF16), v_ref[pl.ds(starts[u], SW_SPAN), :]) + _dot(p_ctx.astype(BF16), cv)) / l
        for g in range(G_C):
            o_ref[rows[u], g * HEAD_DIM:(g + 1) * HEAD_DIM] = o[g * SW_SUB:(g + 1) * SW_SUB].astype(o_ref.dtype)


def window_latent(z, cache_k, cache_v, layer, sink, batch, seq):
    assert seq % SW_TQ == 0 and seq >= SW_SPAN and SW_TQ % SW_SUB == 0
    n_ctx = cache_k.shape[2]
    ck = cache_k.reshape(batch, DEPTH, n_ctx, KVW_C)
    cv = cache_v.reshape(batch, DEPTH, n_ctx, KVW_C)
    nqt = seq // SW_TQ
    gw = G_C * HEAD_DIM
    ctx_spec = pl.BlockSpec((None, None, n_ctx, HEAD_DIM), lambda b, n, t: (b, layer, 0, n))
    return pl.pallas_call(
        functools.partial(_window_latent_kernel, seq=seq),
        grid=(batch, KV_C, nqt),
        in_specs=[
            pl.BlockSpec(memory_space=pltpu.SMEM),
            pl.BlockSpec((3, SW_SUB, SW_SPAN), lambda b, n, t: (0, 0, 0)),
            pl.BlockSpec((SW_TQ, gw), lambda b, n, t: (b * nqt + t, OFF_QC // gw + n)),
            pl.BlockSpec((seq, HEAD_DIM), lambda b, n, t: (b, OFF_KC // HEAD_DIM + n)),
            pl.BlockSpec((seq, HEAD_DIM), lambda b, n, t: (b, OFF_VC // HEAD_DIM + n)),
            ctx_spec, ctx_spec,
        ],
        out_specs=pl.BlockSpec((SW_TQ, gw), lambda b, n, t: (b * nqt + t, n)),
        out_shape=jax.ShapeDtypeStruct((batch * seq, MIX_C), BF16),
        compiler_params=_params("arbitrary", "arbitrary", "arbitrary"),
        name="window_latent",
    )(sink, _window_mask_table(), z, z, z, ck, cv)


def _prompt_row(tm, i):
    return 0


def kernel(x_prompt, x_sample, cache_a_k, cache_a_v, cache_b_k, cache_b_v, cache_c_k, cache_c_v, c, c_ctx,
           w_ada, b_ada, w_in, rpb_a, lam_b, subln_b, sink_c, w_branch, w_out, ln_g, ln_b, w_gate_up, w_down):
    bp, tp, _ = x_prompt.shape
    bs, ts, _ = x_sample.shape

    def sample_row(tm, i):
        return 1 + (i * tm) // ts

    streams = {
        "p": dict(x=x_prompt.reshape(bp * tp, D_MODEL), rowfn=_prompt_row),
        "s": dict(x=x_sample.reshape(bs * ts, D_MODEL), rowfn=sample_row),
    }
    cc = jnp.concatenate([c_ctx[None, :], c], axis=0)
    cc = jnp.pad(cc, ((0, (-cc.shape[0]) % 8), (0, 0)))
    mod = ada_modulation_all(cc, w_ada, b_ada)

    for st in streams.values():
        st["h"] = modulate_cast(st["x"], mod, 0, 0, st["rowfn"])

    new_kv = [[] for _ in range(6)]
    for l in range(DEPTH):
        w_qkv = cast_weight(w_in, l, 0, QKV_W)
        w_g = cast_weight(w_in, l, QKV_W)
        w_br = cast_weight(w_branch, l)
        w_o = cast_weight(w_out, l)
        w_gu = cast_weight(w_gate_up, l)
        w_dn = cast_weight(w_down, l)
        na_bias = _na_bias_table(rpb_a[l])

        sp = streams["p"]
        zp = matmul(sp["h"], w_qkv, F32, PROJ_TILE, PROJ_TILE, name="qkv_context")
        for lst, off, wdt, shp in (
                (new_kv[0], OFF_KA, MIX_A, (H_A, HEAD_DIM)), (new_kv[1], OFF_VA, MIX_A, (H_A, HEAD_DIM)),
                (new_kv[2], OFF_KB, MIX_B, (H_B, 2, DB_HALF)), (new_kv[3], OFF_VB, MIX_B, (H_B, HEAD_DIM)),
                (new_kv[4], OFF_KC, KVW_C, (KV_C, HEAD_DIM)), (new_kv[5], OFF_VC, KVW_C, (KV_C, HEAD_DIM))):
            lst.append(zp[:, off:off + wdt].reshape((bp, tp) + shp))
        op = context_attention(zp, tp, lam_b[l], subln_b[l], sink_c[l], l)
        sp["o"] = ((op, 0), (op, MIX_A // MIX_B), (op, (MIX_A + MIX_B) // MIX_C))

        ss = streams["s"]
        zs = qkv_latent(ss["h"], w_qkv, ts)
        oa = na_latent(zs, cache_a_k, cache_a_v, l, na_bias, bs, ts)
        ob = diff_latent(zs, cache_b_k, cache_b_v, l, lam_b[l], subln_b[l], bs, ts)
        oc = window_latent(zs, cache_c_k, cache_c_v, l, sink_c[l], bs, ts)
        ss["o"] = ((oa, 0), (ob, 0), (oc, 0))

        for st in streams.values():
            gates = matmul(st["h"], w_g, BF16, PROJ_TILE, PROJ_TILE, sigmoid=True, name="branch_gates")
            u = merge_branches(st["o"], gates, w_br)
            v = matmul_residual(u, w_o, st["x"], mod, l, 0, st["rowfn"], PROJ_TILE, PROJ_TILE, "mixer_out")
            st["x"], h2 = post_norm(v, mod, ln_g[l, 0], ln_b[l, 0], st["rowfn"], next_mod=(l, 1))
            a = swiglu_up(h2, w_gu)
            v = matmul_residual(a, w_dn, st["x"], mod, l, 1, st["rowfn"], DOWN_TILE, DOWN_TILE, "ffn_down")
            nxt = (l + 1, 0) if l + 1 < DEPTH else None
            st["x"], st["h"] = post_norm(v, mod, ln_g[l, 1], ln_b[l, 1], st["rowfn"], next_mod=nxt)

    outs = [jnp.stack(lst, axis=1) for lst in new_kv]
    return (streams["p"]["x"].reshape(bp, tp, D_MODEL), streams["s"]["x"].reshape(bs, ts, D_MODEL), *outs)
```

```python
import functools
import math

import numpy as np
import jax
import jax.numpy as jnp
from jax import lax
from jax.experimental import pallas as pl
from jax.experimental.pallas import tpu as pltpu

D_MODEL = 4096
DEPTH = 2
GRID_W = 64
HEAD_DIM = 128
H_A = 8
MIX_A = H_A * HEAD_DIM
NA_ROWS = 8
NA_COLS = 16
H_B = 8
DB_HALF = HEAD_DIM // 2
MIX_B = H_B * HEAD_DIM
H_C = 16
KV_C = 4
G_C = H_C // KV_C
MIX_C = H_C * HEAD_DIM
KVW_C = KV_C * HEAD_DIM
WINDOW_C = 128
N_BRANCH = 3
QKV_W = 3 * MIX_A + 3 * MIX_B + MIX_C + 2 * KVW_C
D_FF = 11008
ROPE_BASE = 10000.0
LN_EPS = 1e-5
RMS_EPS = 1e-5
ALPHA = (2 * DEPTH) ** 0.25
NEG_INF = -1e30
LOG2E = math.log2(math.e)
BF16 = jnp.bfloat16
F32 = jnp.float32

OFF_QA, OFF_KA, OFF_VA = 0, MIX_A, 2 * MIX_A
OFF_QB, OFF_KB, OFF_VB = 3 * MIX_A, 3 * MIX_A + MIX_B, 3 * MIX_A + 2 * MIX_B
OFF_QC = 3 * MIX_A + 3 * MIX_B
OFF_KC = OFF_QC + MIX_C
OFF_VC = OFF_KC + KVW_C

V7X_VMEM_LIMIT = 56 * 1024 * 1024
V7X_MXU_COLS = 256

PROJ_TILE = 1024
SWIGLU_TM, SWIGLU_TN = 4096, V7X_MXU_COLS
DOWN_TILE = 512
MODULATE_TM, NORM_TM = 512, 512

NA_QROWS = 16
NA_HEADS = 4
SW_TQ = 2048
SW_SUB = WINDOW_C
SW_SPAN = 3 * SW_SUB
DIFF_TQ = 128
DIFF_NSUB = 8


def _params(*sem):
    return pltpu.CompilerParams(dimension_semantics=sem, vmem_limit_bytes=V7X_VMEM_LIMIT)


def _dot(a, b):
    return jnp.dot(a, b, preferred_element_type=F32)


def _dot_nt(a, b):
    return lax.dot_general(a, b, (((1,), (1,)), ((), ())), preferred_element_type=F32)


def _sigmoid(x):
    return 0.5 * jnp.tanh(0.5 * x) + 0.5


def _cast_kernel(w_ref, o_ref):
    o_ref[...] = w_ref[...].astype(o_ref.dtype)


def cast_weight(w, layer, col0=0, ncols=None):
    _, r, c = w.shape
    ncols = c - col0 if ncols is None else ncols
    tc = 512 if r <= D_MODEL else HEAD_DIM
    assert col0 % tc == 0 and ncols % tc == 0
    return pl.pallas_call(
        _cast_kernel,
        grid=(ncols // tc,),
        in_specs=[pl.BlockSpec((None, r, tc), lambda j: (layer, 0, col0 // tc + j))],
        out_specs=pl.BlockSpec((r, tc), lambda j: (0, j)),
        out_shape=jax.ShapeDtypeStruct((r, ncols), BF16),
        compiler_params=_params("arbitrary"),
        name="cast_weight",
    )(w)


def _ada_kernel(c_ref, w_ref, b_ref, o_ref):
    cv = c_ref[...]
    s = (cv * _sigmoid(cv)).astype(BF16)
    o_ref[...] = _dot(s, w_ref[...].astype(BF16)) + b_ref[...]


def ada_modulation_all(cc, w_ada, b_ada):
    r = cc.shape[0]
    n = 6 * D_MODEL
    tn = 512
    out = pl.pallas_call(
        _ada_kernel,
        grid=(DEPTH, n // tn),
        in_specs=[
            pl.BlockSpec((r, D_MODEL), lambda l, j: (0, 0)),
            pl.BlockSpec((None, D_MODEL, tn), lambda l, j: (l, 0, j)),
            pl.BlockSpec((None, 1, tn), lambda l, j: (l, 0, j)),
        ],
        out_specs=pl.BlockSpec((None, r, tn), lambda l, j: (l, 0, j)),
        out_shape=jax.ShapeDtypeStruct((DEPTH, r, n), F32),
        compiler_params=_params("arbitrary", "arbitrary"),
        name="ada_modulation",
    )(cc, w_ada, b_ada.reshape(DEPTH, 1, n))
    return out.reshape(DEPTH, r, 6, 1, D_MODEL)


def _mod_spec(layer, chunk, rowfn):
    return pl.BlockSpec((None, None, None, 1, D_MODEL), lambda i: (layer, rowfn(i), chunk, 0, 0))


def _modulate_kernel(x_ref, sc_ref, sh_ref, o_ref):
    o_ref[...] = (x_ref[...] * (1.0 + sc_ref[...]) + sh_ref[...]).astype(BF16)


def modulate_cast(x, mod, layer, idx, rowfn_of_tile, tm=MODULATE_TM):
    m = x.shape[0]
    rowfn = functools.partial(rowfn_of_tile, tm)
    return pl.pallas_call(
        _modulate_kernel,
        grid=(m // tm,),
        in_specs=[
            pl.BlockSpec((tm, D_MODEL), lambda i: (i, 0)),
            _mod_spec(layer, 3 * idx + 1, rowfn),
            _mod_spec(layer, 3 * idx, rowfn),
        ],
        out_specs=pl.BlockSpec((tm, D_MODEL), lambda i: (i, 0)),
        out_shape=jax.ShapeDtypeStruct((m, D_MODEL), BF16),
        compiler_params=_params("arbitrary"),
        name="modulate",
    )(x, mod, mod)


def _ln_core(v_ref, g_ref, b_ref):
    v = v_ref[...]
    mu = jnp.mean(v, axis=-1, keepdims=True)
    d = v - mu
    var = jnp.mean(d * d, axis=-1, keepdims=True)
    return d * lax.rsqrt(var + LN_EPS) * g_ref[...] + b_ref[...]


def _ln_mod_kernel(v_ref, g_ref, b_ref, sc_ref, sh_ref, xo_ref, ho_ref):
    xn = _ln_core(v_ref, g_ref, b_ref)
    xo_ref[...] = xn
    ho_ref[...] = (xn * (1.0 + sc_ref[...]) + sh_ref[...]).astype(BF16)


def _ln_kernel(v_ref, g_ref, b_ref, xo_ref):
    xo_ref[...] = _ln_core(v_ref, g_ref, b_ref)


def post_norm(v, mod, ln_g, ln_b, rowfn_of_tile, next_mod=None, tm=NORM_TM):
    m = v.shape[0]
    rowfn = functools.partial(rowfn_of_tile, tm)
    row_spec = pl.BlockSpec((tm, D_MODEL), lambda i: (i, 0))
    vec_spec = pl.BlockSpec((1, D_MODEL), lambda i: (0, 0))
    in_specs = [row_spec, vec_spec, vec_spec]
    args = [v, ln_g.reshape(1, D_MODEL), ln_b.reshape(1, D_MODEL)]
    if next_mod is None:
        return pl.pallas_call(
            _ln_kernel, grid=(m // tm,), in_specs=in_specs, out_specs=row_spec,
            out_shape=jax.ShapeDtypeStruct((m, D_MODEL), F32),
            compiler_params=_params("arbitrary"), name="post_norm",
        )(*args), None
    nl, nidx = next_mod
    in_specs += [_mod_spec(nl, 3 * nidx + 1, rowfn), _mod_spec(nl, 3 * nidx, rowfn)]
    args += [mod, mod]
    return pl.pallas_call(
        _ln_mod_kernel, grid=(m // tm,), in_specs=in_specs, out_specs=[row_spec, row_spec],
        out_shape=[jax.ShapeDtypeStruct((m, D_MODEL), F32), jax.ShapeDtypeStruct((m, D_MODEL), BF16)],
        compiler_params=_params("arbitrary"), name="post_norm_modulate",
    )(*args)


def _mm_kernel(x_ref, w_ref, o_ref, *, sigmoid):
    acc = _dot(x_ref[...], w_ref[...])
    if sigmoid:
        acc = _sigmoid(acc)
    o_ref[...] = acc.astype(o_ref.dtype)


def matmul(x, w, out_dtype, tm, tn, sigmoid=False, name="matmul"):
    m, k = x.shape
    n = w.shape[1]
    return pl.pallas_call(
        functools.partial(_mm_kernel, sigmoid=sigmoid),
        grid=(m // tm, n // tn),
        in_specs=[pl.BlockSpec((tm, k), lambda i, j: (i, 0)), pl.BlockSpec((k, tn), lambda i, j: (0, j))],
        out_specs=pl.BlockSpec((tm, tn), lambda i, j: (i, j)),
        out_shape=jax.ShapeDtypeStruct((m, n), out_dtype),
        compiler_params=_params("arbitrary", "arbitrary"),
        name=name,
    )(x, w)


def _mm_residual_kernel(x_ref, w_ref, res_ref, gate_ref, o_ref):
    o_ref[...] = ALPHA * res_ref[...] + gate_ref[...] * _dot(x_ref[...], w_ref[...])


def matmul_residual(x, w, res, mod, layer, idx, rowfn_of_tile, tm, tn, name):
    m, k = x.shape
    n = w.shape[1]
    rowfn = functools.partial(rowfn_of_tile, tm)
    gate_spec = pl.BlockSpec((None, None, None, 1, tn), lambda i, j: (layer, rowfn(i), 3 * idx + 2, 0, j))
    tile_spec = pl.BlockSpec((tm, tn), lambda i, j: (i, j))
    return pl.pallas_call(
        _mm_residual_kernel,
        grid=(m // tm, n // tn),
        in_specs=[pl.BlockSpec((tm, k), lambda i, j: (i, 0)), pl.BlockSpec((k, tn), lambda i, j: (0, j)),
                  tile_spec, gate_spec],
        out_specs=tile_spec,
        out_shape=jax.ShapeDtypeStruct((m, n), F32),
        compiler_params=_params("arbitrary", "arbitrary"),
        name=name,
    )(x, w, res, mod)


def _rope_tables(t, width):
    half, quarter = width // 2, width // 4
    lane = np.arange(HEAD_DIM)
    e = lane % width
    pos = np.arange(t)
    p = np.where((e < half)[None, :], (pos // GRID_W)[:, None], (pos % GRID_W)[:, None]).astype(np.float64)
    inv = np.exp(-math.log(ROPE_BASE) * (e % quarter).astype(np.float64) / quarter)
    ang = (p.astype(np.float32) * inv.astype(np.float32)[None, :]).astype(np.float64)
    first = ((e % half) < quarter)[None, :]
    cos, sin = np.cos(ang), np.sin(ang)
    return jnp.asarray(cos, F32), jnp.asarray(np.where(first, -sin, sin), F32)


def _rope(x, cos, ssin, quarter):
    partner = lax.broadcasted_iota(jnp.int32, x.shape, 1) ^ quarter
    return x * cos + jnp.take_along_axis(x, partner, axis=1) * ssin


def _qkv_latent_kernel(x_ref, w_ref, cb_ref, sb_ref, cc_ref, sc_ref, o_ref, *, tn):
    j = pl.program_id(1)
    acc = _dot(x_ref[...], w_ref[...])
    nchunk = tn // HEAD_DIM
    tiles_qb = (OFF_QB // tn,)
    tiles_kb = (OFF_KB // tn,)
    tiles_qc = tuple(range(OFF_QC // tn, OFF_KC // tn))
    tile_kcvc = OFF_KC // tn

    def is_in(tiles):
        c = j == tiles[0]
        for tt in tiles[1:]:
            c = jnp.logical_or(c, j == tt)
        return c

    def store(fn_of_chunk):
        for ch in range(nchunk):
            sl = slice(ch * HEAD_DIM, (ch + 1) * HEAD_DIM)
            o_ref[:, sl] = fn_of_chunk(ch, acc[:, sl]).astype(o_ref.dtype)

    rope_b = lambda x: _rope(x, cb_ref[...], sb_ref[...], DB_HALF // 4)
    rope_c = lambda x: _rope(x, cc_ref[...], sc_ref[...], HEAD_DIM // 4)
    tiles_qa = tuple(range(OFF_QA // tn, OFF_KA // tn))
    special = is_in(tiles_qa + tiles_qb + tiles_kb + tiles_qc + (tile_kcvc,))

    @pl.when(jnp.logical_not(special))
    def _():
        o_ref[...] = acc.astype(o_ref.dtype)

    @pl.when(is_in(tiles_qa))
    def _():
        o_ref[...] = (acc * (HEAD_DIM ** -0.5 * LOG2E)).astype(o_ref.dtype)

    @pl.when(is_in(tiles_qb))
    def _():
        store(lambda ch, x: rope_b(x) * (DB_HALF ** -0.5 * LOG2E))

    @pl.when(is_in(tiles_kb))
    def _():
        store(lambda ch, x: rope_b(x))

    @pl.when(is_in(tiles_qc))
    def _():
        store(lambda ch, x: rope_c(x) * (HEAD_DIM ** -0.5 * LOG2E))

    @pl.when(j == tile_kcvc)
    def _():
        store(lambda ch, x: rope_c(x) if ch < KVW_C // HEAD_DIM else x)


def qkv_latent(h, w_qkv, seq, tm=PROJ_TILE, tn=PROJ_TILE):
    m, k = h.shape
    assert tn == MIX_A and OFF_KC % tn == 0 and (OFF_VC - OFF_KC) == KVW_C and seq % tm == 0
    tabs = _rope_tables(seq, DB_HALF) + _rope_tables(seq, HEAD_DIM)
    per_seq = seq // tm
    tab_spec = pl.BlockSpec((tm, HEAD_DIM), lambda i, j: (i % per_seq, 0))
    return pl.pallas_call(
        functools.partial(_qkv_latent_kernel, tn=tn),
        grid=(m // tm, QKV_W // tn),
        in_specs=[pl.BlockSpec((tm, k), lambda i, j: (i, 0)), pl.BlockSpec((k, tn), lambda i, j: (0, j))]
        + [tab_spec] * len(tabs),
        out_specs=pl.BlockSpec((tm, tn), lambda i, j: (i, j)),
        out_shape=jax.ShapeDtypeStruct((m, QKV_W), BF16),
        compiler_params=_params("arbitrary", "arbitrary"),
        name="qkv_latent",
    )(h, w_qkv, *tabs)


def _merge_kernel(oa_ref, ob_ref, oc_ref, wa_ref, wb_ref, wc_ref, ga_ref, gb_ref, gc_ref, o_ref):
    y = ga_ref[...].astype(F32) * _dot(oa_ref[...], wa_ref[...])
    y += gb_ref[...].astype(F32) * _dot(ob_ref[...], wb_ref[...])
    y += gc_ref[...].astype(F32) * _dot(oc_ref[...], wc_ref[...])
    o_ref[...] = y.astype(o_ref.dtype)


def merge_branches(o_parts, gates, w_branch, tm=PROJ_TILE, tn=PROJ_TILE):
    (oa, ja), (ob, jb), (oc, jc) = o_parts
    m = gates.shape[0]
    nj = D_MODEL // tn
    return pl.pallas_call(
        _merge_kernel,
        grid=(m // tm, nj),
        in_specs=[
            pl.BlockSpec((tm, MIX_A), lambda i, j: (i, ja)),
            pl.BlockSpec((tm, MIX_B), lambda i, j: (i, jb)),
            pl.BlockSpec((tm, MIX_C), lambda i, j: (i, jc)),
            pl.BlockSpec((MIX_A, tn), lambda i, j: (0, j)),
            pl.BlockSpec((MIX_B, tn), lambda i, j: (MIX_A // MIX_B, j)),
            pl.BlockSpec((MIX_C, tn), lambda i, j: ((MIX_A + MIX_B) // MIX_C, j)),
            pl.BlockSpec((tm, tn), lambda i, j: (i, j)),
            pl.BlockSpec((tm, tn), lambda i, j: (i, nj + j)),
            pl.BlockSpec((tm, tn), lambda i, j: (i, 2 * nj + j)),
        ],
        out_specs=pl.BlockSpec((tm, tn), lambda i, j: (i, j)),
        out_shape=jax.ShapeDtypeStruct((m, D_MODEL), BF16),
        compiler_params=_params("arbitrary", "arbitrary"),
        name="merge_branches",
    )(oa, ob, oc, w_branch, w_branch, w_branch, gates, gates, gates)


def _swiglu_kernel(x_ref, wg_ref, wu_ref, o_ref):
    x = x_ref[...]
    g = _dot(x, wg_ref[...])
    u = _dot(x, wu_ref[...])
    o_ref[...] = (g * _sigmoid(g) * u).astype(o_ref.dtype)


def swiglu_up(h, w_gate_up, tm=SWIGLU_TM, tn=SWIGLU_TN):
    m, k = h.shape
    nj = D_FF // tn
    return pl.pallas_call(
        _swiglu_kernel,
        grid=(m // tm, nj),
        in_specs=[
            pl.BlockSpec((tm, k), lambda i, j: (i, 0), pipeline_mode=pl.Buffered(1)),
            pl.BlockSpec((k, tn), lambda i, j: (0, j)),
            pl.BlockSpec((k, tn), lambda i, j: (0, nj + j)),
        ],
        out_specs=pl.BlockSpec((tm, tn), lambda i, j: (i, j)),
        out_shape=jax.ShapeDtypeStruct((m, D_FF), BF16),
        compiler_params=_params("arbitrary", "arbitrary"),
        name="swiglu_up",
    )(h, w_gate_up, w_gate_up)


def _diff_lambda(lam_ref, layer):
    lam_init = 0.8 - 0.6 * math.exp(-0.3 * layer)
    lf = lam_ref[...]
    a = jnp.sum(lf[0:1, :] * lf[1:2, :], axis=-1, keepdims=True)
    b = jnp.sum(lf[2:3, :] * lf[3:4, :], axis=-1, keepdims=True)
    return jnp.exp(a) - jnp.exp(b) + lam_init, lam_init


def _diff_finish(o, gain_ref, lam_init):
    o = o * lax.rsqrt(jnp.mean(o * o, axis=-1, keepdims=True) + RMS_EPS)
    return o * gain_ref[...] * (1.0 - lam_init)


def _split_halves(q):
    lane = lax.broadcasted_iota(jnp.int32, (1, HEAD_DIM), 1)
    zero = jnp.zeros_like(q)
    return jnp.where(lane < DB_HALF, q, zero), jnp.where(lane >= DB_HALF, q, zero)


def _sink_column(sink_ref, n, t):
    row = lax.broadcasted_iota(jnp.int32, (G_C * t, 1), 0)
    col = jnp.full((G_C * t, 1), sink_ref[n * G_C + G_C - 1], F32)
    for g in range(G_C - 2, -1, -1):
        col = jnp.where(row < (g + 1) * t, sink_ref[n * G_C + g], col)
    return col


def _softmax_parts(parts, extra=None, exp=jnp.exp):
    m = functools.reduce(jnp.maximum, [jnp.max(s, axis=-1, keepdims=True) for s in parts])
    if extra is not None:
        m = jnp.maximum(m, extra)
    ps = [exp(s - m) for s in parts]
    l = functools.reduce(jnp.add, [jnp.sum(p, axis=-1, keepdims=True) for p in ps])
    if extra is not None:
        l = l + exp(extra - m)
    return ps, l


def _context_attn_kernel(sink_ref, z_ref, lam_ref, gain_ref, o_ref, *, layer):
    scale = HEAD_DIM ** -0.5

    def col(off, h, width=HEAD_DIM):
        return z_ref[:, off + h * width: off + (h + 1) * width]

    for h in range(H_A):
        q, k, v = col(OFF_QA, h).astype(BF16), col(OFF_KA, h).astype(BF16), col(OFF_VA, h).astype(BF16)
        (p,), l = _softmax_parts([_dot_nt(q, k) * scale])
        o = _dot((p / l).astype(BF16), v)
        o_ref[:, h * HEAD_DIM:(h + 1) * HEAD_DIM] = o.astype(o_ref.dtype)

    lam, lam_init = _diff_lambda(lam_ref, layer)
    for h in range(H_B):
        q, k, v = col(OFF_QB, h).astype(BF16), col(OFF_KB, h).astype(BF16), col(OFF_VB, h).astype(BF16)
        q1, q2 = _split_halves(q)
        (p0,), l0 = _softmax_parts([_dot_nt(q1, k) * (DB_HALF ** -0.5)])
        (p1,), l1 = _softmax_parts([_dot_nt(q2, k) * (DB_HALF ** -0.5)])
        w = p0 / l0 - lam * (p1 / l1)
        o = _diff_finish(_dot(w.astype(BF16), v), gain_ref, lam_init)
        o_ref[:, MIX_A + h * HEAD_DIM: MIX_A + (h + 1) * HEAD_DIM] = o.astype(o_ref.dtype)

    t = z_ref.shape[0]
    for n in range(KV_C):
        q = jnp.concatenate([col(OFF_QC, n * G_C + g) for g in range(G_C)], axis=0).astype(BF16)
        k, v = col(OFF_KC, n).astype(BF16), col(OFF_VC, n).astype(BF16)
        sink = _sink_column(sink_ref, n, t)
        (p,), l = _softmax_parts([_dot_nt(q, k) * scale], extra=sink)
        o = _dot((p / l).astype(BF16), v)
        for g in range(G_C):
            c0 = MIX_A + MIX_B + (n * G_C + g) * HEAD_DIM
            o_ref[:, c0:c0 + HEAD_DIM] = o[g * t:(g + 1) * t].astype(o_ref.dtype)


def context_attention(z, seq, lam, gain, sink, layer):
    m = z.shape[0]
    return pl.pallas_call(
        functools.partial(_context_attn_kernel, layer=layer),
        grid=(m // seq,),
        in_specs=[
            pl.BlockSpec(memory_space=pltpu.SMEM),
            pl.BlockSpec((seq, QKV_W), lambda b: (b, 0)),
            pl.BlockSpec((4, DB_HALF), lambda b: (0, 0)),
            pl.BlockSpec((1, HEAD_DIM), lambda b: (0, 0)),
        ],
        out_specs=pl.BlockSpec((seq, D_MODEL), lambda b: (b, 0)),
        out_shape=jax.ShapeDtypeStruct((m, D_MODEL), BF16),
        compiler_params=_params("arbitrary"),
        name="context_attention",
    )(sink, z, lam, gain.reshape(1, HEAD_DIM))


def _na_bias_table(rpb_l):
    pad = GRID_W
    rp = jnp.pad(rpb_l.astype(F32) * LOG2E, ((0, 0), (0, 0), (pad, pad)))
    first = pad + NA_COLS - 1
    tz = jnp.stack([rp[:, :, first - qc:first - qc + GRID_W] for qc in range(GRID_W)], axis=2)
    qc, kc = np.arange(GRID_W)[:, None], np.arange(GRID_W)[None, :]
    ws = np.clip(qc - NA_COLS // 2, 0, GRID_W - NA_COLS)
    valid = (kc >= ws) & (kc < ws + NA_COLS)
    tz = jnp.where(jnp.asarray(valid)[None, None], tz, NEG_INF)
    return jnp.stack([jnp.concatenate([tz[:, e0 + j] for j in range(NA_ROWS)], axis=-1)
                      for e0 in range(NA_ROWS)], axis=1)


def _na_latent_kernel(q_ref, k_ref, v_ref, ck_ref, cv_ref, bias_ref, o_ref, *, rows):
    rb = pl.program_id(2)
    span = NA_ROWS * GRID_W
    starts, biases = [], []
    for r in range(NA_QROWS):
        r_abs = rb * NA_QROWS + r
        rs = jnp.clip(r_abs - NA_ROWS // 2, 0, rows - NA_ROWS)
        starts.append(pl.multiple_of(rs * GRID_W, GRID_W))
        biases.append(rs - r_abs + NA_ROWS - 1)
    heads = [slice(hh * HEAD_DIM, (hh + 1) * HEAD_DIM) for hh in range(NA_HEADS)]
    scores = []
    for hh, cols in enumerate(heads):
        q = q_ref[:, cols]
        s_loc = [_dot_nt(q[r * GRID_W:(r + 1) * GRID_W], k_ref[pl.ds(starts[r], span), cols]) + bias_ref[hh, biases[r]]
                 for r in range(NA_QROWS)]
        scores.append([jnp.concatenate(s_loc, axis=0), _dot_nt(q, ck_ref[:, cols].astype(BF16))])
    stats = [_softmax_parts(sc, exp=jnp.exp2) for sc in scores]
    for cols, ((p_loc, p_ctx), l) in zip(heads, stats):
        p_loc = p_loc.astype(BF16)
        o_loc = [_dot(p_loc[r * GRID_W:(r + 1) * GRID_W], v_ref[pl.ds(starts[r], span), cols]) for r in range(NA_QROWS)]
        o = jnp.concatenate(o_loc, axis=0) + _dot(p_ctx.astype(BF16), cv_ref[:, cols].astype(BF16))
        o_ref[:, cols] = (o / l).astype(o_ref.dtype)


def na_latent(z, cache_k, cache_v, layer, bias, batch, seq):
    rows = seq // GRID_W
    assert rows % NA_QROWS == 0 and rows >= NA_ROWS and H_A % NA_HEADS == 0
    nrb = rows // NA_QROWS
    tq = NA_QROWS * GRID_W
    hw = NA_HEADS * HEAD_DIM
    n_ctx = cache_k.shape[2]
    ck = cache_k.reshape(batch, DEPTH, n_ctx, MIX_A)
    cv = cache_v.reshape(batch, DEPTH, n_ctx, MIX_A)
    ctx_spec = pl.BlockSpec((None, None, n_ctx, hw), lambda b, h, rb: (b, layer, 0, h))
    return pl.pallas_call(
        functools.partial(_na_latent_kernel, rows=rows),
        grid=(batch, H_A // NA_HEADS, nrb),
        in_specs=[
            pl.BlockSpec((tq, hw), lambda b, h, rb: (b * nrb + rb, OFF_QA // hw + h)),
            pl.BlockSpec((seq, hw), lambda b, h, rb: (b, OFF_KA // hw + h)),
            pl.BlockSpec((seq, hw), lambda b, h, rb: (b, OFF_VA // hw + h)),
            ctx_spec, ctx_spec,
            pl.BlockSpec((NA_HEADS, NA_ROWS, GRID_W, NA_ROWS * GRID_W), lambda b, h, rb: (h, 0, 0, 0)),
        ],
        out_specs=pl.BlockSpec((tq, hw), lambda b, h, rb: (b * nrb + rb, h)),
        out_shape=jax.ShapeDtypeStruct((batch * seq, MIX_A), BF16),
        compiler_params=_params("arbitrary", "arbitrary", "arbitrary"),
        name="na_latent",
    )(z, z, z, ck, cv, bias)


def _diff_latent_kernel(lam_ref, gain_ref, q_ref, k_ref, v_ref, ck_ref, cv_ref, o_ref, *, layer):
    lam, lam_init = _diff_lambda(lam_ref, layer)
    k = k_ref[...]
    ck = ck_ref[...].astype(BF16)
    cv = cv_ref[...].astype(BF16)
    nsub = q_ref.shape[0] // DIFF_TQ
    rows = [slice(u * DIFF_TQ, (u + 1) * DIFF_TQ) for u in range(nsub)]
    scores = [[[_dot_nt(qh, k), _dot_nt(qh, ck)] for qh in _split_halves(q_ref[r, :])] for r in rows]
    stats = [[_softmax_parts(sc, exp=jnp.exp2) for sc in su] for su in scores]
    for r, ((p0, l0), (p1, l1)) in zip(rows, stats):
        c = lam * l0 / l1
        wl = (p0[0] - p1[0] * c).astype(BF16)
        wc = (p0[1] - p1[1] * c).astype(BF16)
        o = (_dot(wl, v_ref[...]) + _dot(wc, cv)) / l0
        o_ref[r, :] = _diff_finish(o, gain_ref, lam_init).astype(o_ref.dtype)


def diff_latent(z, cache_k, cache_v, layer, lam, gain, batch, seq):
    tq = DIFF_TQ * DIFF_NSUB
    n_ctx = cache_k.shape[2]
    ck = cache_k.reshape(batch, DEPTH, n_ctx, MIX_B)
    cv = cache_v.reshape(batch, DEPTH, n_ctx, MIX_B)
    nqt = seq // tq
    ctx_spec = pl.BlockSpec((None, None, n_ctx, HEAD_DIM), lambda b, h, t: (b, layer, 0, h))
    return pl.pallas_call(
        functools.partial(_diff_latent_kernel, layer=layer),
        grid=(batch, H_B, nqt),
        in_specs=[
            pl.BlockSpec((4, DB_HALF), lambda b, h, t: (0, 0)),
            pl.BlockSpec((1, HEAD_DIM), lambda b, h, t: (0, 0)),
            pl.BlockSpec((tq, HEAD_DIM), lambda b, h, t: (b * nqt + t, OFF_QB // HEAD_DIM + h)),
            pl.BlockSpec((seq, HEAD_DIM), lambda b, h, t: (b, OFF_KB // HEAD_DIM + h)),
            pl.BlockSpec((seq, HEAD_DIM), lambda b, h, t: (b, OFF_VB // HEAD_DIM + h)),
            ctx_spec, ctx_spec,
        ],
        out_specs=pl.BlockSpec((tq, HEAD_DIM), lambda b, h, t: (b * nqt + t, h)),
        out_shape=jax.ShapeDtypeStruct((batch * seq, MIX_B), BF16),
        compiler_params=_params("arbitrary", "arbitrary", "arbitrary"),
        name="diff_latent",
    )(lam, gain.reshape(1, HEAD_DIM), z, z, z, ck, cv)


def _window_mask_table():
    i, j = np.arange(SW_SUB)[:, None], np.arange(SW_SPAN)[None, :]
    tabs = [np.where(np.abs(c * WINDOW_C + i - j) <= WINDOW_C, 0.0, NEG_INF) for c in range(3)]
    return jnp.asarray(np.stack(tabs), F32)


def _window_latent_kernel(sink_ref, mask_ref, q_ref, k_ref, v_ref, ck_ref, cv_ref, o_ref, *, seq):
    n = pl.program_id(1)
    t0 = pl.program_id(2) * SW_TQ
    ck = ck_ref[...].astype(BF16)
    cv = cv_ref[...].astype(BF16)
    sink = _sink_column(sink_ref, n, SW_SUB) * LOG2E
    nsub = SW_TQ // SW_SUB
    rows = [slice(u * SW_SUB, (u + 1) * SW_SUB) for u in range(nsub)]
    starts, scores = [], []
    for u in range(nsub):
        tu = t0 + u * SW_SUB
        ks = pl.multiple_of(jnp.clip(tu - WINDOW_C, 0, seq - SW_SPAN), WINDOW_C)
        case = (tu - ks) // WINDOW_C
        q = jnp.concatenate([q_ref[rows[u], g * HEAD_DIM:(g + 1) * HEAD_DIM] for g in range(G_C)], axis=0)
        s_loc = _dot_nt(q, k_ref[pl.ds(ks, SW_SPAN), :]).reshape(G_C, SW_SUB, SW_SPAN) + mask_ref[case][None]
        starts.append(ks)
        scores.append([s_loc.reshape(G_C * SW_SUB, SW_SPAN), _dot_nt(q, ck)])
    stats = [_softmax_parts(sc, extra=sink, exp=jnp.exp2) for sc in scores]
    for u, ((p_loc, p_ctx), l) in enumerate(stats):
        o = (_dot(p_loc.astype(BF16), v_ref[pl.ds(starts[u], SW_SPAN), :]) + _dot(p_ctx.astype(BF16), cv)) / l
        for g in range(G_C):
            o_ref[rows[u], g * HEAD_DIM:(g + 1) * HEAD_DIM] = o[g * SW_SUB:(g + 1) * SW_SUB].astype(o_ref.dtype)


def window_latent(z, cache_k, cache_v, layer, sink, batch, seq):
    assert seq % SW_TQ == 0 and seq >= SW_SPAN and SW_TQ % SW_SUB == 0
    n_ctx = cache_k.shape[2]
    ck = cache_k.reshape(batch, DEPTH, n_ctx, KVW_C)
    cv = cache_v.reshape(batch, DEPTH, n_ctx, KVW_C)
    nqt = seq // SW_TQ
    gw = G_C * HEAD_DIM
    ctx_spec = pl.BlockSpec((None, None, n_ctx, HEAD_DIM), lambda b, n, t: (b, layer, 0, n))
    return pl.pallas_call(
        functools.partial(_window_latent_kernel, seq=seq),
        grid=(batch, KV_C, nqt),
        in_specs=[
            pl.BlockSpec(memory_space=pltpu.SMEM),
            pl.BlockSpec((3, SW_SUB, SW_SPAN), lambda b, n, t: (0, 0, 0)),
            pl.BlockSpec((SW_TQ, gw), lambda b, n, t: (b * nqt + t, OFF_QC // gw + n)),
            pl.BlockSpec((seq, HEAD_DIM), lambda b, n, t: (b, OFF_KC // HEAD_DIM + n)),
            pl.BlockSpec((seq, HEAD_DIM), lambda b, n, t: (b, OFF_VC // HEAD_DIM + n)),
            ctx_spec, ctx_spec,
        ],
        out_specs=pl.BlockSpec((SW_TQ, gw), lambda b, n, t: (b * nqt + t, n)),
        out_shape=jax.ShapeDtypeStruct((batch * seq, MIX_C), BF16),
        compiler_params=_params("arbitrary", "arbitrary", "arbitrary"),
        name="window_latent",
    )(sink, _window_mask_table(), z, z, z, ck, cv)


def _prompt_row(tm, i):
    return 0


def kernel(x_prompt, x_sample, cache_a_k, cache_a_v, cache_b_k, cache_b_v, cache_c_k, cache_c_v, c, c_ctx,
           w_ada, b_ada, w_in, rpb_a, lam_b, subln_b, sink_c, w_branch, w_out, ln_g, ln_b, w_gate_up, w_down):
    bp, tp, _ = x_prompt.shape
    bs, ts, _ = x_sample.shape

    def sample_row(tm, i):
        return 1 + (i * tm) // ts

    streams = {
        "p": dict(x=x_prompt.reshape(bp * tp, D_MODEL), rowfn=_prompt_row),
        "s": dict(x=x_sample.reshape(bs * ts, D_MODEL), rowfn=sample_row),
    }
    cc = jnp.concatenate([c_ctx[None, :], c], axis=0)
    cc = jnp.pad(cc, ((0, (-cc.shape[0]) % 8), (0, 0)))
    mod = ada_modulation_all(cc, w_ada, b_ada)

    for st in streams.values():
        st["h"] = modulate_cast(st["x"], mod, 0, 0, st["rowfn"])

    new_kv = [[] for _ in range(6)]
    for l in range(DEPTH):
        w_qkv = cast_weight(w_in, l, 0, QKV_W)
        w_g = cast_weight(w_in, l, QKV_W)
        w_br = cast_weight(w_branch, l)
        w_o = cast_weight(w_out, l)
        w_gu = cast_weight(w_gate_up, l)
        w_dn = cast_weight(w_down, l)
        na_bias = _na_bias_table(rpb_a[l])

        sp = streams["p"]
        zp = matmul(sp["h"], w_qkv, F32, PROJ_TILE, PROJ_TILE, name="qkv_context")
        for lst, off, wdt, shp in (
                (new_kv[0], OFF_KA, MIX_A, (H_A, HEAD_DIM)), (new_kv[1], OFF_VA, MIX_A, (H_A, HEAD_DIM)),
                (new_kv[2], OFF_KB, MIX_B, (H_B, 2, DB_HALF)), (new_kv[3], OFF_VB, MIX_B, (H_B, HEAD_DIM)),
                (new_kv[4], OFF_KC, KVW_C, (KV_C, HEAD_DIM)), (new_kv[5], OFF_VC, KVW_C, (KV_C, HEAD_DIM))):
            lst.append(zp[:, off:off + wdt].reshape((bp, tp) + shp))
        op = context_attention(zp, tp, lam_b[l], subln_b[l], sink_c[l], l)
        sp["o"] = ((op, 0), (op, MIX_A // MIX_B), (op, (MIX_A + MIX_B) // MIX_C))

        ss = streams["s"]
        zs = qkv_latent(ss["h"], w_qkv, ts)
        oa = na_latent(zs, cache_a_k, cache_a_v, l, na_bias, bs, ts)
        ob = diff_latent(zs, cache_b_k, cache_b_v, l, lam_b[l], subln_b[l], bs, ts)
        oc = window_latent(zs, cache_c_k, cache_c_v, l, sink_c[l], bs, ts)
        ss["o"] = ((oa, 0), (ob, 0), (oc, 0))

        for st in streams.values():
            gates = matmul(st["h"], w_g, BF16, PROJ_TILE, PROJ_TILE, sigmoid=True, name="branch_gates")
            u = merge_branches(st["o"], gates, w_br)
            v = matmul_residual(u, w_o, st["x"], mod, l, 0, st["rowfn"], PROJ_TILE, PROJ_TILE, "mixer_out")
            st["x"], h2 = post_norm(v, mod, ln_g[l, 0], ln_b[l, 0], st["rowfn"], next_mod=(l, 1))
            a = swiglu_up(h2, w_gu)
            v = matmul_residual(a, w_dn, st["x"], mod, l, 1, st["rowfn"], DOWN_TILE, DOWN_TILE, "ffn_down")
            nxt = (l + 1, 0) if l + 1 < DEPTH else None
            st["x"], st["h"] = post_norm(v, mod, ln_g[l, 1], ln_b[l, 1], st["rowfn"], next_mod=nxt)

    outs = [jnp.stack(lst, axis=1) for lst in new_kv]
    return (streams["p"]["x"].reshape(bp, tp, D_MODEL), streams["s"]["x"].reshape(bs, ts, D_MODEL), *outs)
```

```python
import functools
import math

import numpy as np
import jax
import jax.numpy as jnp
from jax import lax
from jax.experimental import pallas as pl
from jax.experimental.pallas import tpu as pltpu

D_MODEL = 4096
DEPTH = 2
GRID_W = 64
HEAD_DIM = 128
H_A = 8
MIX_A = H_A * HEAD_DIM
NA_ROWS = 8
NA_COLS = 16
H_B = 8
DB_HALF = HEAD_DIM // 2
MIX_B = H_B * HEAD_DIM
H_C = 16
KV_C = 4
G_C = H_C // KV_C
MIX_C = H_C * HEAD_DIM
KVW_C = KV_C * HEAD_DIM
WINDOW_C = 128
N_BRANCH = 3
QKV_W = 3 * MIX_A + 3 * MIX_B + MIX_C + 2 * KVW_C
D_FF = 11008
ROPE_BASE = 10000.0
LN_EPS = 1e-5
RMS_EPS = 1e-5
ALPHA = (2 * DEPTH) ** 0.25
NEG_INF = -1e30
LOG2E = math.log2(math.e)
BF16 = jnp.bfloat16
F32 = jnp.float32

OFF_QA, OFF_KA, OFF_VA = 0, MIX_A, 2 * MIX_A
OFF_QB, OFF_KB, OFF_VB = 3 * MIX_A, 3 * MIX_A + MIX_B, 3 * MIX_A + 2 * MIX_B
OFF_QC = 3 * MIX_A + 3 * MIX_B
OFF_KC = OFF_QC + MIX_C
OFF_VC = OFF_KC + KVW_C

V7X_VMEM_LIMIT = 56 * 1024 * 1024
V7X_VMEM_LIMIT_W_OUT = 60 * 1024 * 1024
V7X_MXU_COLS = 256

PROJ_TILE = 1024
SWIGLU_TM, SWIGLU_TN = 4096, V7X_MXU_COLS
DOWN_TILE = 512
MODULATE_TM, NORM_TM = 512, 512

NA_QROWS = 16
NA_HEADS = 4
SW_TQ = 2048
SW_SUB = WINDOW_C
SW_SPAN = 3 * SW_SUB
DIFF_TQ = 128
DIFF_NSUB = 8


def _params(*sem):
    return pltpu.CompilerParams(dimension_semantics=sem, vmem_limit_bytes=V7X_VMEM_LIMIT)


def _dot(a, b):
    return jnp.dot(a, b, preferred_element_type=F32)


def _dot_nt(a, b):
    return lax.dot_general(a, b, (((1,), (1,)), ((), ())), preferred_element_type=F32)


def _sigmoid(x):
    return 0.5 * jnp.tanh(0.5 * x) + 0.5


def _cast_kernel(w_ref, o_ref):
    o_ref[...] = w_ref[...].astype(o_ref.dtype)


def cast_weight(w, layer, col0=0, ncols=None):
    _, r, c = w.shape
    ncols = c - col0 if ncols is None else ncols
    tc = 512 if r <= D_MODEL else HEAD_DIM
    assert col0 % tc == 0 and ncols % tc == 0
    return pl.pallas_call(
        _cast_kernel,
        grid=(ncols // tc,),
        in_specs=[pl.BlockSpec((None, r, tc), lambda j: (layer, 0, col0 // tc + j))],
        out_specs=pl.BlockSpec((r, tc), lambda j: (0, j)),
        out_shape=jax.ShapeDtypeStruct((r, ncols), BF16),
        compiler_params=_params("arbitrary"),
        name="cast_weight",
    )(w)


def _ada_kernel(c_ref, w_ref, b_ref, o_ref):
    cv = c_ref[...]
    s = (cv * _sigmoid(cv)).astype(BF16)
    o_ref[...] = _dot(s, w_ref[...].astype(BF16)) + b_ref[...]


def ada_modulation_all(cc, w_ada, b_ada):
    r = cc.shape[0]
    n = 6 * D_MODEL
    tn = 512
    out = pl.pallas_call(
        _ada_kernel,
        grid=(DEPTH, n // tn),
        in_specs=[
            pl.BlockSpec((r, D_MODEL), lambda l, j: (0, 0)),
            pl.BlockSpec((None, D_MODEL, tn), lambda l, j: (l, 0, j)),
            pl.BlockSpec((None, 1, tn), lambda l, j: (l, 0, j)),
        ],
        out_specs=pl.BlockSpec((None, r, tn), lambda l, j: (l, 0, j)),
        out_shape=jax.ShapeDtypeStruct((DEPTH, r, n), F32),
        compiler_params=_params("arbitrary", "arbitrary"),
        name="ada_modulation",
    )(cc, w_ada, b_ada.reshape(DEPTH, 1, n))
    return out.reshape(DEPTH, r, 6, 1, D_MODEL)


def _mod_spec(layer, chunk, rowfn):
    return pl.BlockSpec((None, None, None, 1, D_MODEL), lambda i: (layer, rowfn(i), chunk, 0, 0))


def _modulate_kernel(x_ref, sc_ref, sh_ref, o_ref):
    o_ref[...] = (x_ref[...] * (1.0 + sc_ref[...]) + sh_ref[...]).astype(BF16)


def modulate_cast(x, mod, layer, idx, rowfn_of_tile, tm=MODULATE_TM):
    m = x.shape[0]
    rowfn = functools.partial(rowfn_of_tile, tm)
    return pl.pallas_call(
        _modulate_kernel,
        grid=(m // tm,),
        in_specs=[
            pl.BlockSpec((tm, D_MODEL), lambda i: (i, 0)),
            _mod_spec(layer, 3 * idx + 1, rowfn),
            _mod_spec(layer, 3 * idx, rowfn),
        ],
        out_specs=pl.BlockSpec((tm, D_MODEL), lambda i: (i, 0)),
        out_shape=jax.ShapeDtypeStruct((m, D_MODEL), BF16),
        compiler_params=_params("arbitrary"),
        name="modulate",
    )(x, mod, mod)


def _ln_stats(v):
    mu = jnp.mean(v, axis=-1, keepdims=True)
    d = v - mu
    var = jnp.mean(d * d, axis=-1, keepdims=True)
    return mu, lax.rsqrt(var + LN_EPS)


def _ln_apply(v, mu, rstd, g, b):
    return (v - mu) * rstd * g + b


def _ln_mod_kernel(v_ref, g_ref, b_ref, sc_ref, sh_ref, ho_ref, mu_ref, rs_ref):
    v = v_ref[...]
    mu, rstd = _ln_stats(v)
    xn = _ln_apply(v, mu, rstd, g_ref[...], b_ref[...])
    ho_ref[...] = (xn * (1.0 + sc_ref[...]) + sh_ref[...]).astype(BF16)
    mu_ref[...] = mu
    rs_ref[...] = rstd


def _ln_kernel(v_ref, g_ref, b_ref, xo_ref):
    v = v_ref[...]
    mu, rstd = _ln_stats(v)
    xo_ref[...] = _ln_apply(v, mu, rstd, g_ref[...], b_ref[...])


def raw_residual(x):
    m = x.shape[0]
    return (x, jnp.zeros((m, 1), F32), jnp.ones((m, 1), F32), jnp.ones((1, D_MODEL), F32), jnp.zeros((1, D_MODEL), F32))


def post_norm(v, mod, ln_g, ln_b, rowfn_of_tile, next_mod=None, tm=NORM_TM):
    m = v.shape[0]
    rowfn = functools.partial(rowfn_of_tile, tm)
    row_spec = pl.BlockSpec((tm, D_MODEL), lambda i: (i, 0))
    vec_spec = pl.BlockSpec((1, D_MODEL), lambda i: (0, 0))
    stat_spec = pl.BlockSpec((tm, 1), lambda i: (i, 0))
    g, b = ln_g.reshape(1, D_MODEL), ln_b.reshape(1, D_MODEL)
    if next_mod is None:
        return pl.pallas_call(
            _ln_kernel, grid=(m // tm,), in_specs=[row_spec, vec_spec, vec_spec], out_specs=row_spec,
            out_shape=jax.ShapeDtypeStruct((m, D_MODEL), F32),
            compiler_params=_params("arbitrary"), name="post_norm",
        )(v, g, b)
    nl, nidx = next_mod
    h, mu, rstd = pl.pallas_call(
        _ln_mod_kernel, grid=(m // tm,),
        in_specs=[row_spec, vec_spec, vec_spec, _mod_spec(nl, 3 * nidx + 1, rowfn), _mod_spec(nl, 3 * nidx, rowfn)],
        out_specs=[row_spec, stat_spec, stat_spec],
        out_shape=[jax.ShapeDtypeStruct((m, D_MODEL), BF16), jax.ShapeDtypeStruct((m, 1), F32),
                   jax.ShapeDtypeStruct((m, 1), F32)],
        compiler_params=_params("arbitrary"), name="post_norm_modulate",
    )(v, g, b, mod, mod)
    return (v, mu, rstd, g, b), h


def _mm_kernel(x_ref, w_ref, o_ref, *, sigmoid):
    acc = _dot(x_ref[...], w_ref[...])
    if sigmoid:
        acc = _sigmoid(acc)
    o_ref[...] = acc.astype(o_ref.dtype)


def matmul(x, w, out_dtype, tm, tn, sigmoid=False, name="matmul"):
    m, k = x.shape
    n = w.shape[1]
    return pl.pallas_call(
        functools.partial(_mm_kernel, sigmoid=sigmoid),
        grid=(m // tm, n // tn),
        in_specs=[pl.BlockSpec((tm, k), lambda i, j: (i, 0)), pl.BlockSpec((k, tn), lambda i, j: (0, j))],
        out_specs=pl.BlockSpec((tm, tn), lambda i, j: (i, j)),
        out_shape=jax.ShapeDtypeStruct((m, n), out_dtype),
        compiler_params=_params("arbitrary", "arbitrary"),
        name=name,
    )(x, w)


def _mm_residual_kernel(x_ref, w_ref, v_ref, mu_ref, rs_ref, g_ref, b_ref, gate_ref, o_ref):
    res = _ln_apply(v_ref[...], mu_ref[...], rs_ref[...], g_ref[...], b_ref[...])
    o_ref[...] = ALPHA * res + gate_ref[...] * _dot(x_ref[...], w_ref[...])


def matmul_residual(x, w, lazy_res, mod, layer, idx, rowfn_of_tile, tm, tn, name, vmem_limit=V7X_VMEM_LIMIT):
    m, k = x.shape
    n = w.shape[1]
    rowfn = functools.partial(rowfn_of_tile, tm)
    gate_spec = pl.BlockSpec((None, None, None, 1, tn), lambda i, j: (layer, rowfn(i), 3 * idx + 2, 0, j))
    tile_spec = pl.BlockSpec((tm, tn), lambda i, j: (i, j))
    stat_spec = pl.BlockSpec((tm, 1), lambda i, j: (i, 0))
    vec_spec = pl.BlockSpec((1, tn), lambda i, j: (0, j))
    return pl.pallas_call(
        _mm_residual_kernel,
        grid=(m // tm, n // tn),
        in_specs=[pl.BlockSpec((tm, k), lambda i, j: (i, 0)), pl.BlockSpec((k, tn), lambda i, j: (0, j)),
                  tile_spec, stat_spec, stat_spec, vec_spec, vec_spec, gate_spec],
        out_specs=tile_spec,
        out_shape=jax.ShapeDtypeStruct((m, n), F32),
        compiler_params=pltpu.CompilerParams(dimension_semantics=("arbitrary", "arbitrary"),
                                             vmem_limit_bytes=vmem_limit),
        name=name,
    )(x, w, *lazy_res, mod)


def _rope_tables(t, width):
    half, quarter = width // 2, width // 4
    lane = np.arange(HEAD_DIM)
    e = lane % width
    pos = np.arange(t)
    p = np.where((e < half)[None, :], (pos // GRID_W)[:, None], (pos % GRID_W)[:, None]).astype(np.float64)
    inv = np.exp(-math.log(ROPE_BASE) * (e % quarter).astype(np.float64) / quarter)
    ang = (p.astype(np.float32) * inv.astype(np.float32)[None, :]).astype(np.float64)
    first = ((e % half) < quarter)[None, :]
    cos, sin = np.cos(ang), np.sin(ang)
    return jnp.asarray(cos, F32), jnp.asarray(np.where(first, -sin, sin), F32)


def _rope(x, cos, ssin, quarter):
    partner = lax.broadcasted_iota(jnp.int32, x.shape, 1) ^ quarter
    return x * cos + jnp.take_along_axis(x, partner, axis=1) * ssin


def _qkv_latent_kernel(x_ref, w_ref, cb_ref, sb_ref, cc_ref, sc_ref, o_ref, *, tn):
    j = pl.program_id(1)
    acc = _dot(x_ref[...], w_ref[...])
    nchunk = tn // HEAD_DIM
    tiles_qb = (OFF_QB // tn,)
    tiles_kb = (OFF_KB // tn,)
    tiles_qc = tuple(range(OFF_QC // tn, OFF_KC // tn))
    tile_kcvc = OFF_KC // tn

    def is_in(tiles):
        c = j == tiles[0]
        for tt in tiles[1:]:
            c = jnp.logical_or(c, j == tt)
        return c

    def store(fn_of_chunk):
        for ch in range(nchunk):
            sl = slice(ch * HEAD_DIM, (ch + 1) * HEAD_DIM)
            o_ref[:, sl] = fn_of_chunk(ch, acc[:, sl]).astype(o_ref.dtype)

    rope_b = lambda x: _rope(x, cb_ref[...], sb_ref[...], DB_HALF // 4)
    rope_c = lambda x: _rope(x, cc_ref[...], sc_ref[...], HEAD_DIM // 4)
    tiles_qa = tuple(range(OFF_QA // tn, OFF_KA // tn))
    special = is_in(tiles_qa + tiles_qb + tiles_kb + tiles_qc + (tile_kcvc,))

    @pl.when(jnp.logical_not(special))
    def _():
        o_ref[...] = acc.astype(o_ref.dtype)

    @pl.when(is_in(tiles_qa))
    def _():
        o_ref[...] = (acc * (HEAD_DIM ** -0.5 * LOG2E)).astype(o_ref.dtype)

    @pl.when(is_in(tiles_qb))
    def _():
        store(lambda ch, x: rope_b(x) * (DB_HALF ** -0.5 * LOG2E))

    @pl.when(is_in(tiles_kb))
    def _():
        store(lambda ch, x: rope_b(x))

    @pl.when(is_in(tiles_qc))
    def _():
        store(lambda ch, x: rope_c(x) * (HEAD_DIM ** -0.5 * LOG2E))

    @pl.when(j == tile_kcvc)
    def _():
        store(lambda ch, x: rope_c(x) if ch < KVW_C // HEAD_DIM else x)


def qkv_latent(h, w_qkv, seq, tm=PROJ_TILE, tn=PROJ_TILE):
    m, k = h.shape
    assert tn == MIX_A and OFF_KC % tn == 0 and (OFF_VC - OFF_KC) == KVW_C and seq % tm == 0
    tabs = _rope_tables(seq, DB_HALF) + _rope_tables(seq, HEAD_DIM)
    per_seq = seq // tm
    tab_spec = pl.BlockSpec((tm, HEAD_DIM), lambda i, j: (i % per_seq, 0))
    return pl.pallas_call(
        functools.partial(_qkv_latent_kernel, tn=tn),
        grid=(m // tm, QKV_W // tn),
        in_specs=[pl.BlockSpec((tm, k), lambda i, j: (i, 0)), pl.BlockSpec((k, tn), lambda i, j: (0, j))]
        + [tab_spec] * len(tabs),
        out_specs=pl.BlockSpec((tm, tn), lambda i, j: (i, j)),
        out_shape=jax.ShapeDtypeStruct((m, QKV_W), BF16),
        compiler_params=_params("arbitrary", "arbitrary"),
        name="qkv_latent",
    )(h, w_qkv, *tabs)


def _merge_kernel(oa_ref, ob_ref, oc_ref, wa_ref, wb_ref, wc_ref, ga_ref, gb_ref, gc_ref, o_ref):
    y = ga_ref[...].astype(F32) * _dot(oa_ref[...], wa_ref[...])
    y += gb_ref[...].astype(F32) * _dot(ob_ref[...], wb_ref[...])
    y += gc_ref[...].astype(F32) * _dot(oc_ref[...], wc_ref[...])
    o_ref[...] = y.astype(o_ref.dtype)


def merge_branches(o_parts, gates, w_branch, tm=PROJ_TILE, tn=PROJ_TILE):
    (oa, ja), (ob, jb), (oc, jc) = o_parts
    m = gates.shape[0]
    nj = D_MODEL // tn
    return pl.pallas_call(
        _merge_kernel,
        grid=(m // tm, nj),
        in_specs=[
            pl.BlockSpec((tm, MIX_A), lambda i, j: (i, ja)),
            pl.BlockSpec((tm, MIX_B), lambda i, j: (i, jb)),
            pl.BlockSpec((tm, MIX_C), lambda i, j: (i, jc)),
            pl.BlockSpec((MIX_A, tn), lambda i, j: (0, j)),
            pl.BlockSpec((MIX_B, tn), lambda i, j: (MIX_A // MIX_B, j)),
            pl.BlockSpec((MIX_C, tn), lambda i, j: ((MIX_A + MIX_B) // MIX_C, j)),
            pl.BlockSpec((tm, tn), lambda i, j: (i, j)),
            pl.BlockSpec((tm, tn), lambda i, j: (i, nj + j)),
            pl.BlockSpec((tm, tn), lambda i, j: (i, 2 * nj + j)),
        ],
        out_specs=pl.BlockSpec((tm, tn), lambda i, j: (i, j)),
        out_shape=jax.ShapeDtypeStruct((m, D_MODEL), BF16),
        compiler_params=_params("arbitrary", "arbitrary"),
        name="merge_branches",
    )(oa, ob, oc, w_branch, w_branch, w_branch, gates, gates, gates)


def _swiglu_kernel(x_ref, wg_ref, wu_ref, o_ref):
    x = x_ref[...]
    g = _dot(x, wg_ref[...])
    u = _dot(x, wu_ref[...])
    o_ref[...] = (g * _sigmoid(g) * u).astype(o_ref.dtype)


def swiglu_up(h, w_gate_up, tm=SWIGLU_TM, tn=SWIGLU_TN):
    m, k = h.shape
    nj = D_FF // tn
    return pl.pallas_call(
        _swiglu_kernel,
        grid=(m // tm, nj),
        in_specs=[
            pl.BlockSpec((tm, k), lambda i, j: (i, 0), pipeline_mode=pl.Buffered(1)),
            pl.BlockSpec((k, tn), lambda i, j: (0, j)),
            pl.BlockSpec((k, tn), lambda i, j: (0, nj + j)),
        ],
        out_specs=pl.BlockSpec((tm, tn), lambda i, j: (i, j)),
        out_shape=jax.ShapeDtypeStruct((m, D_FF), BF16),
        compiler_params=_params("arbitrary", "arbitrary"),
        name="swiglu_up",
    )(h, w_gate_up, w_gate_up)


def _diff_lambda(lam_ref, layer):
    lam_init = 0.8 - 0.6 * math.exp(-0.3 * layer)
    lf = lam_ref[...]
    a = jnp.sum(lf[0:1, :] * lf[1:2, :], axis=-1, keepdims=True)
    b = jnp.sum(lf[2:3, :] * lf[3:4, :], axis=-1, keepdims=True)
    return jnp.exp(a) - jnp.exp(b) + lam_init, lam_init


def _diff_finish(o, gain_ref, lam_init):
    o = o * lax.rsqrt(jnp.mean(o * o, axis=-1, keepdims=True) + RMS_EPS)
    return o * gain_ref[...] * (1.0 - lam_init)


def _split_halves(q):
    lane = lax.broadcasted_iota(jnp.int32, (1, HEAD_DIM), 1)
    zero = jnp.zeros_like(q)
    return jnp.where(lane < DB_HALF, q, zero), jnp.where(lane >= DB_HALF, q, zero)


def _sink_column(sink_ref, n, t):
    row = lax.broadcasted_iota(jnp.int32, (G_C * t, 1), 0)
    col = jnp.full((G_C * t, 1), sink_ref[n * G_C + G_C - 1], F32)
    for g in range(G_C - 2, -1, -1):
        col = jnp.where(row < (g + 1) * t, sink_ref[n * G_C + g], col)
    return col


def _softmax_parts(parts, extra=None, exp=jnp.exp):
    m = functools.reduce(jnp.maximum, [jnp.max(s, axis=-1, keepdims=True) for s in parts])
    if extra is not None:
        m = jnp.maximum(m, extra)
    ps = [exp(s - m) for s in parts]
    l = functools.reduce(jnp.add, [jnp.sum(p, axis=-1, keepdims=True) for p in ps])
    if extra is not None:
        l = l + exp(extra - m)
    return ps, l


def _context_attn_kernel(sink_ref, z_ref, lam_ref, gain_ref, o_ref, *, layer):
    scale = HEAD_DIM ** -0.5

    def col(off, h, width=HEAD_DIM):
        return z_ref[:, off + h * width: off + (h + 1) * width]

    for h in range(H_A):
        q, k, v = col(OFF_QA, h).astype(BF16), col(OFF_KA, h).astype(BF16), col(OFF_VA, h).astype(BF16)
        (p,), l = _softmax_parts([_dot_nt(q, k) * scale])
        o = _dot((p / l).astype(BF16), v)
        o_ref[:, h * HEAD_DIM:(h + 1) * HEAD_DIM] = o.astype(o_ref.dtype)

    lam, lam_init = _diff_lambda(lam_ref, layer)
    for h in range(H_B):
        q, k, v = col(OFF_QB, h).astype(BF16), col(OFF_KB, h).astype(BF16), col(OFF_VB, h).astype(BF16)
        q1, q2 = _split_halves(q)
        (p0,), l0 = _softmax_parts([_dot_nt(q1, k) * (DB_HALF ** -0.5)])
        (p1,), l1 = _softmax_parts([_dot_nt(q2, k) * (DB_HALF ** -0.5)])
        w = p0 / l0 - lam * (p1 / l1)
        o = _diff_finish(_dot(w.astype(BF16), v), gain_ref, lam_init)
        o_ref[:, MIX_A + h * HEAD_DIM: MIX_A + (h + 1) * HEAD_DIM] = o.astype(o_ref.dtype)

    t = z_ref.shape[0]
    for n in range(KV_C):
        q = jnp.concatenate([col(OFF_QC, n * G_C + g) for g in range(G_C)], axis=0).astype(BF16)
        k, v = col(OFF_KC, n).astype(BF16), col(OFF_VC, n).astype(BF16)
        sink = _sink_column(sink_ref, n, t)
        (p,), l = _softmax_parts([_dot_nt(q, k) * scale], extra=sink)
        o = _dot((p / l).astype(BF16), v)
        for g in range(G_C):
            c0 = MIX_A + MIX_B + (n * G_C + g) * HEAD_DIM
            o_ref[:, c0:c0 + HEAD_DIM] = o[g * t:(g + 1) * t].astype(o_ref.dtype)


def context_attention(z, seq, lam, gain, sink, layer):
    m = z.shape[0]
    return pl.pallas_call(
        functools.partial(_context_attn_kernel, layer=layer),
        grid=(m // seq,),
        in_specs=[
            pl.BlockSpec(memory_space=pltpu.SMEM),
            pl.BlockSpec((seq, QKV_W), lambda b: (b, 0)),
            pl.BlockSpec((4, DB_HALF), lambda b: (0, 0)),
            pl.BlockSpec((1, HEAD_DIM), lambda b: (0, 0)),
        ],
        out_specs=pl.BlockSpec((seq, D_MODEL), lambda b: (b, 0)),
        out_shape=jax.ShapeDtypeStruct((m, D_MODEL), BF16),
        compiler_params=_params("arbitrary"),
        name="context_attention",
    )(sink, z, lam, gain.reshape(1, HEAD_DIM))


def _na_bias_table(rpb_l):
    pad = GRID_W
    rp = jnp.pad(rpb_l.astype(F32) * LOG2E, ((0, 0), (0, 0), (pad, pad)))
    first = pad + NA_COLS - 1
    tz = jnp.stack([rp[:, :, first - qc:first - qc + GRID_W] for qc in range(GRID_W)], axis=2)
    qc, kc = np.arange(GRID_W)[:, None], np.arange(GRID_W)[None, :]
    ws = np.clip(qc - NA_COLS // 2, 0, GRID_W - NA_COLS)
    valid = (kc >= ws) & (kc < ws + NA_COLS)
    tz = jnp.where(jnp.asarray(valid)[None, None], tz, NEG_INF)
    return jnp.stack([jnp.concatenate([tz[:, e0 + j] for j in range(NA_ROWS)], axis=-1)
                      for e0 in range(NA_ROWS)], axis=1)


def _na_latent_kernel(q_ref, k_ref, v_ref, ck_ref, cv_ref, bias_ref, o_ref, *, rows):
    rb = pl.program_id(2)
    span = NA_ROWS * GRID_W
    starts, biases = [], []
    for r in range(NA_QROWS):
        r_abs = rb * NA_QROWS + r
        rs = jnp.clip(r_abs - NA_ROWS // 2, 0, rows - NA_ROWS)
        starts.append(pl.multiple_of(rs * GRID_W, GRID_W))
        biases.append(rs - r_abs + NA_ROWS - 1)
    heads = [slice(hh * HEAD_DIM, (hh + 1) * HEAD_DIM) for hh in range(NA_HEADS)]
    scores = []
    for hh, cols in enumerate(heads):
        q = q_ref[:, cols]
        s_loc = [_dot_nt(q[r * GRID_W:(r + 1) * GRID_W], k_ref[pl.ds(starts[r], span), cols]) + bias_ref[hh, biases[r]]
                 for r in range(NA_QROWS)]
        scores.append([jnp.concatenate(s_loc, axis=0), _dot_nt(q, ck_ref[:, cols].astype(BF16))])
    stats = [_softmax_parts(sc, exp=jnp.exp2) for sc in scores]
    for cols, ((p_loc, p_ctx), l) in zip(heads, stats):
        p_loc = p_loc.astype(BF16)
        o_loc = [_dot(p_loc[r * GRID_W:(r + 1) * GRID_W], v_ref[pl.ds(starts[r], span), cols]) for r in range(NA_QROWS)]
        o = jnp.concatenate(o_loc, axis=0) + _dot(p_ctx.astype(BF16), cv_ref[:, cols].astype(BF16))
        o_ref[:, cols] = (o / l).astype(o_ref.dtype)


def na_latent(z, cache_k, cache_v, layer, bias, batch, seq):
    rows = seq // GRID_W
    assert rows % NA_QROWS == 0 and rows >= NA_ROWS and H_A % NA_HEADS == 0
    nrb = rows // NA_QROWS
    tq = NA_QROWS * GRID_W
    hw = NA_HEADS * HEAD_DIM
    n_ctx = cache_k.shape[2]
    ck = cache_k.reshape(batch, DEPTH, n_ctx, MIX_A)
    cv = cache_v.reshape(batch, DEPTH, n_ctx, MIX_A)
    ctx_spec = pl.BlockSpec((None, None, n_ctx, hw), lambda b, h, rb: (b, layer, 0, h))
    return pl.pallas_call(
        functools.partial(_na_latent_kernel, rows=rows),
        grid=(batch, H_A // NA_HEADS, nrb),
        in_specs=[
            pl.BlockSpec((tq, hw), lambda b, h, rb: (b * nrb + rb, OFF_QA // hw + h)),
            pl.BlockSpec((seq, hw), lambda b, h, rb: (b, OFF_KA // hw + h)),
            pl.BlockSpec((seq, hw), lambda b, h, rb: (b, OFF_VA // hw + h)),
            ctx_spec, ctx_spec,
            pl.BlockSpec((NA_HEADS, NA_ROWS, GRID_W, NA_ROWS * GRID_W), lambda b, h, rb: (h, 0, 0, 0)),
        ],
        out_specs=pl.BlockSpec((tq, hw), lambda b, h, rb: (b * nrb + rb, h)),
        out_shape=jax.ShapeDtypeStruct((batch * seq, MIX_A), BF16),
        compiler_params=_params("arbitrary", "arbitrary", "arbitrary"),
        name="na_latent",
    )(z, z, z, ck, cv, bias)


def _diff_latent_kernel(lam_ref, gain_ref, q_ref, k_ref, v_ref, ck_ref, cv_ref, o_ref, *, layer):
    lam, lam_init = _diff_lambda(lam_ref, layer)
    k = k_ref[...]
    ck = ck_ref[...].astype(BF16)
    cv = cv_ref[...].astype(BF16)
    nsub = q_ref.shape[0] // DIFF_TQ
    rows = [slice(u * DIFF_TQ, (u + 1) * DIFF_TQ) for u in range(nsub)]
    scores = [[[_dot_nt(qh, k), _dot_nt(qh, ck)] for qh in _split_halves(q_ref[r, :])] for r in rows]
    stats = [[_softmax_parts(sc, exp=jnp.exp2) for sc in su] for su in scores]
    for r, ((p0, l0), (p1, l1)) in zip(rows, stats):
        c = lam * l0 / l1
        wl = (p0[0] - p1[0] * c).astype(BF16)
        wc = (p0[1] - p1[1] * c).astype(BF16)
        o = (_dot(wl, v_ref[...]) + _dot(wc, cv)) / l0
        o_ref[r, :] = _diff_finish(o, gain_ref, lam_init).astype(o_ref.dtype)


def diff_latent(z, cache_k, cache_v, layer, lam, gain, batch, seq):
    tq = DIFF_TQ * DIFF_NSUB
    n_ctx = cache_k.shape[2]
    ck = cache_k.reshape(batch, DEPTH, n_ctx, MIX_B)
    cv = cache_v.reshape(batch, DEPTH, n_ctx, MIX_B)
    nqt = seq // tq
    ctx_spec = pl.BlockSpec((None, None, n_ctx, HEAD_DIM), lambda b, h, t: (b, layer, 0, h))
    return pl.pallas_call(
        functools.partial(_diff_latent_kernel, layer=layer),
        grid=(batch, H_B, nqt),
        in_specs=[
            pl.BlockSpec((4, DB_HALF), lambda b, h, t: (0, 0)),
            pl.BlockSpec((1, HEAD_DIM), lambda b, h, t: (0, 0)),
            pl.BlockSpec((tq, HEAD_DIM), lambda b, h, t: (b * nqt + t, OFF_QB // HEAD_DIM + h)),
            pl.BlockSpec((seq, HEAD_DIM), lambda b, h, t: (b, OFF_KB // HEAD_DIM + h)),
            pl.BlockSpec((seq, HEAD_DIM), lambda b, h, t: (b, OFF_VB // HEAD_DIM + h)),
            ctx_spec, ctx_spec,
        ],
        out_specs=pl.BlockSpec((tq, HEAD_DIM), lambda b, h, t: (b * nqt + t, h)),
        out_shape=jax.ShapeDtypeStruct((batch * seq, MIX_B), BF16),
        compiler_params=_params("arbitrary", "arbitrary", "arbitrary"),
        name="diff_latent",
    )(lam, gain.reshape(1, HEAD_DIM), z, z, z, ck, cv)


def _window_mask_table():
    i, j = np.arange(SW_SUB)[:, None], np.arange(SW_SPAN)[None, :]
    tabs = [np.where(np.abs(c * WINDOW_C + i - j) <= WINDOW_C, 0.0, NEG_INF) for c in range(3)]
    return jnp.asarray(np.stack(tabs), F32)


def _window_latent_kernel(sink_ref, mask_ref, q_ref, k_ref, v_ref, ck_ref, cv_ref, o_ref, *, seq):
    n = pl.program_id(1)
    t0 = pl.program_id(2) * SW_TQ
    ck = ck_ref[...].astype(BF16)
    cv = cv_ref[...].astype(BF16)
    sink = _sink_column(sink_ref, n, SW_SUB) * LOG2E
    nsub = SW_TQ // SW_SUB
    rows = [slice(u * SW_SUB, (u + 1) * SW_SUB) for u in range(nsub)]
    starts, scores = [], []
    for u in range(nsub):
        tu = t0 + u * SW_SUB
        ks = pl.multiple_of(jnp.clip(tu - WINDOW_C, 0, seq - SW_SPAN), WINDOW_C)
        case = (tu - ks) // WINDOW_C
        q = jnp.concatenate([q_ref[rows[u], g * HEAD_DIM:(g + 1) * HEAD_DIM] for g in range(G_C)], axis=0)
        s_loc = _dot_nt(q, k_ref[pl.ds(ks, SW_SPAN), :]).reshape(G_C, SW_SUB, SW_SPAN) + mask_ref[case][None]
        starts.append(ks)
        scores.append([s_loc.reshape(G_C * SW_SUB, SW_SPAN), _dot_nt(q, ck)])
    stats = [_softmax_parts(sc, extra=sink, exp=jnp.exp2) for sc in scores]
    for u, ((p_loc, p_ctx), l) in enumerate(stats):
        o = (_dot(p_loc.astype(BF16), v_ref[pl.ds(starts[u], SW_SPAN), :]) + _dot(p_ctx.astype(BF16), cv)) / l
        for g in range(G_C):
            o_ref[rows[u], g * HEAD_DIM:(g + 1) * HEAD_DIM] = o[g * SW_SUB:(g + 1) * SW_SUB].astype(o_ref.dtype)


def window_latent(z, cache_k, cache_v, layer, sink, batch, seq):
    assert seq % SW_TQ == 0 and seq >= SW_SPAN and SW_TQ % SW_SUB == 0
    n_ctx = cache_k.shape[2]
    ck = cache_k.reshape(batch, DEPTH, n_ctx, KVW_C)
    cv = cache_v.reshape(batch, DEPTH, n_ctx, KVW_C)
    nqt = seq // SW_TQ
    gw = G_C * HEAD_DIM
    ctx_spec = pl.BlockSpec((None, None, n_ctx, HEAD_DIM), lambda b, n, t: (b, layer, 0, n))
    return pl.pallas_call(
        functools.partial(_window_latent_kernel, seq=seq),
        grid=(batch, KV_C, nqt),
        in_specs=[
            pl.BlockSpec(memory_space=pltpu.SMEM),
            pl.BlockSpec((3, SW_SUB, SW_SPAN), lambda b, n, t: (0, 0, 0)),
            pl.BlockSpec((SW_TQ, gw), lambda b, n, t: (b * nqt + t, OFF_QC // gw + n)),
            pl.BlockSpec((seq, HEAD_DIM), lambda b, n, t: (b, OFF_KC // HEAD_DIM + n)),
            pl.BlockSpec((seq, HEAD_DIM), lambda b, n, t: (b, OFF_VC // HEAD_DIM + n)),
            ctx_spec, ctx_spec,
        ],
        out_specs=pl.BlockSpec((SW_TQ, gw), lambda b, n, t: (b * nqt + t, n)),
        out_shape=jax.ShapeDtypeStruct((batch * seq, MIX_C), BF16),
        compiler_params=_params("arbitrary", "arbitrary", "arbitrary"),
        name="window_latent",
    )(sink, _window_mask_table(), z, z, z, ck, cv)


def _prompt_row(tm, i):
    return 0


def kernel(x_prompt, x_sample, cache_a_k, cache_a_v, cache_b_k, cache_b_v, cache_c_k, cache_c_v, c, c_ctx,
           w_ada, b_ada, w_in, rpb_a, lam_b, subln_b, sink_c, w_branch, w_out, ln_g, ln_b, w_gate_up, w_down):
    bp, tp, _ = x_prompt.shape
    bs, ts, _ = x_sample.shape

    def sample_row(tm, i):
        return 1 + (i * tm) // ts

    streams = {
        "p": dict(x=x_prompt.reshape(bp * tp, D_MODEL), rowfn=_prompt_row),
        "s": dict(x=x_sample.reshape(bs * ts, D_MODEL), rowfn=sample_row),
    }
    cc = jnp.concatenate([c_ctx[None, :], c], axis=0)
    cc = jnp.pad(cc, ((0, (-cc.shape[0]) % 8), (0, 0)))
    mod = ada_modulation_all(cc, w_ada, b_ada)

    for st in streams.values():
        st["h"] = modulate_cast(st["x"], mod, 0, 0, st["rowfn"])
        st["res"] = raw_residual(st["x"])

    new_kv = [[] for _ in range(6)]
    for l in range(DEPTH):
        w_qkv = cast_weight(w_in, l, 0, QKV_W)
        w_g = cast_weight(w_in, l, QKV_W)
        w_br = cast_weight(w_branch, l)
        w_o = cast_weight(w_out, l)
        w_gu = cast_weight(w_gate_up, l)
        w_dn = cast_weight(w_down, l)
        na_bias = _na_bias_table(rpb_a[l])

        sp = streams["p"]
        zp = matmul(sp["h"], w_qkv, F32, PROJ_TILE, PROJ_TILE, name="qkv_context")
        for lst, off, wdt, shp in (
                (new_kv[0], OFF_KA, MIX_A, (H_A, HEAD_DIM)), (new_kv[1], OFF_VA, MIX_A, (H_A, HEAD_DIM)),
                (new_kv[2], OFF_KB, MIX_B, (H_B, 2, DB_HALF)), (new_kv[3], OFF_VB, MIX_B, (H_B, HEAD_DIM)),
                (new_kv[4], OFF_KC, KVW_C, (KV_C, HEAD_DIM)), (new_kv[5], OFF_VC, KVW_C, (KV_C, HEAD_DIM))):
            lst.append(zp[:, off:off + wdt].reshape((bp, tp) + shp))
        op = context_attention(zp, tp, lam_b[l], subln_b[l], sink_c[l], l)
        sp["o"] = ((op, 0), (op, MIX_A // MIX_B), (op, (MIX_A + MIX_B) // MIX_C))

        ss = streams["s"]
        zs = qkv_latent(ss["h"], w_qkv, ts)
        oa = na_latent(zs, cache_a_k, cache_a_v, l, na_bias, bs, ts)
        ob = diff_latent(zs, cache_b_k, cache_b_v, l, lam_b[l], subln_b[l], bs, ts)
        oc = window_latent(zs, cache_c_k, cache_c_v, l, sink_c[l], bs, ts)
        ss["o"] = ((oa, 0), (ob, 0), (oc, 0))

        for st in streams.values():
            gates = matmul(st["h"], w_g, BF16, PROJ_TILE, PROJ_TILE, sigmoid=True, name="branch_gates")
            u = merge_branches(st["o"], gates, w_br)
            v = matmul_residual(u, w_o, st["res"], mod, l, 0, st["rowfn"], PROJ_TILE, PROJ_TILE, "mixer_out",
                                vmem_limit=V7X_VMEM_LIMIT_W_OUT)
            st["res"], h2 = post_norm(v, mod, ln_g[l, 0], ln_b[l, 0], st["rowfn"], next_mod=(l, 1))
            a = swiglu_up(h2, w_gu)
            v = matmul_residual(a, w_dn, st["res"], mod, l, 1, st["rowfn"], DOWN_TILE, DOWN_TILE, "ffn_down")
            if l + 1 < DEPTH:
                st["res"], st["h"] = post_norm(v, mod, ln_g[l, 1], ln_b[l, 1], st["rowfn"], next_mod=(l + 1, 0))
            else:
                st["out"] = post_norm(v, mod, ln_g[l, 1], ln_b[l, 1], st["rowfn"])

    outs = [jnp.stack(lst, axis=1) for lst in new_kv]
    return (streams["p"]["out"].reshape(bp, tp, D_MODEL), streams["s"]["out"].reshape(bs, ts, D_MODEL), *outs)
```

```python
import functools
import math

import numpy as np
import jax
import jax.numpy as jnp
from jax import lax
from jax.experimental import pallas as pl
from jax.experimental.pallas import tpu as pltpu

D_MODEL = 4096
DEPTH = 2
GRID_W = 64
HEAD_DIM = 128
H_A = 8
MIX_A = H_A * HEAD_DIM
NA_ROWS = 8
NA_COLS = 16
H_B = 8
DB_HALF = HEAD_DIM // 2
MIX_B = H_B * HEAD_DIM
H_C = 16
KV_C = 4
G_C = H_C // KV_C
MIX_C = H_C * HEAD_DIM
KVW_C = KV_C * HEAD_DIM
WINDOW_C = 128
N_BRANCH = 3
QKV_W = 3 * MIX_A + 3 * MIX_B + MIX_C + 2 * KVW_C
D_FF = 11008
ROPE_BASE = 10000.0
LN_EPS = 1e-5
RMS_EPS = 1e-5
ALPHA = (2 * DEPTH) ** 0.25
NEG_INF = -1e30
LOG2E = math.log2(math.e)
BF16 = jnp.bfloat16
F32 = jnp.float32

OFF_QA, OFF_KA, OFF_VA = 0, MIX_A, 2 * MIX_A
OFF_QB, OFF_KB, OFF_VB = 3 * MIX_A, 3 * MIX_A + MIX_B, 3 * MIX_A + 2 * MIX_B
OFF_QC = 3 * MIX_A + 3 * MIX_B
OFF_KC = OFF_QC + MIX_C
OFF_VC = OFF_KC + KVW_C

V7X_VMEM_LIMIT = 56 * 1024 * 1024
V7X_MXU_COLS = 256

PROJ_TILE = 1024
SWIGLU_TM, SWIGLU_TN = 4096, V7X_MXU_COLS
DOWN_TILE = 512
MODULATE_TM, NORM_TM = 512, 512

NA_QROWS = 16
NA_HEADS = 4
SW_TQ = 2048
SW_SUB = WINDOW_C
SW_SPAN = 3 * SW_SUB
DIFF_TQ = 128
DIFF_NSUB = 8


def _params(*sem):
    return pltpu.CompilerParams(dimension_semantics=sem, vmem_limit_bytes=V7X_VMEM_LIMIT)


def _dot(a, b):
    return jnp.dot(a, b, preferred_element_type=F32)


def _dot_nt(a, b):
    return lax.dot_general(a, b, (((1,), (1,)), ((), ())), preferred_element_type=F32)


def _sigmoid(x):
    return 0.5 * jnp.tanh(0.5 * x) + 0.5


def _cast_kernel(w_ref, o_ref):
    o_ref[...] = w_ref[...].astype(o_ref.dtype)


def cast_weight(w, layer, col0=0, ncols=None):
    _, r, c = w.shape
    ncols = c - col0 if ncols is None else ncols
    tc = 512 if r <= D_MODEL else HEAD_DIM
    assert col0 % tc == 0 and ncols % tc == 0
    return pl.pallas_call(
        _cast_kernel,
        grid=(ncols // tc,),
        in_specs=[pl.BlockSpec((None, r, tc), lambda j: (layer, 0, col0 // tc + j))],
        out_specs=pl.BlockSpec((r, tc), lambda j: (0, j)),
        out_shape=jax.ShapeDtypeStruct((r, ncols), BF16),
        compiler_params=_params("arbitrary"),
        name="cast_weight",
    )(w)


def _ada_kernel(c_ref, w_ref, b_ref, o_ref):
    cv = c_ref[...]
    s = (cv * _sigmoid(cv)).astype(BF16)
    o_ref[...] = _dot(s, w_ref[...].astype(BF16)) + b_ref[...]


def ada_modulation_all(cc, w_ada, b_ada):
    r = cc.shape[0]
    n = 6 * D_MODEL
    tn = 512
    out = pl.pallas_call(
        _ada_kernel,
        grid=(DEPTH, n // tn),
        in_specs=[
            pl.BlockSpec((r, D_MODEL), lambda l, j: (0, 0)),
            pl.BlockSpec((None, D_MODEL, tn), lambda l, j: (l, 0, j)),
            pl.BlockSpec((None, 1, tn), lambda l, j: (l, 0, j)),
        ],
        out_specs=pl.BlockSpec((None, r, tn), lambda l, j: (l, 0, j)),
        out_shape=jax.ShapeDtypeStruct((DEPTH, r, n), F32),
        compiler_params=_params("arbitrary", "arbitrary"),
        name="ada_modulation",
    )(cc, w_ada, b_ada.reshape(DEPTH, 1, n))
    return out.reshape(DEPTH, r, 6, 1, D_MODEL)


def _mod_spec(layer, chunk, rowfn):
    return pl.BlockSpec((None, None, None, 1, D_MODEL), lambda i: (layer, rowfn(i), chunk, 0, 0))


def _modulate_kernel(x_ref, sc_ref, sh_ref, o_ref):
    o_ref[...] = (x_ref[...] * (1.0 + sc_ref[...]) + sh_ref[...]).astype(BF16)


def modulate_cast(x, mod, layer, idx, rowfn_of_tile, tm=MODULATE_TM):
    m = x.shape[0]
    rowfn = functools.partial(rowfn_of_tile, tm)
    return pl.pallas_call(
        _modulate_kernel,
        grid=(m // tm,),
        in_specs=[
            pl.BlockSpec((tm, D_MODEL), lambda i: (i, 0)),
            _mod_spec(layer, 3 * idx + 1, rowfn),
            _mod_spec(layer, 3 * idx, rowfn),
        ],
        out_specs=pl.BlockSpec((tm, D_MODEL), lambda i: (i, 0)),
        out_shape=jax.ShapeDtypeStruct((m, D_MODEL), BF16),
        compiler_params=_params("arbitrary"),
        name="modulate",
    )(x, mod, mod)


def _ln_core(v_ref, g_ref, b_ref):
    v = v_ref[...]
    mu = jnp.mean(v, axis=-1, keepdims=True)
    d = v - mu
    var = jnp.mean(d * d, axis=-1, keepdims=True)
    return d * lax.rsqrt(var + LN_EPS) * g_ref[...] + b_ref[...]


def _ln_mod_kernel(v_ref, g_ref, b_ref, sc_ref, sh_ref, xo_ref, ho_ref):
    xn = _ln_core(v_ref, g_ref, b_ref)
    xo_ref[...] = xn
    ho_ref[...] = (xn * (1.0 + sc_ref[...]) + sh_ref[...]).astype(BF16)


def _ln_kernel(v_ref, g_ref, b_ref, xo_ref):
    xo_ref[...] = _ln_core(v_ref, g_ref, b_ref)


def post_norm(v, mod, ln_g, ln_b, rowfn_of_tile, next_mod=None, tm=NORM_TM):
    m = v.shape[0]
    rowfn = functools.partial(rowfn_of_tile, tm)
    row_spec = pl.BlockSpec((tm, D_MODEL), lambda i: (i, 0))
    vec_spec = pl.BlockSpec((1, D_MODEL), lambda i: (0, 0))
    in_specs = [row_spec, vec_spec, vec_spec]
    args = [v, ln_g.reshape(1, D_MODEL), ln_b.reshape(1, D_MODEL)]
    if next_mod is None:
        return pl.pallas_call(
            _ln_kernel, grid=(m // tm,), in_specs=in_specs, out_specs=row_spec,
            out_shape=jax.ShapeDtypeStruct((m, D_MODEL), F32),
            compiler_params=_params("arbitrary"), name="post_norm",
        )(*args), None
    nl, nidx = next_mod
    in_specs += [_mod_spec(nl, 3 * nidx + 1, rowfn), _mod_spec(nl, 3 * nidx, rowfn)]
    args += [mod, mod]
    return pl.pallas_call(
        _ln_mod_kernel, grid=(m // tm,), in_specs=in_specs, out_specs=[row_spec, row_spec],
        out_shape=[jax.ShapeDtypeStruct((m, D_MODEL), F32), jax.ShapeDtypeStruct((m, D_MODEL), BF16)],
        compiler_params=_params("arbitrary"), name="post_norm_modulate",
    )(*args)


def _mm_kernel(x_ref, w_ref, o_ref, *, sigmoid):
    acc = _dot(x_ref[...], w_ref[...])
    if sigmoid:
        acc = _sigmoid(acc)
    o_ref[...] = acc.astype(o_ref.dtype)


def matmul(x, w, out_dtype, tm, tn, sigmoid=False, name="matmul"):
    m, k = x.shape
    n = w.shape[1]
    return pl.pallas_call(
        functools.partial(_mm_kernel, sigmoid=sigmoid),
        grid=(m // tm, n // tn),
        in_specs=[pl.BlockSpec((tm, k), lambda i, j: (i, 0)), pl.BlockSpec((k, tn), lambda i, j: (0, j))],
        out_specs=pl.BlockSpec((tm, tn), lambda i, j: (i, j)),
        out_shape=jax.ShapeDtypeStruct((m, n), out_dtype),
        compiler_params=_params("arbitrary", "arbitrary"),
        name=name,
    )(x, w)


def _mm_residual_kernel(x_ref, w_ref, res_ref, gate_ref, o_ref):
    o_ref[...] = ALPHA * res_ref[...] + gate_ref[...] * _dot(x_ref[...], w_ref[...])


def matmul_residual(x, w, res, mod, layer, idx, rowfn_of_tile, tm, tn, name):
    m, k = x.shape
    n = w.shape[1]
    rowfn = functools.partial(rowfn_of_tile, tm)
    gate_spec = pl.BlockSpec((None, None, None, 1, tn), lambda i, j: (layer, rowfn(i), 3 * idx + 2, 0, j))
    tile_spec = pl.BlockSpec((tm, tn), lambda i, j: (i, j))
    return pl.pallas_call(
        _mm_residual_kernel,
        grid=(m // tm, n // tn),
        in_specs=[pl.BlockSpec((tm, k), lambda i, j: (i, 0)), pl.BlockSpec((k, tn), lambda i, j: (0, j)),
                  tile_spec, gate_spec],
        out_specs=tile_spec,
        out_shape=jax.ShapeDtypeStruct((m, n), F32),
        compiler_params=_params("arbitrary", "arbitrary"),
        name=name,
    )(x, w, res, mod)


def _rope_tables(t, width):
    half, quarter = width // 2, width // 4
    lane = np.arange(HEAD_DIM)
    e = lane % width
    pos = np.arange(t)
    p = np.where((e < half)[None, :], (pos // GRID_W)[:, None], (pos % GRID_W)[:, None]).astype(np.float64)
    inv = np.exp(-math.log(ROPE_BASE) * (e % quarter).astype(np.float64) / quarter)
    ang = (p.astype(np.float32) * inv.astype(np.float32)[None, :]).astype(np.float64)
    first = ((e % half) < quarter)[None, :]
    cos, sin = np.cos(ang), np.sin(ang)
    return jnp.asarray(cos, F32), jnp.asarray(np.where(first, -sin, sin), F32)


def _rope(x, cos, ssin, quarter):
    partner = lax.broadcasted_iota(jnp.int32, x.shape, 1) ^ quarter
    return x * cos + jnp.take_along_axis(x, partner, axis=1) * ssin


def _qkv_latent_kernel(x_ref, w_ref, cb_ref, sb_ref, cc_ref, sc_ref, o_ref, *, tn):
    j = pl.program_id(1)
    acc = _dot(x_ref[...], w_ref[...])
    nchunk = tn // HEAD_DIM
    tiles_qb = (OFF_QB // tn,)
    tiles_kb = (OFF_KB // tn,)
    tiles_qc = tuple(range(OFF_QC // tn, OFF_KC // tn))
    tile_kcvc = OFF_KC // tn

    def is_in(tiles):
        c = j == tiles[0]
        for tt in tiles[1:]:
            c = jnp.logical_or(c, j == tt)
        return c

    def store(fn_of_chunk):
        for ch in range(nchunk):
            sl = slice(ch * HEAD_DIM, (ch + 1) * HEAD_DIM)
            o_ref[:, sl] = fn_of_chunk(ch, acc[:, sl]).astype(o_ref.dtype)

    rope_b = lambda x: _rope(x, cb_ref[...], sb_ref[...], DB_HALF // 4)
    rope_c = lambda x: _rope(x, cc_ref[...], sc_ref[...], HEAD_DIM // 4)
    tiles_qa = tuple(range(OFF_QA // tn, OFF_KA // tn))
    special = is_in(tiles_qa + tiles_qb + tiles_kb + tiles_qc + (tile_kcvc,))

    @pl.when(jnp.logical_not(special))
    def _():
        o_ref[...] = acc.astype(o_ref.dtype)

    @pl.when(is_in(tiles_qa))
    def _():
        o_ref[...] = (acc * (HEAD_DIM ** -0.5 * LOG2E)).astype(o_ref.dtype)

    @pl.when(is_in(tiles_qb))
    def _():
        store(lambda ch, x: rope_b(x) * (DB_HALF ** -0.5 * LOG2E))

    @pl.when(is_in(tiles_kb))
    def _():
        store(lambda ch, x: rope_b(x))

    @pl.when(is_in(tiles_qc))
    def _():
        store(lambda ch, x: rope_c(x) * (HEAD_DIM ** -0.5 * LOG2E))

    @pl.when(j == tile_kcvc)
    def _():
        store(lambda ch, x: rope_c(x) if ch < KVW_C // HEAD_DIM else x)


def qkv_latent(h, w_qkv, seq, tm=PROJ_TILE, tn=PROJ_TILE):
    m, k = h.shape
    assert tn == MIX_A and OFF_KC % tn == 0 and (OFF_VC - OFF_KC) == KVW_C and seq % tm == 0
    tabs = _rope_tables(seq, DB_HALF) + _rope_tables(seq, HEAD_DIM)
    per_seq = seq // tm
    tab_spec = pl.BlockSpec((tm, HEAD_DIM), lambda i, j: (i % per_seq, 0))
    return pl.pallas_call(
        functools.partial(_qkv_latent_kernel, tn=tn),
        grid=(m // tm, QKV_W // tn),
        in_specs=[pl.BlockSpec((tm, k), lambda i, j: (i, 0)), pl.BlockSpec((k, tn), lambda i, j: (0, j))]
        + [tab_spec] * len(tabs),
        out_specs=pl.BlockSpec((tm, tn), lambda i, j: (i, j)),
        out_shape=jax.ShapeDtypeStruct((m, QKV_W), BF16),
        compiler_params=_params("arbitrary", "arbitrary"),
        name="qkv_latent",
    )(h, w_qkv, *tabs)


def _merge_kernel(oa_ref, ob_ref, oc_ref, wa_ref, wb_ref, wc_ref, ga_ref, gb_ref, gc_ref, o_ref):
    y = ga_ref[...].astype(F32) * _dot(oa_ref[...], wa_ref[...])
    y += gb_ref[...].astype(F32) * _dot(ob_ref[...], wb_ref[...])
    y += gc_ref[...].astype(F32) * _dot(oc_ref[...], wc_ref[...])
    o_ref[...] = y.astype(o_ref.dtype)


def merge_branches(o_parts, gates, w_branch, tm=PROJ_TILE, tn=PROJ_TILE):
    (oa, ja), (ob, jb), (oc, jc) = o_parts
    m = gates.shape[0]
    nj = D_MODEL // tn
    return pl.pallas_call(
        _merge_kernel,
        grid=(m // tm, nj),
        in_specs=[
            pl.BlockSpec((tm, MIX_A), lambda i, j: (i, ja)),
            pl.BlockSpec((tm, MIX_B), lambda i, j: (i, jb)),
            pl.BlockSpec((tm, MIX_C), lambda i, j: (i, jc)),
            pl.BlockSpec((MIX_A, tn), lambda i, j: (0, j)),
            pl.BlockSpec((MIX_B, tn), lambda i, j: (MIX_A // MIX_B, j)),
            pl.BlockSpec((MIX_C, tn), lambda i, j: ((MIX_A + MIX_B) // MIX_C, j)),
            pl.BlockSpec((tm, tn), lambda i, j: (i, j)),
            pl.BlockSpec((tm, tn), lambda i, j: (i, nj + j)),
            pl.BlockSpec((tm, tn), lambda i, j: (i, 2 * nj + j)),
        ],
        out_specs=pl.BlockSpec((tm, tn), lambda i, j: (i, j)),
        out_shape=jax.ShapeDtypeStruct((m, D_MODEL), BF16),
        compiler_params=_params("arbitrary", "arbitrary"),
        name="merge_branches",
    )(oa, ob, oc, w_branch, w_branch, w_branch, gates, gates, gates)


def _swiglu_kernel(x_ref, wg_ref, wu_ref, o_ref):
    x = x_ref[...]
    g = _dot(x, wg_ref[...])
    u = _dot(x, wu_ref[...])
    o_ref[...] = (g * _sigmoid(g) * u).astype(o_ref.dtype)


def swiglu_up(h, w_gate_up, tm=SWIGLU_TM, tn=SWIGLU_TN):
    m, k = h.shape
    nj = D_FF // tn
    return pl.pallas_call(
        _swiglu_kernel,
        grid=(m // tm, nj),
        in_specs=[
            pl.BlockSpec((tm, k), lambda i, j: (i, 0), pipeline_mode=pl.Buffered(1)),
            pl.BlockSpec((k, tn), lambda i, j: (0, j)),
            pl.BlockSpec((k, tn), lambda i, j: (0, nj + j)),
        ],
        out_specs=pl.BlockSpec((tm, tn), lambda i, j: (i, j)),
        out_shape=jax.ShapeDtypeStruct((m, D_FF), BF16),
        compiler_params=_params("arbitrary", "arbitrary"),
        name="swiglu_up",
    )(h, w_gate_up, w_gate_up)


def _diff_lambda(lam_ref, layer):
    lam_init = 0.8 - 0.6 * math.exp(-0.3 * layer)
    lf = lam_ref[...]
    a = jnp.sum(lf[0:1, :] * lf[1:2, :], axis=-1, keepdims=True)
    b = jnp.sum(lf[2:3, :] * lf[3:4, :], axis=-1, keepdims=True)
    return jnp.exp(a) - jnp.exp(b) + lam_init, lam_init


def _diff_finish(o, gain_ref, lam_init):
    o = o * lax.rsqrt(jnp.mean(o * o, axis=-1, keepdims=True) + RMS_EPS)
    return o * gain_ref[...] * (1.0 - lam_init)


def _split_halves(q):
    lane = lax.broadcasted_iota(jnp.int32, (1, HEAD_DIM), 1)
    zero = jnp.zeros_like(q)
    return jnp.where(lane < DB_HALF, q, zero), jnp.where(lane >= DB_HALF, q, zero)


def _sink_column(sink_ref, n, t):
    row = lax.broadcasted_iota(jnp.int32, (G_C * t, 1), 0)
    col = jnp.full((G_C * t, 1), sink_ref[n * G_C + G_C - 1], F32)
    for g in range(G_C - 2, -1, -1):
        col = jnp.where(row < (g + 1) * t, sink_ref[n * G_C + g], col)
    return col


def _softmax_parts(parts, extra=None, exp=jnp.exp):
    m = functools.reduce(jnp.maximum, [jnp.max(s, axis=-1, keepdims=True) for s in parts])
    if extra is not None:
        m = jnp.maximum(m, extra)
    ps = [exp(s - m) for s in parts]
    l = functools.reduce(jnp.add, [jnp.sum(p, axis=-1, keepdims=True) for p in ps])
    if extra is not None:
        l = l + exp(extra - m)
    return ps, l


def _context_attn_kernel(sink_ref, z_ref, lam_ref, gain_ref, o_ref, *, layer):
    scale = HEAD_DIM ** -0.5

    def col(off, h, width=HEAD_DIM):
        return z_ref[:, off + h * width: off + (h + 1) * width]

    for h in range(H_A):
        q, k, v = col(OFF_QA, h).astype(BF16), col(OFF_KA, h).astype(BF16), col(OFF_VA, h).astype(BF16)
        (p,), l = _softmax_parts([_dot_nt(q, k) * scale])
        o = _dot((p / l).astype(BF16), v)
        o_ref[:, h * HEAD_DIM:(h + 1) * HEAD_DIM] = o.astype(o_ref.dtype)

    lam, lam_init = _diff_lambda(lam_ref, layer)
    for h in range(H_B):
        q, k, v = col(OFF_QB, h).astype(BF16), col(OFF_KB, h).astype(BF16), col(OFF_VB, h).astype(BF16)
        q1, q2 = _split_halves(q)
        (p0,), l0 = _softmax_parts([_dot_nt(q1, k) * (DB_HALF ** -0.5)])
        (p1,), l1 = _softmax_parts([_dot_nt(q2, k) * (DB_HALF ** -0.5)])
        w = p0 / l0 - lam * (p1 / l1)
        o = _diff_finish(_dot(w.astype(BF16), v), gain_ref, lam_init)
        o_ref[:, MIX_A + h * HEAD_DIM: MIX_A + (h + 1) * HEAD_DIM] = o.astype(o_ref.dtype)

    t = z_ref.shape[0]
    for n in range(KV_C):
        q = jnp.concatenate([col(OFF_QC, n * G_C + g) for g in range(G_C)], axis=0).astype(BF16)
        k, v = col(OFF_KC, n).astype(BF16), col(OFF_VC, n).astype(BF16)
        sink = _sink_column(sink_ref, n, t)
        (p,), l = _softmax_parts([_dot_nt(q, k) * scale], extra=sink)
        o = _dot((p / l).astype(BF16), v)
        for g in range(G_C):
            c0 = MIX_A + MIX_B + (n * G_C + g) * HEAD_DIM
            o_ref[:, c0:c0 + HEAD_DIM] = o[g * t:(g + 1) * t].astype(o_ref.dtype)


def context_attention(z, seq, lam, gain, sink, layer):
    m = z.shape[0]
    return pl.pallas_call(
        functools.partial(_context_attn_kernel, layer=layer),
        grid=(m // seq,),
        in_specs=[
            pl.BlockSpec(memory_space=pltpu.SMEM),
            pl.BlockSpec((seq, QKV_W), lambda b: (b, 0)),
            pl.BlockSpec((4, DB_HALF), lambda b: (0, 0)),
            pl.BlockSpec((1, HEAD_DIM), lambda b: (0, 0)),
        ],
        out_specs=pl.BlockSpec((seq, D_MODEL), lambda b: (b, 0)),
        out_shape=jax.ShapeDtypeStruct((m, D_MODEL), BF16),
        compiler_params=_params("arbitrary"),
        name="context_attention",
    )(sink, z, lam, gain.reshape(1, HEAD_DIM))


def _na_bias_table(rpb_l):
    pad = GRID_W
    rp = jnp.pad(rpb_l.astype(F32) * LOG2E, ((0, 0), (0, 0), (pad, pad)))
    first = pad + NA_COLS - 1
    tz = jnp.stack([rp[:, :, first - qc:first - qc + GRID_W] for qc in range(GRID_W)], axis=2)
    qc, kc = np.arange(GRID_W)[:, None], np.arange(GRID_W)[None, :]
    ws = np.clip(qc - NA_COLS // 2, 0, GRID_W - NA_COLS)
    valid = (kc >= ws) & (kc < ws + NA_COLS)
    tz = jnp.where(jnp.asarray(valid)[None, None], tz, NEG_INF)
    return jnp.stack([jnp.concatenate([tz[:, e0 + j] for j in range(NA_ROWS)], axis=-1)
                      for e0 in range(NA_ROWS)], axis=1)


def _na_latent_kernel(q_ref, k_ref, v_ref, ck_ref, cv_ref, bias_ref, o_ref, *, rows):
    rb = pl.program_id(2)
    span = NA_ROWS * GRID_W
    starts, biases = [], []
    for r in range(NA_QROWS):
        r_abs = rb * NA_QROWS + r
        rs = jnp.clip(r_abs - NA_ROWS // 2, 0, rows - NA_ROWS)
        starts.append(pl.multiple_of(rs * GRID_W, GRID_W))
        biases.append(rs - r_abs + NA_ROWS - 1)
    heads = [slice(hh * HEAD_DIM, (hh + 1) * HEAD_DIM) for hh in range(NA_HEADS)]
    scores = []
    for hh, cols in enumerate(heads):
        q = q_ref[:, cols]
        s_loc = [_dot_nt(q[r * GRID_W:(r + 1) * GRID_W], k_ref[pl.ds(starts[r], span), cols]) + bias_ref[hh, biases[r]]
                 for r in range(NA_QROWS)]
        scores.append([jnp.concatenate(s_loc, axis=0), _dot_nt(q, ck_ref[:, cols].astype(BF16))])
    stats = [_softmax_parts(sc, exp=jnp.exp2) for sc in scores]
    for cols, ((p_loc, p_ctx), l) in zip(heads, stats):
        p_loc = p_loc.astype(BF16)
        o_loc = [_dot(p_loc[r * GRID_W:(r + 1) * GRID_W], v_ref[pl.ds(starts[r], span), cols]) for r in range(NA_QROWS)]
        o = jnp.concatenate(o_loc, axis=0) + _dot(p_ctx.astype(BF16), cv_ref[:, cols].astype(BF16))
        o_ref[:, cols] = (o / l).astype(o_ref.dtype)


def na_latent(z, cache_k, cache_v, layer, bias, batch, seq):
    rows = seq // GRID_W
    assert rows % NA_QROWS == 0 and rows >= NA_ROWS and H_A % NA_HEADS == 0
    nrb = rows // NA_QROWS
    tq = NA_QROWS * GRID_W
    hw = NA_HEADS * HEAD_DIM
    n_ctx = cache_k.shape[2]
    ck = cache_k.reshape(batch, DEPTH, n_ctx, MIX_A)
    cv = cache_v.reshape(batch, DEPTH, n_ctx, MIX_A)
    ctx_spec = pl.BlockSpec((None, None, n_ctx, hw), lambda b, h, rb: (b, layer, 0, h))
    return pl.pallas_call(
        functools.partial(_na_latent_kernel, rows=rows),
        grid=(batch, H_A // NA_HEADS, nrb),
        in_specs=[
            pl.BlockSpec((tq, hw), lambda b, h, rb: (b * nrb + rb, OFF_QA // hw + h)),
            pl.BlockSpec((seq, hw), lambda b, h, rb: (b, OFF_KA // hw + h)),
            pl.BlockSpec((seq, hw), lambda b, h, rb: (b, OFF_VA // hw + h)),
            ctx_spec, ctx_spec,
            pl.BlockSpec((NA_HEADS, NA_ROWS, GRID_W, NA_ROWS * GRID_W), lambda b, h, rb: (h, 0, 0, 0)),
        ],
        out_specs=pl.BlockSpec((tq, hw), lambda b, h, rb: (b * nrb + rb, h)),
        out_shape=jax.ShapeDtypeStruct((batch * seq, MIX_A), BF16),
        compiler_params=_params("arbitrary", "arbitrary", "arbitrary"),
        name="na_latent",
    )(z, z, z, ck, cv, bias)


def _diff_latent_kernel(lam_ref, gain_ref, q_ref, k_ref, v_ref, ck_ref, cv_ref, o_ref, *, layer):
    lam, lam_init = _diff_lambda(lam_ref, layer)
    k = k_ref[...]
    ck = ck_ref[...].astype(BF16)
    cv = cv_ref[...].astype(BF16)
    nsub = q_ref.shape[0] // DIFF_TQ
    rows = [slice(u * DIFF_TQ, (u + 1) * DIFF_TQ) for u in range(nsub)]
    scores = [[[_dot_nt(qh, k), _dot_nt(qh, ck)] for qh in _split_halves(q_ref[r, :])] for r in rows]
    stats = [[_softmax_parts(sc, exp=jnp.exp2) for sc in su] for su in scores]
    for r, ((p0, l0), (p1, l1)) in zip(rows, stats):
        c = lam * l0 / l1
        wl = (p0[0] - p1[0] * c).astype(BF16)
        wc = (p0[1] - p1[1] * c).astype(BF16)
        o = (_dot(wl, v_ref[...]) + _dot(wc, cv)) / l0
        o_ref[r, :] = _diff_finish(o, gain_ref, lam_init).astype(o_ref.dtype)


def diff_latent(z, cache_k, cache_v, layer, lam, gain, batch, seq):
    tq = DIFF_TQ * DIFF_NSUB
    n_ctx = cache_k.shape[2]
    ck = cache_k.reshape(batch, DEPTH, n_ctx, MIX_B)
    cv = cache_v.reshape(batch, DEPTH, n_ctx, MIX_B)
    nqt = seq // tq
    ctx_spec = pl.BlockSpec((None, None, n_ctx, HEAD_DIM), lambda b, h, t: (b, layer, 0, h))
    return pl.pallas_call(
        functools.partial(_diff_latent_kernel, layer=layer),
        grid=(batch, H_B, nqt),
        in_specs=[
            pl.BlockSpec((4, DB_HALF), lambda b, h, t: (0, 0)),
            pl.BlockSpec((1, HEAD_DIM), lambda b, h, t: (0, 0)),
            pl.BlockSpec((tq, HEAD_DIM), lambda b, h, t: (b * nqt + t, OFF_QB // HEAD_DIM + h)),
            pl.BlockSpec((seq, HEAD_DIM), lambda b, h, t: (b, OFF_KB // HEAD_DIM + h)),
            pl.BlockSpec((seq, HEAD_DIM), lambda b, h, t: (b, OFF_VB // HEAD_DIM + h)),
            ctx_spec, ctx_spec,
        ],
        out_specs=pl.BlockSpec((tq, HEAD_DIM), lambda b, h, t: (b * nqt + t, h)),
        out_shape=jax.ShapeDtypeStruct((batch * seq, MIX_B), BF16),
        compiler_params=_params("arbitrary", "arbitrary", "arbitrary"),
        name="diff_latent",
    )(lam, gain.reshape(1, HEAD_DIM), z, z, z, ck, cv)


def _window_mask_table():
    i, j = np.arange(SW_SUB)[:, None], np.arange(SW_SPAN)[None, :]
    tabs = [np.where(np.abs(c * WINDOW_C + i - j) <= WINDOW_C, 0.0, NEG_INF) for c in range(3)]
    return jnp.asarray(np.stack(tabs), F32)


def _window_latent_kernel(sink_ref, mask_ref, q_ref, k_ref, v_ref, ck_ref, cv_ref, o_ref, *, seq):
    n = pl.program_id(1)
    t0 = pl.program_id(2) * SW_TQ
    ck = ck_ref[...].astype(BF16)
    cv = cv_ref[...].astype(BF16)
    sink = _sink_column(sink_ref, n, SW_SUB) * LOG2E
    nsub = SW_TQ // SW_SUB
    rows = [slice(u * SW_SUB, (u + 1) * SW_SUB) for u in range(nsub)]
    starts, scores = [], []
    for u in range(nsub):
        tu = t0 + u * SW_SUB
        ks = pl.multiple_of(jnp.clip(tu - WINDOW_C, 0, seq - SW_SPAN), WINDOW_C)
        case = (tu - ks) // WINDOW_C
        q = jnp.concatenate([q_ref[rows[u], g * HEAD_DIM:(g + 1) * HEAD_DIM] for g in range(G_C)], axis=0)
        s_loc = _dot_nt(q, k_ref[pl.ds(ks, SW_SPAN), :]).reshape(G_C, SW_SUB, SW_SPAN) + mask_ref[case][None]
        starts.append(ks)
        scores.append([s_loc.reshape(G_C * SW_SUB, SW_SPAN), _dot_nt(q, ck)])
    stats = [_softmax_parts(sc, extra=sink, exp=jnp.exp2) for sc in scores]
    for u, ((p_loc, p_ctx), l) in enumerate(stats):
        o = (_dot(p_loc.astype(BF16), v_ref[pl.ds(starts[u], SW_SPAN), :]) + _dot(p_ctx.astype(BF16), cv)) / l
        for g in range(G_C):
            o_ref[rows[u], g * HEAD_DIM:(g + 1) * HEAD_DIM] = o[g * SW_SUB:(g + 1) * SW_SUB].astype(o_ref.dtype)


def window_latent(z, cache_k, cache_v, layer, sink, batch, seq):
    assert seq % SW_TQ == 0 and seq >= SW_SPAN and SW_TQ % SW_SUB == 0
    n_ctx = cache_k.shape[2]
    ck = cache_k.reshape(batch, DEPTH, n_ctx, KVW_C)
    cv = cache_v.reshape(batch, DEPTH, n_ctx, KVW_C)
    nqt = seq // SW_TQ
    gw = G_C * HEAD_DIM
    ctx_spec = pl.BlockSpec((None, None, n_ctx, HEAD_DIM), lambda b, n, t: (b, layer, 0, n))
    return pl.pallas_call(
        functools.partial(_window_latent_kernel, seq=seq),
        grid=(batch, KV_C, nqt),
        in_specs=[
            pl.BlockSpec(memory_space=pltpu.SMEM),
            pl.BlockSpec((3, SW_SUB, SW_SPAN), lambda b, n, t: (0, 0, 0)),
            pl.BlockSpec((SW_TQ, gw), lambda b, n, t: (b * nqt + t, OFF_QC // gw + n)),
            pl.BlockSpec((seq, HEAD_DIM), lambda b, n, t: (b, OFF_KC // HEAD_DIM + n)),
            pl.BlockSpec((seq, HEAD_DIM), lambda b, n, t: (b, OFF_VC // HEAD_DIM + n)),
            ctx_spec, ctx_spec,
        ],
        out_specs=pl.BlockSpec((SW_TQ, gw), lambda b, n, t: (b * nqt + t, n)),
        out_shape=jax.ShapeDtypeStruct((batch * seq, MIX_C), BF16),
        compiler_params=_params("arbitrary", "arbitrary", "arbitrary"),
        name="window_latent",
    )(sink, _window_mask_table(), z, z, z, ck, cv)


def _prompt_row(tm, i):
    return 0


def kernel(x_prompt, x_sample, cache_a_k, cache_a_v, cache_b_k, cache_b_v, cache_c_k, cache_c_v, c, c_ctx,
           w_ada, b_ada, w_in, rpb_a, lam_b, subln_b, sink_c, w_branch, w_out, ln_g, ln_b, w_gate_up, w_down):
    bp, tp, _ = x_prompt.shape
    bs, ts, _ = x_sample.shape

    def sample_row(tm, i):
        return 1 + (i * tm) // ts

    streams = {
        "p": dict(x=x_prompt.reshape(bp * tp, D_MODEL), rowfn=_prompt_row),
        "s": dict(x=x_sample.reshape(bs * ts, D_MODEL), rowfn=sample_row),
    }
    cc = jnp.concatenate([c_ctx[None, :], c], axis=0)
    cc = jnp.pad(cc, ((0, (-cc.shape[0]) % 8), (0, 0)))
    mod = ada_modulation_all(cc, w_ada, b_ada)

    for st in streams.values():
        st["h"] = modulate_cast(st["x"], mod, 0, 0, st["rowfn"])

    new_kv = [[] for _ in range(6)]
    for l in range(DEPTH):
        w_qkv = cast_weight(w_in, l, 0, QKV_W)
        w_g = cast_weight(w_in, l, QKV_W)
        w_br = cast_weight(w_branch, l)
        w_o = cast_weight(w_out, l)
        w_gu = cast_weight(w_gate_up, l)
        w_dn = cast_weight(w_down, l)
        na_bias = _na_bias_table(rpb_a[l])

        sp = streams["p"]
        zp = matmul(sp["h"], w_qkv, F32, PROJ_TILE, PROJ_TILE, name="qkv_context")
        for lst, off, wdt, shp in (
                (new_kv[0], OFF_KA, MIX_A, (H_A, HEAD_DIM)), (new_kv[1], OFF_VA, MIX_A, (H_A, HEAD_DIM)),
                (new_kv[2], OFF_KB, MIX_B, (H_B, 2, DB_HALF)), (new_kv[3], OFF_VB, MIX_B, (H_B, HEAD_DIM)),
                (new_kv[4], OFF_KC, KVW_C, (KV_C, HEAD_DIM)), (new_kv[5], OFF_VC, KVW_C, (KV_C, HEAD_DIM))):
            lst.append(zp[:, off:off + wdt].reshape((bp, tp) + shp))
        op = context_attention(zp, tp, lam_b[l], subln_b[l], sink_c[l], l)
        sp["o"] = ((op, 0), (op, MIX_A // MIX_B), (op, (MIX_A + MIX_B) // MIX_C))

        ss = streams["s"]
        zs = qkv_latent(ss["h"], w_qkv, ts)
        oa = na_latent(zs, cache_a_k, cache_a_v, l, na_bias, bs, ts)
        ob = diff_latent(zs, cache_b_k, cache_b_v, l, lam_b[l], subln_b[l], bs, ts)
        oc = window_latent(zs, cache_c_k, cache_c_v, l, sink_c[l], bs, ts)
        ss["o"] = ((oa, 0), (ob, 0), (oc, 0))

        for st in streams.values():
            gates = matmul(st["h"], w_g, BF16, PROJ_TILE, PROJ_TILE, sigmoid=True, name="branch_gates")
            u = merge_branches(st["o"], gates, w_br)
            v = matmul_residual(u, w_o, st["x"], mod, l, 0, st["rowfn"], PROJ_TILE, PROJ_TILE, "mixer_out")
            st["x"], h2 = post_norm(v, mod, ln_g[l, 0], ln_b[l, 0], st["rowfn"], next_mod=(l, 1))
            a = swiglu_up(h2, w_gu)
            v = matmul_residual(a, w_dn, st["x"], mod, l, 1, st["rowfn"], DOWN_TILE, DOWN_TILE, "ffn_down")
            nxt = (l + 1, 0) if l + 1 < DEPTH else None
            st["x"], st["h"] = post_norm(v, mod, ln_g[l, 1], ln_b[l, 1], st["rowfn"], next_mod=nxt)

    outs = [jnp.stack(lst, axis=1) for lst in new_kv]
    return (streams["p"]["x"].reshape(bp, tp, D_MODEL), streams["s"]["x"].reshape(bs, ts, D_MODEL), *outs)
```

```python
import functools
import math

import numpy as np
import jax
import jax.numpy as jnp
from jax import lax
from jax.experimental import pallas as pl
from jax.experimental.pallas import tpu as pltpu

D_MODEL = 4096
DEPTH = 2
GRID_W = 64
HEAD_DIM = 128
H_A = 8
MIX_A = H_A * HEAD_DIM
NA_ROWS = 8
NA_COLS = 16
H_B = 8
DB_HALF = HEAD_DIM // 2
MIX_B = H_B * HEAD_DIM
H_C = 16
KV_C = 4
G_C = H_C // KV_C
MIX_C = H_C * HEAD_DIM
KVW_C = KV_C * HEAD_DIM
WINDOW_C = 128
N_BRANCH = 3
QKV_W = 3 * MIX_A + 3 * MIX_B + MIX_C + 2 * KVW_C
D_FF = 11008
ROPE_BASE = 10000.0
LN_EPS = 1e-5
RMS_EPS = 1e-5
ALPHA = (2 * DEPTH) ** 0.25
NEG_INF = -1e30
LOG2E = math.log2(math.e)
BF16 = jnp.bfloat16
F32 = jnp.float32

OFF_QA, OFF_KA, OFF_VA = 0, MIX_A, 2 * MIX_A
OFF_QB, OFF_KB, OFF_VB = 3 * MIX_A, 3 * MIX_A + MIX_B, 3 * MIX_A + 2 * MIX_B
OFF_QC = 3 * MIX_A + 3 * MIX_B
OFF_KC = OFF_QC + MIX_C
OFF_VC = OFF_KC + KVW_C

V7X_VMEM_LIMIT = 56 * 1024 * 1024
V7X_MXU_COLS = 256

PROJ_TILE = 1024
SWIGLU_TM, SWIGLU_TN = 4096, V7X_MXU_COLS
DOWN_TILE = 512
MODULATE_TM, NORM_TM = 512, 512

NA_QROWS = 16
NA_HEADS = 4
SW_TQ = 2048
SW_SUB = WINDOW_C
SW_SPAN = 3 * SW_SUB
DIFF_TQ = 128
DIFF_NSUB = 8


def _params(*sem):
    return pltpu.CompilerParams(dimension_semantics=sem, vmem_limit_bytes=V7X_VMEM_LIMIT)


def _dot(a, b):
    return jnp.dot(a, b, preferred_element_type=F32)


def _dot_nt(a, b):
    return lax.dot_general(a, b, (((1,), (1,)), ((), ())), preferred_element_type=F32)


def _sigmoid(x):
    return 0.5 * jnp.tanh(0.5 * x) + 0.5


def _cast_kernel(w_ref, o_ref):
    o_ref[...] = w_ref[...].astype(o_ref.dtype)


def cast_weight(w, layer, col0=0, ncols=None):
    _, r, c = w.shape
    ncols = c - col0 if ncols is None else ncols
    tc = 512 if r <= D_MODEL else HEAD_DIM
    assert col0 % tc == 0 and ncols % tc == 0
    return pl.pallas_call(
        _cast_kernel,
        grid=(ncols // tc,),
        in_specs=[pl.BlockSpec((None, r, tc), lambda j: (layer, 0, col0 // tc + j))],
        out_specs=pl.BlockSpec((r, tc), lambda j: (0, j)),
        out_shape=jax.ShapeDtypeStruct((r, ncols), BF16),
        compiler_params=_params("arbitrary"),
        name="cast_weight",
    )(w)


def _ada_kernel(c_ref, w_ref, b_ref, o_ref):
    cv = c_ref[...]
    s = (cv * _sigmoid(cv)).astype(BF16)
    o_ref[...] = _dot(s, w_ref[...].astype(BF16)) + b_ref[...]


def ada_modulation_all(cc, w_ada, b_ada):
    r = cc.shape[0]
    n = 6 * D_MODEL
    tn = 512
    out = pl.pallas_call(
        _ada_kernel,
        grid=(DEPTH, n // tn),
        in_specs=[
            pl.BlockSpec((r, D_MODEL), lambda l, j: (0, 0)),
            pl.BlockSpec((None, D_MODEL, tn), lambda l, j: (l, 0, j)),
            pl.BlockSpec((None, 1, tn), lambda l, j: (l, 0, j)),
        ],
        out_specs=pl.BlockSpec((None, r, tn), lambda l, j: (l, 0, j)),
        out_shape=jax.ShapeDtypeStruct((DEPTH, r, n), F32),
        compiler_params=_params("arbitrary", "arbitrary"),
        name="ada_modulation",
    )(cc, w_ada, b_ada.reshape(DEPTH, 1, n))
    return out.reshape(DEPTH, r, 6, 1, D_MODEL)


def _mod_spec(layer, chunk, rowfn):
    return pl.BlockSpec((None, None, None, 1, D_MODEL), lambda i: (layer, rowfn(i), chunk, 0, 0))


def _modulate_kernel(x_ref, sc_ref, sh_ref, o_ref):
    o_ref[...] = (x_ref[...] * (1.0 + sc_ref[...]) + sh_ref[...]).astype(BF16)


def modulate_cast(x, mod, layer, idx, rowfn_of_tile, tm=MODULATE_TM):
    m = x.shape[0]
    rowfn = functools.partial(rowfn_of_tile, tm)
    return pl.pallas_call(
        _modulate_kernel,
        grid=(m // tm,),
        in_specs=[
            pl.BlockSpec((tm, D_MODEL), lambda i: (i, 0)),
            _mod_spec(layer, 3 * idx + 1, rowfn),
            _mod_spec(layer, 3 * idx, rowfn),
        ],
        out_specs=pl.BlockSpec((tm, D_MODEL), lambda i: (i, 0)),
        out_shape=jax.ShapeDtypeStruct((m, D_MODEL), BF16),
        compiler_params=_params("arbitrary"),
        name="modulate",
    )(x, mod, mod)


def _ln_core(v_ref, g_ref, b_ref):
    v = v_ref[...]
    mu = jnp.mean(v, axis=-1, keepdims=True)
    d = v - mu
    var = jnp.mean(d * d, axis=-1, keepdims=True)
    return d * lax.rsqrt(var + LN_EPS) * g_ref[...] + b_ref[...]


def _ln_mod_kernel(v_ref, g_ref, b_ref, sc_ref, sh_ref, xo_ref, ho_ref):
    xn = _ln_core(v_ref, g_ref, b_ref)
    xo_ref[...] = xn
    ho_ref[...] = (xn * (1.0 + sc_ref[...]) + sh_ref[...]).astype(BF16)


def _ln_kernel(v_ref, g_ref, b_ref, xo_ref):
    xo_ref[...] = _ln_core(v_ref, g_ref, b_ref)


def post_norm(v, mod, ln_g, ln_b, rowfn_of_tile, next_mod=None, tm=NORM_TM):
    m = v.shape[0]
    rowfn = functools.partial(rowfn_of_tile, tm)
    row_spec = pl.BlockSpec((tm, D_MODEL), lambda i: (i, 0))
    vec_spec = pl.BlockSpec((1, D_MODEL), lambda i: (0, 0))
    in_specs = [row_spec, vec_spec, vec_spec]
    args = [v, ln_g.reshape(1, D_MODEL), ln_b.reshape(1, D_MODEL)]
    if next_mod is None:
        return pl.pallas_call(
            _ln_kernel, grid=(m // tm,), in_specs=in_specs, out_specs=row_spec,
            out_shape=jax.ShapeDtypeStruct((m, D_MODEL), F32),
            compiler_params=_params("arbitrary"), name="post_norm",
        )(*args), None
    nl, nidx = next_mod
    in_specs += [_mod_spec(nl, 3 * nidx + 1, rowfn), _mod_spec(nl, 3 * nidx, rowfn)]
    args += [mod, mod]
    return pl.pallas_call(
        _ln_mod_kernel, grid=(m // tm,), in_specs=in_specs, out_specs=[row_spec, row_spec],
        out_shape=[jax.ShapeDtypeStruct((m, D_MODEL), F32), jax.ShapeDtypeStruct((m, D_MODEL), BF16)],
        compiler_params=_params("arbitrary"), name="post_norm_modulate",
    )(*args)


def _mm_kernel(x_ref, w_ref, o_ref, *, sigmoid):
    acc = _dot(x_ref[...], w_ref[...])
    if sigmoid:
        acc = _sigmoid(acc)
    o_ref[...] = acc.astype(o_ref.dtype)


def matmul(x, w, out_dtype, tm, tn, sigmoid=False, name="matmul"):
    m, k = x.shape
    n = w.shape[1]
    return pl.pallas_call(
        functools.partial(_mm_kernel, sigmoid=sigmoid),
        grid=(m // tm, n // tn),
        in_specs=[pl.BlockSpec((tm, k), lambda i, j: (i, 0)), pl.BlockSpec((k, tn), lambda i, j: (0, j))],
        out_specs=pl.BlockSpec((tm, tn), lambda i, j: (i, j)),
        out_shape=jax.ShapeDtypeStruct((m, n), out_dtype),
        compiler_params=_params("arbitrary", "arbitrary"),
        name=name,
    )(x, w)


def _mm_residual_kernel(x_ref, w_ref, res_ref, gate_ref, o_ref):
    o_ref[...] = ALPHA * res_ref[...] + gate_ref[...] * _dot(x_ref[...], w_ref[...])


def matmul_residual(x, w, res, mod, layer, idx, rowfn_of_tile, tm, tn, name):
    m, k = x.shape
    n = w.shape[1]
    rowfn = functools.partial(rowfn_of_tile, tm)
    gate_spec = pl.BlockSpec((None, None, None, 1, tn), lambda i, j: (layer, rowfn(i), 3 * idx + 2, 0, j))
    tile_spec = pl.BlockSpec((tm, tn), lambda i, j: (i, j))
    return pl.pallas_call(
        _mm_residual_kernel,
        grid=(m // tm, n // tn),
        in_specs=[pl.BlockSpec((tm, k), lambda i, j: (i, 0)), pl.BlockSpec((k, tn), lambda i, j: (0, j)),
                  tile_spec, gate_spec],
        out_specs=tile_spec,
        out_shape=jax.ShapeDtypeStruct((m, n), F32),
        compiler_params=_params("arbitrary", "arbitrary"),
        name=name,
    )(x, w, res, mod)


def _rope_tables(t, width):
    half, quarter = width // 2, width // 4
    lane = np.arange(HEAD_DIM)
    e = lane % width
    pos = np.arange(t)
    p = np.where((e < half)[None, :], (pos // GRID_W)[:, None], (pos % GRID_W)[:, None]).astype(np.float64)
    inv = np.exp(-math.log(ROPE_BASE) * (e % quarter).astype(np.float64) / quarter)
    ang = (p.astype(np.float32) * inv.astype(np.float32)[None, :]).astype(np.float64)
    first = ((e % half) < quarter)[None, :]
    cos, sin = np.cos(ang), np.sin(ang)
    return jnp.asarray(cos, F32), jnp.asarray(np.where(first, -sin, sin), F32)


def _rope(x, cos, ssin, quarter):
    partner = lax.broadcasted_iota(jnp.int32, x.shape, 1) ^ quarter
    return x * cos + jnp.take_along_axis(x, partner, axis=1) * ssin


def _qkv_latent_kernel(x_ref, w_ref, cb_ref, sb_ref, cc_ref, sc_ref, o_ref, *, tn):
    j = pl.program_id(1)
    acc = _dot(x_ref[...], w_ref[...])
    nchunk = tn // HEAD_DIM
    tiles_qb = (OFF_QB // tn,)
    tiles_kb = (OFF_KB // tn,)
    tiles_qc = tuple(range(OFF_QC // tn, OFF_KC // tn))
    tile_kcvc = OFF_KC // tn

    def is_in(tiles):
        c = j == tiles[0]
        for tt in tiles[1:]:
            c = jnp.logical_or(c, j == tt)
        return c

    def store(fn_of_chunk):
        for ch in range(nchunk):
            sl = slice(ch * HEAD_DIM, (ch + 1) * HEAD_DIM)
            o_ref[:, sl] = fn_of_chunk(ch, acc[:, sl]).astype(o_ref.dtype)

    rope_b = lambda x: _rope(x, cb_ref[...], sb_ref[...], DB_HALF // 4)
    rope_c = lambda x: _rope(x, cc_ref[...], sc_ref[...], HEAD_DIM // 4)
    tiles_qa = tuple(range(OFF_QA // tn, OFF_KA // tn))
    special = is_in(tiles_qa + tiles_qb + tiles_kb + tiles_qc + (tile_kcvc,))

    @pl.when(jnp.logical_not(special))
    def _():
        o_ref[...] = acc.astype(o_ref.dtype)

    @pl.when(is_in(tiles_qa))
    def _():
        o_ref[...] = (acc * (HEAD_DIM ** -0.5 * LOG2E)).astype(o_ref.dtype)

    @pl.when(is_in(tiles_qb))
    def _():
        store(lambda ch, x: rope_b(x) * (DB_HALF ** -0.5 * LOG2E))

    @pl.when(is_in(tiles_kb))
    def _():
        store(lambda ch, x: rope_b(x))

    @pl.when(is_in(tiles_qc))
    def _():
        store(lambda ch, x: rope_c(x) * (HEAD_DIM ** -0.5 * LOG2E))

    @pl.when(j == tile_kcvc)
    def _():
        store(lambda ch, x: rope_c(x) if ch < KVW_C // HEAD_DIM else x)


def qkv_latent(h, w_qkv, seq, tm=PROJ_TILE, tn=PROJ_TILE):
    m, k = h.shape
    assert tn == MIX_A and OFF_KC % tn == 0 and (OFF_VC - OFF_KC) == KVW_C and seq % tm == 0
    tabs = _rope_tables(seq, DB_HALF) + _rope_tables(seq, HEAD_DIM)
    per_seq = seq // tm
    tab_spec = pl.BlockSpec((tm, HEAD_DIM), lambda i, j: (i % per_seq, 0))
    return pl.pallas_call(
        functools.partial(_qkv_latent_kernel, tn=tn),
        grid=(m // tm, QKV_W // tn),
        in_specs=[pl.BlockSpec((tm, k), lambda i, j: (i, 0)), pl.BlockSpec((k, tn), lambda i, j: (0, j))]
        + [tab_spec] * len(tabs),
        out_specs=pl.BlockSpec((tm, tn), lambda i, j: (i, j)),
        out_shape=jax.ShapeDtypeStruct((m, QKV_W), BF16),
        compiler_params=_params("arbitrary", "arbitrary"),
        name="qkv_latent",
    )(h, w_qkv, *tabs)


def _merge_kernel(oa_ref, ob_ref, oc_ref, wa_ref, wb_ref, wc_ref, ga_ref, gb_ref, gc_ref, o_ref):
    y = ga_ref[...].astype(F32) * _dot(oa_ref[...], wa_ref[...])
    y += gb_ref[...].astype(F32) * _dot(ob_ref[...], wb_ref[...])
    y += gc_ref[...].astype(F32) * _dot(oc_ref[...], wc_ref[...])
    o_ref[...] = y.astype(o_ref.dtype)


def merge_branches(o_parts, gates, w_branch, tm=PROJ_TILE, tn=PROJ_TILE):
    (oa, ja), (ob, jb), (oc, jc) = o_parts
    m = gates.shape[0]
    nj = D_MODEL // tn
    return pl.pallas_call(
        _merge_kernel,
        grid=(m // tm, nj),
        in_specs=[
            pl.BlockSpec((tm, MIX_A), lambda i, j: (i, ja)),
            pl.BlockSpec((tm, MIX_B), lambda i, j: (i, jb)),
            pl.BlockSpec((tm, MIX_C), lambda i, j: (i, jc)),
            pl.BlockSpec((MIX_A, tn), lambda i, j: (0, j)),
            pl.BlockSpec((MIX_B, tn), lambda i, j: (MIX_A // MIX_B, j)),
            pl.BlockSpec((MIX_C, tn), lambda i, j: ((MIX_A + MIX_B) // MIX_C, j)),
            pl.BlockSpec((tm, tn), lambda i, j: (i, j)),
            pl.BlockSpec((tm, tn), lambda i, j: (i, nj + j)),
            pl.BlockSpec((tm, tn), lambda i, j: (i, 2 * nj + j)),
        ],
        out_specs=pl.BlockSpec((tm, tn), lambda i, j: (i, j)),
        out_shape=jax.ShapeDtypeStruct((m, D_MODEL), BF16),
        compiler_params=_params("arbitrary", "arbitrary"),
        name="merge_branches",
    )(oa, ob, oc, w_branch, w_branch, w_branch, gates, gates, gates)


def _swiglu_kernel(x_ref, wg_ref, wu_ref, o_ref):
    x = x_ref[...]
    g = _dot(x, wg_ref[...])
    u = _dot(x, wu_ref[...])
    o_ref[...] = (g * _sigmoid(g) * u).astype(o_ref.dtype)


def swiglu_up(h, w_gate_up, tm=SWIGLU_TM, tn=SWIGLU_TN):
    m, k = h.shape
    nj = D_FF // tn
    return pl.pallas_call(
        _swiglu_kernel,
        grid=(m // tm, nj),
        in_specs=[
            pl.BlockSpec((tm, k), lambda i, j: (i, 0), pipeline_mode=pl.Buffered(1)),
            pl.BlockSpec((k, tn), lambda i, j: (0, j)),
            pl.BlockSpec((k, tn), lambda i, j: (0, nj + j)),
        ],
        out_specs=pl.BlockSpec((tm, tn), lambda i, j: (i, j)),
        out_shape=jax.ShapeDtypeStruct((m, D_FF), BF16),
        compiler_params=_params("arbitrary", "arbitrary"),
        name="swiglu_up",
    )(h, w_gate_up, w_gate_up)


def _diff_lambda(lam_ref, layer):
    lam_init = 0.8 - 0.6 * math.exp(-0.3 * layer)
    lf = lam_ref[...]
    a = jnp.sum(lf[0:1, :] * lf[1:2, :], axis=-1, keepdims=True)
    b = jnp.sum(lf[2:3, :] * lf[3:4, :], axis=-1, keepdims=True)
    return jnp.exp(a) - jnp.exp(b) + lam_init, lam_init


def _diff_finish(o, gain_ref, lam_init):
    o = o * lax.rsqrt(jnp.mean(o * o, axis=-1, keepdims=True) + RMS_EPS)
    return o * gain_ref[...] * (1.0 - lam_init)


def _split_halves(q):
    lane = lax.broadcasted_iota(jnp.int32, (1, HEAD_DIM), 1)
    zero = jnp.zeros_like(q)
    return jnp.where(lane < DB_HALF, q, zero), jnp.where(lane >= DB_HALF, q, zero)


def _sink_column(sink_ref, n, t):
    row = lax.broadcasted_iota(jnp.int32, (G_C * t, 1), 0)
    col = jnp.full((G_C * t, 1), sink_ref[n * G_C + G_C - 1], F32)
    for g in range(G_C - 2, -1, -1):
        col = jnp.where(row < (g + 1) * t, sink_ref[n * G_C + g], col)
    return col


def _softmax_parts(parts, extra=None, exp=jnp.exp):
    m = functools.reduce(jnp.maximum, [jnp.max(s, axis=-1, keepdims=True) for s in parts])
    if extra is not None:
        m = jnp.maximum(m, extra)
    ps = [exp(s - m) for s in parts]
    l = functools.reduce(jnp.add, [jnp.sum(p, axis=-1, keepdims=True) for p in ps])
    if extra is not None:
        l = l + exp(extra - m)
    return ps, l


def _context_attn_kernel(sink_ref, z_ref, lam_ref, gain_ref, o_ref, *, layer):
    scale = HEAD_DIM ** -0.5

    def col(off, h, width=HEAD_DIM):
        return z_ref[:, off + h * width: off + (h + 1) * width]

    for h in range(H_A):
        q, k, v = col(OFF_QA, h).astype(BF16), col(OFF_KA, h).astype(BF16), col(OFF_VA, h).astype(BF16)
        (p,), l = _softmax_parts([_dot_nt(q, k) * scale])
        o = _dot((p / l).astype(BF16), v)
        o_ref[:, h * HEAD_DIM:(h + 1) * HEAD_DIM] = o.astype(o_ref.dtype)

    lam, lam_init = _diff_lambda(lam_ref, layer)
    for h in range(H_B):
        q, k, v = col(OFF_QB, h).astype(BF16), col(OFF_KB, h).astype(BF16), col(OFF_VB, h).astype(BF16)
        q1, q2 = _split_halves(q)
        (p0,), l0 = _softmax_parts([_dot_nt(q1, k) * (DB_HALF ** -0.5)])
        (p1,), l1 = _softmax_parts([_dot_nt(q2, k) * (DB_HALF ** -0.5)])
        w = p0 / l0 - lam * (p1 / l1)
        o = _diff_finish(_dot(w.astype(BF16), v), gain_ref, lam_init)
        o_ref[:, MIX_A + h * HEAD_DIM: MIX_A + (h + 1) * HEAD_DIM] = o.astype(o_ref.dtype)

    t = z_ref.shape[0]
    for n in range(KV_C):
        q = jnp.concatenate([col(OFF_QC, n * G_C + g) for g in range(G_C)], axis=0).astype(BF16)
        k, v = col(OFF_KC, n).astype(BF16), col(OFF_VC, n).astype(BF16)
        sink = _sink_column(sink_ref, n, t)
        (p,), l = _softmax_parts([_dot_nt(q, k) * scale], extra=sink)
        o = _dot((p / l).astype(BF16), v)
        for g in range(G_C):
            c0 = MIX_A + MIX_B + (n * G_C + g) * HEAD_DIM
            o_ref[:, c0:c0 + HEAD_DIM] = o[g * t:(g + 1) * t].astype(o_ref.dtype)


def context_attention(z, seq, lam, gain, sink, layer):
    m = z.shape[0]
    return pl.pallas_call(
        functools.partial(_context_attn_kernel, layer=layer),
        grid=(m // seq,),
        in_specs=[
            pl.BlockSpec(memory_space=pltpu.SMEM),
            pl.BlockSpec((seq, QKV_W), lambda b: (b, 0)),
            pl.BlockSpec((4, DB_HALF), lambda b: (0, 0)),
            pl.BlockSpec((1, HEAD_DIM), lambda b: (0, 0)),
        ],
        out_specs=pl.BlockSpec((seq, D_MODEL), lambda b: (b, 0)),
        out_shape=jax.ShapeDtypeStruct((m, D_MODEL), BF16),
        compiler_params=_params("arbitrary"),
        name="context_attention",
    )(sink, z, lam, gain.reshape(1, HEAD_DIM))


def _na_bias_table(rpb_l):
    pad = GRID_W
    rp = jnp.pad(rpb_l.astype(F32) * LOG2E, ((0, 0), (0, 0), (pad, pad)))
    first = pad + NA_COLS - 1
    tz = jnp.stack([rp[:, :, first - qc:first - qc + GRID_W] for qc in range(GRID_W)], axis=2)
    qc, kc = np.arange(GRID_W)[:, None], np.arange(GRID_W)[None, :]
    ws = np.clip(qc - NA_COLS // 2, 0, GRID_W - NA_COLS)
    valid = (kc >= ws) & (kc < ws + NA_COLS)
    tz = jnp.where(jnp.asarray(valid)[None, None], tz, NEG_INF)
    return jnp.stack([jnp.concatenate([tz[:, e0 + j] for j in range(NA_ROWS)], axis=-1)
                      for e0 in range(NA_ROWS)], axis=1)


def _na_latent_kernel(q_ref, k_ref, v_ref, ck_ref, cv_ref, bias_ref, o_ref, *, rows):
    rb = pl.program_id(2)
    span = NA_ROWS * GRID_W
    starts, biases = [], []
    for r in range(NA_QROWS):
        r_abs = rb * NA_QROWS + r
        rs = jnp.clip(r_abs - NA_ROWS // 2, 0, rows - NA_ROWS)
        starts.append(pl.multiple_of(rs * GRID_W, GRID_W))
        biases.append(rs - r_abs + NA_ROWS - 1)
    heads = [slice(hh * HEAD_DIM, (hh + 1) * HEAD_DIM) for hh in range(NA_HEADS)]
    scores = []
    for hh, cols in enumerate(heads):
        q = q_ref[:, cols]
        s_loc = [_dot_nt(q[r * GRID_W:(r + 1) * GRID_W], k_ref[pl.ds(starts[r], span), cols]) + bias_ref[hh, biases[r]]
                 for r in range(NA_QROWS)]
        scores.append([jnp.concatenate(s_loc, axis=0), _dot_nt(q, ck_ref[:, cols].astype(BF16))])
    stats = [_softmax_parts(sc, exp=jnp.exp2) for sc in scores]
    for cols, ((p_loc, p_ctx), l) in zip(heads, stats):
        p_loc = p_loc.astype(BF16)
        o_loc = [_dot(p_loc[r * GRID_W:(r + 1) * GRID_W], v_ref[pl.ds(starts[r], span), cols]) for r in range(NA_QROWS)]
        o = jnp.concatenate(o_loc, axis=0) + _dot(p_ctx.astype(BF16), cv_ref[:, cols].astype(BF16))
        o_ref[:, cols] = (o / l).astype(o_ref.dtype)


def na_latent(z, cache_k, cache_v, layer, bias, batch, seq):
    rows = seq // GRID_W
    assert rows % NA_QROWS == 0 and rows >= NA_ROWS and H_A % NA_HEADS == 0
    nrb = rows // NA_QROWS
    tq = NA_QROWS * GRID_W
    hw = NA_HEADS * HEAD_DIM
    n_ctx = cache_k.shape[2]
    ck = cache_k.reshape(batch, DEPTH, n_ctx, MIX_A)
    cv = cache_v.reshape(batch, DEPTH, n_ctx, MIX_A)
    ctx_spec = pl.BlockSpec((None, None, n_ctx, hw), lambda b, h, rb: (b, layer, 0, h))
    return pl.pallas_call(
        functools.partial(_na_latent_kernel, rows=rows),
        grid=(batch, H_A // NA_HEADS, nrb),
        in_specs=[
            pl.BlockSpec((tq, hw), lambda b, h, rb: (b * nrb + rb, OFF_QA // hw + h)),
            pl.BlockSpec((seq, hw), lambda b, h, rb: (b, OFF_KA // hw + h)),
            pl.BlockSpec((seq, hw), lambda b, h, rb: (b, OFF_VA // hw + h)),
            ctx_spec, ctx_spec,
            pl.BlockSpec((NA_HEADS, NA_ROWS, GRID_W, NA_ROWS * GRID_W), lambda b, h, rb: (h, 0, 0, 0)),
        ],
        out_specs=pl.BlockSpec((tq, hw), lambda b, h, rb: (b * nrb + rb, h)),
        out_shape=jax.ShapeDtypeStruct((batch * seq, MIX_A), BF16),
        compiler_params=_params("arbitrary", "arbitrary", "arbitrary"),
        name="na_latent",
    )(z, z, z, ck, cv, bias)


def _diff_latent_kernel(lam_ref, gain_ref, q_ref, k_ref, v_ref, ck_ref, cv_ref, o_ref, kt_ref, *, layer):
    lam, lam_init = _diff_lambda(lam_ref, layer)
    seq = k_ref.shape[0]

    @pl.when(pl.program_id(2) == 0)
    def _():
        kt_ref[:, :seq] = k_ref[...].T
        kt_ref[:, seq:] = ck_ref[...].astype(BF16).T

    cv = cv_ref[...].astype(BF16)
    nsub = q_ref.shape[0] // DIFF_TQ
    rows = [slice(u * DIFF_TQ, (u + 1) * DIFF_TQ) for u in range(nsub)]
    scores = [[[_dot(qh, kt_ref[:, :seq]), _dot(qh, kt_ref[:, seq:])] for qh in _split_halves(q_ref[r, :])]
              for r in rows]
    stats = [[_softmax_parts(sc, exp=jnp.exp2) for sc in su] for su in scores]
    for r, ((p0, l0), (p1, l1)) in zip(rows, stats):
        c = lam * l0 / l1
        wl = (p0[0] - p1[0] * c).astype(BF16)
        wc = (p0[1] - p1[1] * c).astype(BF16)
        o = (_dot(wl, v_ref[...]) + _dot(wc, cv)) / l0
        o_ref[r, :] = _diff_finish(o, gain_ref, lam_init).astype(o_ref.dtype)


def diff_latent(z, cache_k, cache_v, layer, lam, gain, batch, seq):
    tq = DIFF_TQ * DIFF_NSUB
    n_ctx = cache_k.shape[2]
    ck = cache_k.reshape(batch, DEPTH, n_ctx, MIX_B)
    cv = cache_v.reshape(batch, DEPTH, n_ctx, MIX_B)
    nqt = seq // tq
    ctx_spec = pl.BlockSpec((None, None, n_ctx, HEAD_DIM), lambda b, h, t: (b, layer, 0, h))
    return pl.pallas_call(
        functools.partial(_diff_latent_kernel, layer=layer),
        grid=(batch, H_B, nqt),
        in_specs=[
            pl.BlockSpec((4, DB_HALF), lambda b, h, t: (0, 0)),
            pl.BlockSpec((1, HEAD_DIM), lambda b, h, t: (0, 0)),
            pl.BlockSpec((tq, HEAD_DIM), lambda b, h, t: (b * nqt + t, OFF_QB // HEAD_DIM + h)),
            pl.BlockSpec((seq, HEAD_DIM), lambda b, h, t: (b, OFF_KB // HEAD_DIM + h)),
            pl.BlockSpec((seq, HEAD_DIM), lambda b, h, t: (b, OFF_VB // HEAD_DIM + h)),
            ctx_spec, ctx_spec,
        ],
        out_specs=pl.BlockSpec((tq, HEAD_DIM), lambda b, h, t: (b * nqt + t, h)),
        out_shape=jax.ShapeDtypeStruct((batch * seq, MIX_B), BF16),
        scratch_shapes=[pltpu.VMEM((HEAD_DIM, seq + n_ctx), BF16)],
        compiler_params=_params("arbitrary", "arbitrary", "arbitrary"),
        name="diff_latent",
    )(lam, gain.reshape(1, HEAD_DIM), z, z, z, ck, cv)


def _window_mask_table():
    i, j = np.arange(SW_SUB)[:, None], np.arange(SW_SPAN)[None, :]
    tabs = [np.where(np.abs(c * WINDOW_C + i - j) <= WINDOW_C, 0.0, NEG_INF) for c in range(3)]
    return jnp.asarray(np.stack(tabs), F32)


def _window_latent_kernel(sink_ref, mask_ref, q_ref, k_ref, v_ref, ck_ref, cv_ref, o_ref, *, seq):
    n = pl.program_id(1)
    t0 = pl.program_id(2) * SW_TQ
    ck = ck_ref[...].astype(BF16)
    cv = cv_ref[...].astype(BF16)
    sink = _sink_column(sink_ref, n, SW_SUB) * LOG2E
    nsub = SW_TQ // SW_SUB
    rows = [slice(u * SW_SUB, (u + 1) * SW_SUB) for u in range(nsub)]
    starts, scores = [], []
    for u in range(nsub):
        tu = t0 + u * SW_SUB
        ks = pl.multiple_of(jnp.clip(tu - WINDOW_C, 0, seq - SW_SPAN), WINDOW_C)
        case = (tu - ks) // WINDOW_C
        q = jnp.concatenate([q_ref[rows[u], g * HEAD_DIM:(g + 1) * HEAD_DIM] for g in range(G_C)], axis=0)
        s_loc = _dot_nt(q, k_ref[pl.ds(ks, SW_SPAN), :]).reshape(G_C, SW_SUB, SW_SPAN) + mask_ref[case][None]
        starts.append(ks)
        scores.append([s_loc.reshape(G_C * SW_SUB, SW_SPAN), _dot_nt(q, ck)])
    stats = [_softmax_parts(sc, extra=sink, exp=jnp.exp2) for sc in scores]
    for u, ((p_loc, p_ctx), l) in enumerate(stats):
        o = (_dot(p_loc.astype(BF16), v_ref[pl.ds(starts[u], SW_SPAN), :]) + _dot(p_ctx.astype(BF16), cv)) / l
        for g in range(G_C):
            o_ref[rows[u], g * HEAD_DIM:(g + 1) * HEAD_DIM] = o[g * SW_SUB:(g + 1) * SW_SUB].astype(o_ref.dtype)


def window_latent(z, cache_k, cache_v, layer, sink, batch, seq):
    assert seq % SW_TQ == 0 and seq >= SW_SPAN and SW_TQ % SW_SUB == 0
    n_ctx = cache_k.shape[2]
    ck = cache_k.reshape(batch, DEPTH, n_ctx, KVW_C)
    cv = cache_v.reshape(batch, DEPTH, n_ctx, KVW_C)
    nqt = seq // SW_TQ
    gw = G_C * HEAD_DIM
    ctx_spec = pl.BlockSpec((None, None, n_ctx, HEAD_DIM), lambda b, n, t: (b, layer, 0, n))
    return pl.pallas_call(
        functools.partial(_window_latent_kernel, seq=seq),
        grid=(batch, KV_C, nqt),
        in_specs=[
            pl.BlockSpec(memory_space=pltpu.SMEM),
            pl.BlockSpec((3, SW_SUB, SW_SPAN), lambda b, n, t: (0, 0, 0)),
            pl.BlockSpec((SW_TQ, gw), lambda b, n, t: (b * nqt + t, OFF_QC // gw + n)),
            pl.BlockSpec((seq, HEAD_DIM), lambda b, n, t: (b, OFF_KC // HEAD_DIM + n)),
            pl.BlockSpec((seq, HEAD_DIM), lambda b, n, t: (b, OFF_VC // HEAD_DIM + n)),
            ctx_spec, ctx_spec,
        ],
        out_specs=pl.BlockSpec((SW_TQ, gw), lambda b, n, t: (b * nqt + t, n)),
        out_shape=jax.ShapeDtypeStruct((batch * seq, MIX_C), BF16),
        compiler_params=_params("arbitrary", "arbitrary", "arbitrary"),
        name="window_latent",
    )(sink, _window_mask_table(), z, z, z, ck, cv)


def _prompt_row(tm, i):
    return 0


def kernel(x_prompt, x_sample, cache_a_k, cache_a_v, cache_b_k, cache_b_v, cache_c_k, cache_c_v, c, c_ctx,
           w_ada, b_ada, w_in, rpb_a, lam_b, subln_b, sink_c, w_branch, w_out, ln_g, ln_b, w_gate_up, w_down):
    bp, tp, _ = x_prompt.shape
    bs, ts, _ = x_sample.shape

    def sample_row(tm, i):
        return 1 + (i * tm) // ts

    streams = {
        "p": dict(x=x_prompt.reshape(bp * tp, D_MODEL), rowfn=_prompt_row),
        "s": dict(x=x_sample.reshape(bs * ts, D_MODEL), rowfn=sample_row),
    }
    cc = jnp.concatenate([c_ctx[None, :], c], axis=0)
    cc = jnp.pad(cc, ((0, (-cc.shape[0]) % 8), (0, 0)))
    mod = ada_modulation_all(cc, w_ada, b_ada)

    for st in streams.values():
        st["h"] = modulate_cast(st["x"], mod, 0, 0, st["rowfn"])

    new_kv = [[] for _ in range(6)]
    for l in range(DEPTH):
        w_qkv = cast_weight(w_in, l, 0, QKV_W)
        w_g = cast_weight(w_in, l, QKV_W)
        w_br = cast_weight(w_branch, l)
        w_o = cast_weight(w_out, l)
        w_gu = cast_weight(w_gate_up, l)
        w_dn = cast_weight(w_down, l)
        na_bias = _na_bias_table(rpb_a[l])

        sp = streams["p"]
        zp = matmul(sp["h"], w_qkv, F32, PROJ_TILE, PROJ_TILE, name="qkv_context")
        for lst, off, wdt, shp in (
                (new_kv[0], OFF_KA, MIX_A, (H_A, HEAD_DIM)), (new_kv[1], OFF_VA, MIX_A, (H_A, HEAD_DIM)),
                (new_kv[2], OFF_KB, MIX_B, (H_B, 2, DB_HALF)), (new_kv[3], OFF_VB, MIX_B, (H_B, HEAD_DIM)),
                (new_kv[4], OFF_KC, KVW_C, (KV_C, HEAD_DIM)), (new_kv[5], OFF_VC, KVW_C, (KV_C, HEAD_DIM))):
            lst.append(zp[:, off:off + wdt].reshape((bp, tp) + shp))
        op = context_attention(zp, tp, lam_b[l], subln_b[l], sink_c[l], l)
        sp["o"] = ((op, 0), (op, MIX_A // MIX_B), (op, (MIX_A + MIX_B) // MIX_C))

        ss = streams["s"]
        zs = qkv_latent(ss["h"], w_qkv, ts)
        oa = na_latent(zs, cache_a_k, cache_a_v, l, na_bias, bs, ts)
        ob = diff_latent(zs, cache_b_k, cache_b_v, l, lam_b[l], subln_b[l], bs, ts)
        oc = window_latent(zs, cache_c_k, cache_c_v, l, sink_c[l], bs, ts)
        ss["o"] = ((oa, 0), (ob, 0), (oc, 0))

        for st in streams.values():
            gates = matmul(st["h"], w_g, BF16, PROJ_TILE, PROJ_TILE, sigmoid=True, name="branch_gates")
            u = merge_branches(st["o"], gates, w_br)
            v = matmul_residual(u, w_o, st["x"], mod, l, 0, st["rowfn"], PROJ_TILE, PROJ_TILE, "mixer_out")
            st["x"], h2 = post_norm(v, mod, ln_g[l, 0], ln_b[l, 0], st["rowfn"], next_mod=(l, 1))
            a = swiglu_up(h2, w_gu)
            v = matmul_residual(a, w_dn, st["x"], mod, l, 1, st["rowfn"], DOWN_TILE, DOWN_TILE, "ffn_down")
            nxt = (l + 1, 0) if l + 1 < DEPTH else None
            st["x"], st["h"] = post_norm(v, mod, ln_g[l, 1], ln_b[l, 1], st["rowfn"], next_mod=nxt)

    outs = [jnp.stack(lst, axis=1) for lst in new_kv]
    return (streams["p"]["x"].reshape(bp, tp, D_MODEL), streams["s"]["x"].reshape(bs, ts, D_MODEL), *outs)
```
